```python
import math
import jax, jax.numpy as jnp
from jax import lax
import numpy as np

D_MODEL = 2048
BATCH = 4
SEQ = 2048
DEPTH = 1
DEC_BATCH = 128
DEC_SEQ = 4
PAST_LEN = 2048
PAGE_SIZE = 128

HEAD_DIM = 128
N_HEADS_TOTAL = D_MODEL // HEAD_DIM
N_HEADS_B = N_HEADS_TOTAL // 2
N_GROUPS_A = N_HEADS_TOTAL - N_HEADS_B
W_B = N_HEADS_B * HEAD_DIM
W_A = N_GROUPS_A * HEAD_DIM
D_MIX = W_A + W_B
CHUNK = 128
MOBA_BLOCK = 256
MOBA_TOPK = 3
MOBA_GATHER_ROWS = 64
NUM_BUCKETS = 32
MAX_DISTANCE = 1024
CONV_W = 3
D_FF = ((8 * D_MODEL // 3 + 127) // 128) * 128
ALPHA = (2 * DEPTH) ** 0.25
BETA = (8 * DEPTH) ** -0.25
LN_EPS = 1e-5
NEG_INF = -1e30

kernel_name = 'hymba_gmlp_moba_convffn_decoder_step'


def layer_norm(x, g, b):
    xf = x.astype(jnp.float32)
    mu = jnp.mean(xf, axis=-1, keepdims=True)
    var = jnp.mean(jnp.square(xf - mu), axis=-1, keepdims=True)
    return ((xf - mu) * lax.rsqrt(var + LN_EPS) * g.astype(jnp.float32) + b.astype(jnp.float32)).astype(x.dtype)


def rms_norm(x, g):
    xf = x.astype(jnp.float32)
    return (xf * lax.rsqrt(jnp.mean(xf * xf, axis=-1, keepdims=True) + LN_EPS) * g.astype(jnp.float32)).astype(x.dtype)


def t5_bucket(rel):
    n = jnp.maximum(rel, 0)
    max_exact = NUM_BUCKETS // 2
    nf = jnp.maximum(n, 1).astype(jnp.float32)
    large = max_exact + (jnp.log(nf / max_exact) / math.log(MAX_DISTANCE / max_exact)
                         * (NUM_BUCKETS - max_exact)).astype(jnp.int32)
    large = jnp.minimum(large, NUM_BUCKETS - 1)
    return jnp.where(n < max_exact, n, large)


def query_chunk(n, t):
    qc = max(1, MOBA_GATHER_ROWS // n)
    while t % qc:
        qc -= 1
    return qc


def moba_attention(q, k_pad, v_pad, pos0, rel_bias):
    n, t, h, dh = q.shape
    nb = k_pad.shape[1] // MOBA_BLOCK
    k_blk = k_pad.reshape(n, nb, MOBA_BLOCK, h, dh)
    v_blk = v_pad.reshape(n, nb, MOBA_BLOCK, h, dh)
    k_mean = jnp.mean(k_blk.astype(jnp.float32), axis=2)
    topk = min(MOBA_TOPK, nb)
    qc_len = query_chunk(n, t)
    n_chunks = t // qc_len
    qs = (q * HEAD_DIM ** -0.5).reshape(n, n_chunks, qc_len, h, dh).transpose(1, 0, 3, 2, 4)
    pos = (pos0 + jnp.arange(t, dtype=jnp.int32)).reshape(n_chunks, qc_len)
    n_idx = jnp.arange(n)[:, None, None, None]
    h_idx = jnp.arange(h)[None, :, None, None]
    bias_hb = rel_bias.astype(jnp.float32).T
    blk_ids = jnp.arange(nb, dtype=jnp.int32)
    offs = jnp.arange(MOBA_BLOCK, dtype=jnp.int32)

    def attend(args):
        qc, p = args
        q_blk = p // MOBA_BLOCK
        score = jnp.einsum('nhtd,njhd->nhtj', qc.astype(jnp.float32), k_mean)
        fully_past = blk_ids[None, :] < q_blk[:, None]
        score = jnp.where(fully_past, score, NEG_INF)
        _, sel = lax.top_k(score, topk)
        sel_ok = sel < q_blk[:, None]
        own = jnp.broadcast_to(q_blk[:, None], (n, h, qc_len, 1)).astype(sel.dtype)
        blocks = jnp.concatenate([sel, own], axis=-1)
        blk_ok = jnp.concatenate([sel_ok, jnp.ones(own.shape, dtype=bool)], axis=-1)
        kg = k_blk[n_idx, blocks, :, h_idx]
        vg = v_blk[n_idx, blocks, :, h_idx]
        logits = jnp.einsum('nhtd,nhtsbd->nhtsb', qc, kg).astype(jnp.float32)
        rel = p[:, None, None] - (blocks[..., None] * MOBA_BLOCK + offs)
        bias = bias_hb[h_idx[..., None], t5_bucket(rel)]
        ok = blk_ok[..., None] & (rel >= 0)
        logits = jnp.where(ok, logits + bias, NEG_INF)
        shp = logits.shape
        prob = jax.nn.softmax(logits.reshape(shp[:3] + (-1,)), axis=-1).reshape(shp)
        return jnp.einsum('nhtsb,nhtsbd->nhtd', prob.astype(vg.dtype), vg)

    out = lax.map(attend, (qs, pos))
    return out.transpose(1, 0, 3, 2, 4).reshape(n, t, h * dh)


def chunk_spatial_gate(u, v, w_s, b_s):
    n, t, g, c = v.shape
    pad = (-t) % CHUNK
    nc = (t + pad) // CHUNK
    vp = jnp.pad(v, ((0, 0), (0, pad), (0, 0), (0, 0))).reshape(n, nc, CHUNK, g, c)
    w = w_s * jnp.tril(jnp.ones((CHUNK, CHUNK), w_s.dtype))
    mixed = jnp.einsum('gts,ncsgd->nctgd', w, vp) + b_s.T[None, None, :, :, None]
    return u * mixed.reshape(n, nc * CHUNK, g, c)[:, :t]


def decoder_layer(x, c, past_k, past_v, past_up, p):
    n, t, _ = x.shape
    pos0 = past_k.shape[1]
    mod = jnp.dot(jax.nn.silu(c), p['w_ada']) + p['b_ada']
    shift1, scale1, gate1, shift2, scale2, gate2 = jnp.split(mod[:, None, :], 6, axis=-1)
    h = x * (1 + scale1) + shift1
    proj = jnp.dot(h, p['w_in'])
    q, k, v, u_a, v_a = jnp.split(proj, [W_B, 2 * W_B, 3 * W_B, 3 * W_B + W_A], axis=-1)
    heads = (n, t, N_HEADS_B, HEAD_DIM)
    q, k, v = q.reshape(heads), k.reshape(heads), v.reshape(heads)
    pad = (-(pos0 + t)) % MOBA_BLOCK
    tail = jnp.zeros((n, pad, N_HEADS_B, HEAD_DIM), k.dtype)
    k_all = jnp.concatenate([past_k.astype(k.dtype), k, tail], axis=1)
    v_all = jnp.concatenate([past_v.astype(v.dtype), v, tail], axis=1)
    attn = moba_attention(q, k_all, v_all, pos0, p['rel_bias'])
    u_a = jax.nn.gelu(u_a)
    v_a = layer_norm(jax.nn.gelu(v_a), p['ln_sgu_g'], p['ln_sgu_b'])
    groups = (n, t, N_GROUPS_A, HEAD_DIM)
    gm = chunk_spatial_gate(u_a.reshape(groups), v_a.reshape(groups), p['w_s'], p['b_s']).reshape(n, t, W_A)
    mixed = jnp.concatenate([rms_norm(gm, p['norm_a_g']), rms_norm(attn, p['norm_b_g'])], axis=-1)
    x = layer_norm(ALPHA * x + gate1 * jnp.dot(mixed, p['w_out']), p['ln1_g'], p['ln1_b'])
    h = x * (1 + scale2) + shift2
    up = jnp.dot(h, p['w_up'])
    full = jnp.concatenate([past_up.astype(up.dtype), up], axis=1)
    conv = p['b_conv'] + sum(full[:, i:i + t] * p['w_conv'][i] for i in range(CONV_W))
    a, b = jnp.split(conv, 2, axis=-1)
    ff = jnp.dot(jax.nn.silu(a) * b, p['w_down'])
    y = layer_norm(ALPHA * x + gate2 * ff, p['ln2_g'], p['ln2_b'])
    return y, k, v, v_a, full[:, t:]


def setup_inputs(seed: int = 0) -> dict:
    key = jax.random.key(seed)
    ks = jax.random.split(key, 32)
    f32 = jnp.float32
    n_pages = PAST_LEN // PAGE_SIZE
    n_used = DEC_BATCH * n_pages
    n_phys = n_used + max(1, n_used // 4)

    def nrm(k, shape, s):
        return jax.random.normal(k, shape, f32) * s

    page_table = jax.random.permutation(ks[0], n_phys)[:n_used].reshape(DEC_BATCH, n_pages).astype(jnp.int32)
    return {
        'x_prompt': nrm(ks[1], (BATCH, SEQ, D_MODEL), 1.0),
        'x_sample': nrm(ks[2], (DEC_BATCH, DEC_SEQ, D_MODEL), 1.0),
        'cache_k': nrm(ks[3], (DEPTH, n_phys, PAGE_SIZE, N_HEADS_B, HEAD_DIM), 1.0),
        'cache_v': nrm(ks[4], (DEPTH, n_phys, PAGE_SIZE, N_HEADS_B, HEAD_DIM), 1.0),
        'state_ffn_conv': nrm(ks[5], (DEPTH, DEC_BATCH, CONV_W - 1, 2 * D_FF), 1.0),
        'page_table': page_table,
        'c_prompt': nrm(ks[6], (BATCH, D_MODEL), 1.0),
        'c_sample': nrm(ks[7], (DEC_BATCH, D_MODEL), 1.0),
        'w_ada': nrm(ks[8], (DEPTH, D_MODEL, 6 * D_MODEL), 0.5 * D_MODEL ** -0.5),
        'b_ada': nrm(ks[9], (DEPTH, 6 * D_MODEL), 0.02),
        'w_in': nrm(ks[10], (DEPTH, D_MODEL, 3 * W_B + 2 * W_A), D_MODEL ** -0.5),
        'ln_sgu_g': 1.0 + nrm(ks[11], (DEPTH, W_A), 0.1),
        'ln_sgu_b': nrm(ks[12], (DEPTH, W_A), 0.02),
        'w_s': nrm(ks[13], (DEPTH, N_GROUPS_A, CHUNK, CHUNK), CHUNK ** -0.5),
        'b_s': 1.0 + nrm(ks[14], (DEPTH, N_GROUPS_A, CHUNK), 0.1),
        'rel_bias': nrm(ks[15], (NUM_BUCKETS, N_HEADS_B), 0.5),
        'norm_a_g': 1.0 + nrm(ks[16], (DEPTH, W_A), 0.1),
        'norm_b_g': 1.0 + nrm(ks[17], (DEPTH, W_B), 0.1),
        'w_out': nrm(ks[18], (DEPTH, D_MIX, D_MODEL), BETA * D_MIX ** -0.5),
        'ln1_g': 1.0 + nrm(ks[19], (DEPTH, D_MODEL), 0.1),
        'ln1_b': nrm(ks[20], (DEPTH, D_MODEL), 0.02),
        'w_up': nrm(ks[21], (DEPTH, D_MODEL, 2 * D_FF), D_MODEL ** -0.5),
        'w_conv': nrm(ks[22], (DEPTH, CONV_W, 2 * D_FF), CONV_W ** -0.5),
        'b_conv': nrm(ks[23], (DEPTH, 2 * D_FF), 0.02),
        'w_down': nrm(ks[24], (DEPTH, D_FF, D_MODEL), BETA * D_FF ** -0.5),
        'ln2_g': 1.0 + nrm(ks[25], (DEPTH, D_MODEL), 0.1),
        'ln2_b': nrm(ks[26], (DEPTH, D_MODEL), 0.02),
    }


def reference(x_prompt, x_sample, cache_k, cache_v, state_ffn_conv, page_table, c_prompt, c_sample,
              w_ada, b_ada, w_in, ln_sgu_g, ln_sgu_b, w_s, b_s, rel_bias, norm_a_g, norm_b_g, w_out,
              ln1_g, ln1_b, w_up, w_conv, b_conv, w_down, ln2_g, ln2_b):
    bp = x_prompt.shape[0]
    db, n_pages = page_table.shape
    empty = jnp.zeros((bp, 0, N_HEADS_B, HEAD_DIM), x_prompt.dtype)
    up0 = jnp.zeros((bp, CONV_W - 1, 2 * D_FF), x_prompt.dtype)
    y_prompt, y_sample = x_prompt, x_sample
    k_p, v_p, conv_p, k_s, v_s, chunk_v_s, conv_s = [], [], [], [], [], [], []
    for l in range(DEPTH):
        p = {'w_ada': w_ada[l], 'b_ada': b_ada[l], 'w_in': w_in[l], 'ln_sgu_g': ln_sgu_g[l],
             'ln_sgu_b': ln_sgu_b[l], 'w_s': w_s[l], 'b_s': b_s[l], 'rel_bias': rel_bias,
             'norm_a_g': norm_a_g[l], 'norm_b_g': norm_b_g[l], 'w_out': w_out[l], 'ln1_g': ln1_g[l],
             'ln1_b': ln1_b[l], 'w_up': w_up[l], 'w_conv': w_conv[l], 'b_conv': b_conv[l],
             'w_down': w_down[l], 'ln2_g': ln2_g[l], 'ln2_b': ln2_b[l]}
        past_k = cache_k[l][page_table].reshape(db, n_pages * PAGE_SIZE, N_HEADS_B, HEAD_DIM)
        past_v = cache_v[l][page_table].reshape(db, n_pages * PAGE_SIZE, N_HEADS_B, HEAD_DIM)
        y_prompt, kp, vp, _, cp = decoder_layer(y_prompt, c_prompt, empty, empty, up0, p)
        y_sample, ks_, vs_, chv, cs = decoder_layer(y_sample, c_sample, past_k, past_v, state_ffn_conv[l], p)
        k_p.append(kp); v_p.append(vp); conv_p.append(cp)
        k_s.append(ks_); v_s.append(vs_); chunk_v_s.append(chv); conv_s.append(cs)
    return (y_prompt, y_sample, jnp.stack(k_p), jnp.stack(v_p), jnp.stack(conv_p),
            jnp.stack(k_s), jnp.stack(v_s), jnp.stack(chunk_v_s), jnp.stack(conv_s))
```

```python
import functools

import numpy as np
import jax
import jax.numpy as jnp
from jax import lax
from jax.experimental import pallas as pl
from jax.experimental.pallas import tpu as pltpu

F32 = jnp.float32
BF16 = jnp.bfloat16

D_MODEL = 2048
HEAD_DIM = 128
N_HEADS_B = 8
N_GROUPS_A = 8
W_B = N_HEADS_B * HEAD_DIM
W_A = N_GROUPS_A * HEAD_DIM
CHUNK = 128
MOBA_BLOCK = 256
MOBA_TOPK = 3
NUM_BUCKETS = 32
MAX_DISTANCE = 1024
CONV_W = 3
D_FF = 5504
PAGE_SIZE = 128
ALPHA = 2.0 ** 0.25
LN_EPS = 1e-5
NEG_INF = -1e30

FF_TILE = 512
D_FF_PAD = ((D_FF + FF_TILE - 1) // FF_TILE) * FF_TILE
N_FF_TILES = D_FF_PAD // FF_TILE
SUBLANES = 8
VMEM_LIMIT_BYTES = 56 * 1024 * 1024

_NT = (((1,), (1,)), ((), ()))


def _params(n_axes):
    return pltpu.CompilerParams(dimension_semantics=("arbitrary",) * n_axes,
                                vmem_limit_bytes=VMEM_LIMIT_BYTES)


def _layer_norm(x, g, b):
    mu = jnp.mean(x, axis=-1, keepdims=True)
    xc = x - mu
    var = jnp.mean(xc * xc, axis=-1, keepdims=True)
    return xc * lax.rsqrt(var + LN_EPS) * g + b


def _rms_norm(x, g):
    return x * lax.rsqrt(jnp.mean(x * x, axis=-1, keepdims=True) + LN_EPS) * g


def _split_bf16(x):
    hi = x.astype(BF16)
    lo = (x - hi.astype(F32)).astype(BF16)
    return hi, lo


def _dot(a, b):
    return jnp.dot(a, b, preferred_element_type=F32)


def _dot_nt(a, b):
    return lax.dot_general(a, b, _NT, preferred_element_type=F32)


def _block_scores(q_hi, q_lo, km):
    km_hi, km_lo = _split_bf16(km)
    return _dot_nt(q_hi, km_hi) + _dot_nt(q_hi, km_lo) + _dot_nt(q_lo, km_hi)


def _topk_select(s, n_valid, axis):
    idx = lax.broadcasted_iota(jnp.int32, s.shape, axis)
    rank = jnp.zeros(s.shape, F32)
    for i in range(n_valid):
        si = lax.slice_in_dim(s, i, i + 1, axis=axis)
        beats = (si > s) | ((si == s) & (i < idx))
        rank = rank + beats.astype(F32)
    return ((rank < MOBA_TOPK) & (idx < n_valid)).astype(F32)


def _ada_kernel(c_ref, w_ref, b_ref, o_ref):
    c = c_ref[...]
    a = jax.nn.silu(c).astype(BF16)
    o_ref[...] = _dot(a, w_ref[...].astype(BF16)) + b_ref[...]


def _ada(c_all, w_ada, b_ada):
    rows = c_all.shape[0]
    n_out = w_ada.shape[1]
    tn = 512
    return pl.pallas_call(
        _ada_kernel,
        grid=(n_out // tn,),
        in_specs=[pl.BlockSpec((rows, D_MODEL), lambda j: (0, 0)),
                  pl.BlockSpec((D_MODEL, tn), lambda j: (0, j)),
                  pl.BlockSpec((1, tn), lambda j: (0, j))],
        out_specs=pl.BlockSpec((rows, tn), lambda j: (0, j)),
        out_shape=jax.ShapeDtypeStruct((rows, n_out), F32),
        compiler_params=_params(1),
        name="ada",
    )(c_all, w_ada, b_ada.reshape(1, n_out))


def _proj_prompt_kernel(x_ref, sc_ref, sh_ref, w_ref, lng_ref, lnb_ref, ws_ref, bs_ref, nag_ref,
                        qhi_ref, qlo_ref, k_ref, v_ref, kbf_ref, vbf_ref, km_ref, gm_ref,
                        h_scr, u_scr, gm_scr):
    j = pl.program_id(1)
    tm = x_ref.shape[0]

    @pl.when(j == 0)
    def _():
        h_scr[...] = (x_ref[...] * (1.0 + sc_ref[...]) + sh_ref[...]).astype(BF16)

    acc = _dot(h_scr[...], w_ref[...])

    @pl.when(j == 0)
    def _():
        q_hi, q_lo = _split_bf16(acc * (HEAD_DIM ** -0.5))
        qhi_ref[...] = q_hi
        qlo_ref[...] = q_lo

    @pl.when(j == 1)
    def _():
        k_ref[...] = acc
        kbf_ref[...] = acc.astype(BF16)
        km_ref[...] = jnp.mean(acc.reshape(tm // MOBA_BLOCK, MOBA_BLOCK, W_B), axis=1)

    @pl.when(j == 2)
    def _():
        v_ref[...] = acc
        vbf_ref[...] = acc.astype(BF16)

    @pl.when(j == 3)
    def _():
        u_scr[...] = jax.nn.gelu(acc)

    @pl.when(j == 4)
    def _():
        va = _layer_norm(jax.nn.gelu(acc), lng_ref[...], lnb_ref[...]).astype(BF16)
        n_chunks = tm // CHUNK
        row = lax.broadcasted_iota(jnp.int32, (CHUNK, CHUNK), 0)
        col = lax.broadcasted_iota(jnp.int32, (CHUNK, CHUNK), 1)
        for g in range(N_GROUPS_A):
            gs = slice(g * HEAD_DIM, (g + 1) * HEAD_DIM)
            w_g = jnp.where(row >= col, ws_ref[g], 0.0).astype(BF16)
            v_g = jnp.concatenate([va[c * CHUNK:(c + 1) * CHUNK, gs] for c in range(n_chunks)], axis=1)
            m_g = _dot(w_g, v_g)
            for c in range(n_chunks):
                rs = slice(c * CHUNK, (c + 1) * CHUNK)
                mixed = m_g[:, c * HEAD_DIM:(c + 1) * HEAD_DIM] + bs_ref[g]
                gm_scr[rs, gs] = u_scr[rs, gs] * mixed
        gm_ref[...] = _rms_norm(gm_scr[...], nag_ref[...]).astype(BF16)


def _proj_prompt(x, mod_p, w_in, ln_g, ln_b, w_s, bs_full, norm_a_g, *, tm, seq):
    rows = x.shape[0]
    tiles_per_seq = seq // tm
    n_col = w_in.shape[1] // W_B
    row_blk = lambda i, j: (i, 0)
    vec = lambda i, j: (0, 0)
    out_shape = (
        jax.ShapeDtypeStruct((rows, W_B), BF16),
        jax.ShapeDtypeStruct((rows, W_B), BF16),
        jax.ShapeDtypeStruct((rows, W_B), F32),
        jax.ShapeDtypeStruct((rows, W_B), F32),
        jax.ShapeDtypeStruct((rows, W_B), BF16),
        jax.ShapeDtypeStruct((rows, W_B), BF16),
        jax.ShapeDtypeStruct((rows // tm, tm // MOBA_BLOCK, W_B), F32),
        jax.ShapeDtypeStruct((rows, W_A), BF16),
    )
    out_specs = (
        pl.BlockSpec((tm, W_B), row_blk), pl.BlockSpec((tm, W_B), row_blk),
        pl.BlockSpec((tm, W_B), row_blk), pl.BlockSpec((tm, W_B), row_blk),
        pl.BlockSpec((tm, W_B), row_blk), pl.BlockSpec((tm, W_B), row_blk),
        pl.BlockSpec((None, tm // MOBA_BLOCK, W_B), lambda i, j: (i, 0, 0)),
        pl.BlockSpec((tm, W_A), row_blk),
    )
    return pl.pallas_call(
        _proj_prompt_kernel,
        grid=(rows // tm, n_col),
        in_specs=[pl.BlockSpec((tm, D_MODEL), row_blk),
                  pl.BlockSpec((None, 1, D_MODEL), lambda i, j: (i // tiles_per_seq, 0, 1)),
                  pl.BlockSpec((None, 1, D_MODEL), lambda i, j: (i // tiles_per_seq, 0, 0)),
                  pl.BlockSpec((D_MODEL, W_B), lambda i, j: (0, j)),
                  pl.BlockSpec((1, W_A), vec), pl.BlockSpec((1, W_A), vec),
                  pl.BlockSpec((N_GROUPS_A, CHUNK, CHUNK), lambda i, j: (0, 0, 0)),
                  pl.BlockSpec((N_GROUPS_A, CHUNK, HEAD_DIM), lambda i, j: (0, 0, 0)),
                  pl.BlockSpec((1, W_A), vec)],
        out_specs=out_specs,
        out_shape=out_shape,
        scratch_shapes=[pltpu.VMEM((tm, D_MODEL), BF16), pltpu.VMEM((tm, W_A), F32),
                        pltpu.VMEM((tm, W_A), F32)],
        compiler_params=_params(2),
        name="proj_prompt",
    )(x, mod_p, mod_p, w_in, ln_g, ln_b, w_s, bs_full, norm_a_g)


def _proj_sample_kernel(x_ref, sc_ref, sh_ref, w_ref, lng_ref, lnb_ref,
                        q_ref, k_ref, v_ref, u_ref, va_ref, h_scr):
    j = pl.program_id(1)

    @pl.when(j == 0)
    def _():
        h_scr[...] = (x_ref[...] * (1.0 + sc_ref[...]) + sh_ref[...]).astype(BF16)

    acc = _dot(h_scr[...], w_ref[...])

    @pl.when(j == 0)
    def _():
        q_ref[...] = acc * (HEAD_DIM ** -0.5)

    @pl.when(j == 1)
    def _():
        k_ref[...] = acc

    @pl.when(j == 2)
    def _():
        v_ref[...] = acc

    @pl.when(j == 3)
    def _():
        u_ref[...] = jax.nn.gelu(acc)

    @pl.when(j == 4)
    def _():
        va_ref[...] = _layer_norm(jax.nn.gelu(acc), lng_ref[...], lnb_ref[...])


def _proj_sample(x2d, mod_s, w_in, ln_g, ln_b, *, n_seq, n_tok):
    n_col = w_in.shape[1] // W_B
    tok_blk = lambda t, j: (0, t)
    vec = lambda t, j: (0, 0)
    out = jax.ShapeDtypeStruct((n_seq, n_tok * W_B), F32)
    return pl.pallas_call(
        _proj_sample_kernel,
        grid=(n_tok, n_col),
        in_specs=[pl.BlockSpec((n_seq, D_MODEL), tok_blk),
                  pl.BlockSpec((n_seq, D_MODEL), lambda t, j: (0, 1)),
                  pl.BlockSpec((n_seq, D_MODEL), lambda t, j: (0, 0)),
                  pl.BlockSpec((D_MODEL, W_B), lambda t, j: (0, j)),
                  pl.BlockSpec((1, W_A), vec), pl.BlockSpec((1, W_A), vec)],
        out_specs=tuple(pl.BlockSpec((n_seq, W_B), tok_blk) for _ in range(5)),
        out_shape=(out,) * 5,
        scratch_shapes=[pltpu.VMEM((n_seq, D_MODEL), BF16)],
        compiler_params=_params(2),
        name="proj_sample",
    )(x2d, mod_s, mod_s, w_in, ln_g, ln_b)


def _attn_prompt_kernel(qhi_ref, qlo_ref, k_ref, v_ref, km_ref, rev_ref, o_ref, bias_scr):
    n_blocks = k_ref.shape[0] // MOBA_BLOCK

    @pl.when(pl.program_id(1) == 0)
    def _():
        for d in range(n_blocks):
            r = jnp.broadcast_to(rev_ref[d:d + 1, :], (MOBA_BLOCK, 2 * MOBA_BLOCK))
            rolled = pltpu.roll(r, 0, 1, stride=1, stride_axis=0)
            bias_scr[d] = rolled[:, MOBA_BLOCK:]

    km = km_ref[...]
    row = lax.broadcasted_iota(jnp.int32, (MOBA_BLOCK, MOBA_BLOCK), 0)
    col = lax.broadcasted_iota(jnp.int32, (MOBA_BLOCK, MOBA_BLOCK), 1)
    for qb in range(n_blocks):
        qs = slice(qb * MOBA_BLOCK, (qb + 1) * MOBA_BLOCK)
        n_keys = (qb + 1) * MOBA_BLOCK
        q_hi = qhi_ref[qs, :]
        logits = _dot_nt(q_hi, k_ref[0:n_keys, :])
        sel = None
        if qb > MOBA_TOPK:
            sel = _topk_select(_block_scores(q_hi, qlo_ref[qs, :], km), qb, axis=1)
        pieces = []
        for jb in range(qb + 1):
            lj = logits[:, jb * MOBA_BLOCK:(jb + 1) * MOBA_BLOCK] + bias_scr[qb - jb]
            if jb == qb:
                lj = jnp.where(row >= col, lj, NEG_INF)
            elif sel is not None:
                lj = jnp.where(sel[:, jb:jb + 1] > 0.0, lj, NEG_INF)
            pieces.append(lj)
        lg = jnp.concatenate(pieces, axis=1) if len(pieces) > 1 else pieces[0]
        m = jnp.max(lg, axis=-1, keepdims=True)
        p = jnp.exp(lg - m)
        z = jnp.sum(p, axis=-1, keepdims=True)
        o_ref[qs, :] = _dot(p.astype(BF16), v_ref[0:n_keys, :]) / z


def _attn_prompt(q_hi, q_lo, k_bf, v_bf, k_mean, rev_tab, *, n_seq, seq):
    n_blocks = seq // MOBA_BLOCK
    qkv = pl.BlockSpec((seq, HEAD_DIM), lambda h, n: (n, h))
    return pl.pallas_call(
        _attn_prompt_kernel,
        grid=(N_HEADS_B, n_seq),
        in_specs=[qkv, qkv, qkv, qkv,
                  pl.BlockSpec((None, n_blocks, HEAD_DIM), lambda h, n: (n, 0, h)),
                  pl.BlockSpec((None, n_blocks, 2 * MOBA_BLOCK), lambda h, n: (h, 0, 0))],
        out_specs=pl.BlockSpec((seq, HEAD_DIM), lambda h, n: (n, h)),
        out_shape=jax.ShapeDtypeStruct((n_seq * seq, W_B), F32),
        scratch_shapes=[pltpu.VMEM((n_blocks, MOBA_BLOCK, MOBA_BLOCK), F32)],
        compiler_params=_params(2),
        name="attn_prompt",
    )(q_hi, q_lo, k_bf, v_bf, k_mean, rev_tab)


def _attn_sample_kernel(n_pages, pt_ref, q_ref, kn_ref, vn_ref, bias_ref, biasn_ref, *refs):
    k_pages = refs[:n_pages]
    v_pages = refs[n_pages:2 * n_pages]
    o_ref = refs[2 * n_pages]
    n_tok = q_ref.shape[0]
    rows = N_HEADS_B * n_tok
    pages_per_block = MOBA_BLOCK // PAGE_SIZE
    n_blocks = n_pages // pages_per_block

    q = q_ref[...]
    q_rep = jnp.concatenate([q] * N_HEADS_B, axis=0)
    r_head = lax.broadcasted_iota(jnp.int32, (rows, W_B), 0) // n_tok
    c_head = lax.broadcasted_iota(jnp.int32, (rows, W_B), 1) // HEAD_DIM
    q_bd = jnp.where(r_head == c_head, q_rep, 0.0)
    q_hi, q_lo = _split_bf16(q_bd)

    logit_pages = []
    k_sums = []
    for p in range(n_pages):
        kp = k_pages[p][...]
        logit_pages.append(_dot_nt(q_hi, kp.astype(BF16)))
        k_sums.append(jnp.sum(kp, axis=0, keepdims=True))
    km = jnp.concatenate(
        [sum(k_sums[b * pages_per_block:(b + 1) * pages_per_block]) for b in range(n_blocks)],
        axis=0) * (1.0 / MOBA_BLOCK)
    sel = _topk_select(_block_scores(q_hi, q_lo, km), n_blocks, axis=1)

    lg = jnp.concatenate(logit_pages, axis=1) + bias_ref[...]
    blk = lax.broadcasted_iota(jnp.int32, lg.shape, 1) // MOBA_BLOCK
    keep = jnp.zeros(lg.shape, F32)
    for b in range(n_blocks):
        keep = jnp.where(blk == b, sel[:, b:b + 1], keep)
    lg = jnp.where(keep > 0.0, lg, NEG_INF)

    ln = _dot_nt(q_hi, kn_ref[...].astype(BF16))
    t_idx = lax.broadcasted_iota(jnp.int32, ln.shape, 0) % n_tok
    s_idx = lax.broadcasted_iota(jnp.int32, ln.shape, 1)
    ln = jnp.where(s_idx <= t_idx, ln + biasn_ref[...], NEG_INF)

    m = jnp.maximum(jnp.max(lg, axis=-1, keepdims=True), jnp.max(ln, axis=-1, keepdims=True))
    p_past = jnp.exp(lg - m)
    p_new = jnp.exp(ln - m)
    z = jnp.sum(p_past, axis=-1, keepdims=True) + jnp.sum(p_new, axis=-1, keepdims=True)
    p_past = p_past.astype(BF16)
    acc = _dot(p_new.astype(BF16), vn_ref[...].astype(BF16))
    for p in range(n_pages):
        acc = acc + _dot(p_past[:, p * PAGE_SIZE:(p + 1) * PAGE_SIZE], v_pages[p][...].astype(BF16))
    acc = acc / z
    o_ref[...] = jnp.concatenate(
        [acc[h * n_tok:(h + 1) * n_tok, h * HEAD_DIM:(h + 1) * HEAD_DIM] for h in range(N_HEADS_B)], axis=1)


def _attn_sample(page_table, q, k_new, v_new, bias_past, bias_new, cache_k, cache_v):
    n_seq, n_pages = page_table.shape
    n_tok = q.shape[1]
    rows = N_HEADS_B * n_tok
    past = n_pages * PAGE_SIZE
    tok = pl.BlockSpec((None, n_tok, W_B), lambda n, pt: (n, 0, 0))

    def page_spec(p):
        return pl.BlockSpec((None, PAGE_SIZE, W_B), lambda n, pt: (pt[n * n_pages + p], 0, 0))

    grid_spec = pltpu.PrefetchScalarGridSpec(
        num_scalar_prefetch=1,
        grid=(n_seq,),
        in_specs=[tok, tok, tok,
                  pl.BlockSpec((rows, past), lambda n, pt: (0, 0)),
                  pl.BlockSpec((rows, n_tok), lambda n, pt: (0, 0))]
                 + [page_spec(p) for p in range(n_pages)] * 2,
        out_specs=tok,
    )
    return pl.pallas_call(
        functools.partial(_attn_sample_kernel, n_pages),
        grid_spec=grid_spec,
        out_shape=jax.ShapeDtypeStruct((n_seq, n_tok, W_B), F32),
        compiler_params=_params(1),
        name="attn_sample",
    )(page_table.reshape(-1), q, k_new, v_new, bias_past, bias_new,
      *([cache_k] * n_pages), *([cache_v] * n_pages))


def _mix_tail(x, gm_n, attn, g1, sc2, sh2, wo_ref, nbg_ref, l1g_ref, l1b_ref, x1_ref, h2_ref):
    a_n = _rms_norm(attn, nbg_ref[...]).astype(BF16)
    o = _dot(gm_n, wo_ref[0:W_A, :]) + _dot(a_n, wo_ref[W_A:W_A + W_B, :])
    x1 = _layer_norm(ALPHA * x + g1 * o, l1g_ref[...], l1b_ref[...])
    x1_ref[...] = x1
    h2_ref[...] = (x1 * (1.0 + sc2) + sh2).astype(BF16)


def _mix_prompt_kernel(x_ref, gm_ref, at_ref, g1_ref, sc2_ref, sh2_ref, wo_ref, nbg_ref, l1g_ref, l1b_ref,
                       x1_ref, h2_ref):
    _mix_tail(x_ref[...], gm_ref[...], at_ref[...], g1_ref[...], sc2_ref[...], sh2_ref[...],
              wo_ref, nbg_ref, l1g_ref, l1b_ref, x1_ref, h2_ref)


def _mix_prompt(x, gm_n, attn, mod_p, w_out, norm_b_g, ln1_g, ln1_b, *, tm, seq):
    rows = x.shape[0]
    tiles_per_seq = seq // tm
    row_blk = lambda i: (i, 0)
    vec = lambda i: (0, 0)
    mod = lambda c: pl.BlockSpec((None, 1, D_MODEL), lambda i: (i // tiles_per_seq, 0, c))
    return pl.pallas_call(
        _mix_prompt_kernel,
        grid=(rows // tm,),
        in_specs=[pl.BlockSpec((tm, D_MODEL), row_blk), pl.BlockSpec((tm, W_A), row_blk),
                  pl.BlockSpec((tm, W_B), row_blk), mod(2), mod(4), mod(3),
                  pl.BlockSpec((D_MODEL, D_MODEL), vec), pl.BlockSpec((1, W_B), vec),
                  pl.BlockSpec((1, D_MODEL), vec), pl.BlockSpec((1, D_MODEL), vec)],
        out_specs=(pl.BlockSpec((tm, D_MODEL), row_blk), pl.BlockSpec((tm, D_MODEL), row_blk)),
        out_shape=(jax.ShapeDtypeStruct((rows, D_MODEL), F32), jax.ShapeDtypeStruct((rows, D_MODEL), BF16)),
        compiler_params=_params(1),
        name="mix_prompt",
    )(x, gm_n, attn, mod_p, mod_p, mod_p, w_out, norm_b_g, ln1_g, ln1_b)


def _mix_sample_kernel(x_ref, u_ref, va_ref, ws_ref, bs_ref, nag_ref, at_ref, g1_ref, sc2_ref, sh2_ref,
                       wo_ref, nbg_ref, l1g_ref, l1b_ref, x1_ref, h2_ref):
    n_tok = ws_ref.shape[0]
    mixed = bs_ref[...]
    for s in range(n_tok):
        mixed = mixed + ws_ref[s:s + 1, :] * va_ref[:, s * W_A:(s + 1) * W_A]
    gm_n = _rms_norm(u_ref[...] * mixed, nag_ref[...]).astype(BF16)
    _mix_tail(x_ref[...], gm_n, at_ref[...], g1_ref[...], sc2_ref[...], sh2_ref[...],
              wo_ref, nbg_ref, l1g_ref, l1b_ref, x1_ref, h2_ref)


def _mix_sample(x2d, u, va, ws_tok, bs_tok, norm_a_g, attn2d, mod_s, w_out, norm_b_g, ln1_g, ln1_b,
                *, n_seq, n_tok):
    tok_blk = lambda t: (0, t)
    vec = lambda t: (0, 0)
    mod = lambda c: pl.BlockSpec((n_seq, D_MODEL), lambda t: (0, c))
    out_blk = pl.BlockSpec((n_seq, D_MODEL), lambda t: (t, 0))
    return pl.pallas_call(
        _mix_sample_kernel,
        grid=(n_tok,),
        in_specs=[pl.BlockSpec((n_seq, D_MODEL), tok_blk), pl.BlockSpec((n_seq, W_A), tok_blk),
                  pl.BlockSpec((n_seq, n_tok * W_A), vec),
                  pl.BlockSpec((None, n_tok, W_A), lambda t: (t, 0, 0)),
                  pl.BlockSpec((None, 1, W_A), lambda t: (t, 0, 0)),
                  pl.BlockSpec((1, W_A), vec),
                  pl.BlockSpec((n_seq, W_B), tok_blk), mod(2), mod(4), mod(3),
                  pl.BlockSpec((D_MODEL, D_MODEL), vec), pl.BlockSpec((1, W_B), vec),
                  pl.BlockSpec((1, D_MODEL), vec), pl.BlockSpec((1, D_MODEL), vec)],
        out_specs=(out_blk, out_blk),
        out_shape=(jax.ShapeDtypeStruct((n_tok * n_seq, D_MODEL), F32),
                   jax.ShapeDtypeStruct((n_tok * n_seq, D_MODEL), BF16)),
        compiler_params=_params(1),
        name="mix_sample",
    )(x2d, u, va, ws_tok, bs_tok, norm_a_g, attn2d, mod_s, mod_s, mod_s, w_out, norm_b_g, ln1_g, ln1_b)


def _ffn_act(conv_a, conv_b):
    return (jax.nn.silu(conv_a) * conv_b).astype(BF16)


def _ffn_prompt_kernel(tiles_per_seq, h_ref, x1_ref, g2_ref, wa_ref, wb_ref, wd_ref, wc_ref, bc_ref,
                       l2g_ref, l2b_ref, y_ref, st_ref, acc_scr, ext_scr, carry_scr):
    i = pl.program_id(0)
    j = pl.program_id(1)
    tm = h_ref.shape[0]
    h = h_ref[...]
    seq_start = (i % tiles_per_seq) == 0

    def conv(half, w_ref):
        up = _dot(h, w_ref[...])
        @pl.when(seq_start)
        def _():
            ext_scr[half, 0:SUBLANES, :] = jnp.zeros((SUBLANES, FF_TILE), F32)

        @pl.when(jnp.logical_not(seq_start))
        def _():
            ext_scr[half, 0:SUBLANES, :] = carry_scr[half, j]

        ext_scr[half, SUBLANES:SUBLANES + tm, :] = up
        tail = up[tm - SUBLANES:tm, :]
        carry_scr[half, j] = tail
        st_ref[half] = tail
        w = wc_ref[half]
        return (bc_ref[half:half + 1, :]
                + ext_scr[half, SUBLANES - 2:SUBLANES - 2 + tm, :] * w[0:1, :]
                + ext_scr[half, SUBLANES - 1:SUBLANES - 1 + tm, :] * w[1:2, :]
                + up * w[2:3, :])

    part = _dot(_ffn_act(conv(0, wa_ref), conv(1, wb_ref)), wd_ref[...])

    @pl.when(j == 0)
    def _():
        acc_scr[...] = part

    @pl.when(j > 0)
    def _():
        acc_scr[...] += part

    @pl.when(j == pl.num_programs(1) - 1)
    def _():
        y_ref[...] = _layer_norm(ALPHA * x1_ref[...] + g2_ref[...] * acc_scr[...], l2g_ref[...], l2b_ref[...])


def _ffn_prompt(h2, x1, mod_p, w_up_a, w_up_b, w_down, w_conv, b_conv, ln2_g, ln2_b, *, tm, seq):
    rows = h2.shape[0]
    tiles_per_seq = seq // tm
    n_seq = rows // seq
    row_blk = lambda i, j: (i, 0)
    vec = lambda i, j: (0, 0)
    return pl.pallas_call(
        functools.partial(_ffn_prompt_kernel, tiles_per_seq),
        grid=(rows // tm, N_FF_TILES),
        in_specs=[pl.BlockSpec((tm, D_MODEL), row_blk), pl.BlockSpec((tm, D_MODEL), row_blk),
                  pl.BlockSpec((None, 1, D_MODEL), lambda i, j: (i // tiles_per_seq, 0, 5)),
                  pl.BlockSpec((D_MODEL, FF_TILE), lambda i, j: (0, j)),
                  pl.BlockSpec((D_MODEL, FF_TILE), lambda i, j: (0, j)),
                  pl.BlockSpec((FF_TILE, D_MODEL), lambda i, j: (j, 0)),
                  pl.BlockSpec((2, CONV_W, FF_TILE), lambda i, j: (0, 0, j)),
                  pl.BlockSpec((2, FF_TILE), lambda i, j: (0, j)),
                  pl.BlockSpec((1, D_MODEL), vec), pl.BlockSpec((1, D_MODEL), vec)],
        out_specs=(pl.BlockSpec((tm, D_MODEL), row_blk),
                   pl.BlockSpec((None, 2, SUBLANES, FF_TILE), lambda i, j: (i // tiles_per_seq, 0, 0, j))),
        out_shape=(jax.ShapeDtypeStruct((rows, D_MODEL), F32),
                   jax.ShapeDtypeStruct((n_seq, 2, SUBLANES, D_FF_PAD), F32)),
        scratch_shapes=[pltpu.VMEM((tm, D_MODEL), F32),
                        pltpu.VMEM((2, SUBLANES + tm, FF_TILE), F32),
                        pltpu.VMEM((2, N_FF_TILES, SUBLANES, FF_TILE), F32)],
        compiler_params=_params(2),
        name="ffn_prompt",
    )(h2, x1, mod_p, w_up_a, w_up_b, w_down, w_conv, b_conv, ln2_g, ln2_b)


def _ffn_sample_kernel(n_tok, h_ref, x1_ref, g2_ref, wa_ref, wb_ref, wd_ref, wc_ref, bc_ref, past_ref,
                       l2g_ref, l2b_ref, y_ref, st_ref, acc_scr):
    j = pl.program_id(0)
    n_seq = g2_ref.shape[0]
    h = h_ref[...]

    def conv(half, w_ref):
        up = _dot(h, w_ref[...])
        full = [past_ref[r, half] for r in range(CONV_W - 1)]
        full += [up[t * n_seq:(t + 1) * n_seq, :] for t in range(n_tok)]
        for r in range(CONV_W - 1):
            st_ref[r, half] = full[n_tok + r]
        w = wc_ref[half]
        b = bc_ref[half:half + 1, :]
        return jnp.concatenate(
            [b + sum(full[t + c] * w[c:c + 1, :] for c in range(CONV_W)) for t in range(n_tok)], axis=0)

    part = _dot(_ffn_act(conv(0, wa_ref), conv(1, wb_ref)), wd_ref[...])

    @pl.when(j == 0)
    def _():
        acc_scr[...] = part

    @pl.when(j > 0)
    def _():
        acc_scr[...] += part

    @pl.when(j == pl.num_programs(0) - 1)
    def _():
        g2 = g2_ref[...]
        for t in range(n_tok):
            rs = slice(t * n_seq, (t + 1) * n_seq)
            y_ref[:, t * D_MODEL:(t + 1) * D_MODEL] = _layer_norm(
                ALPHA * x1_ref[rs, :] + g2 * acc_scr[rs, :], l2g_ref[...], l2b_ref[...])


def _ffn_sample(h2, x1, mod_s, w_up_a, w_up_b, w_down, w_conv, b_conv, past, ln2_g, ln2_b, *, n_seq, n_tok):
    rows = n_seq * n_tok
    full = lambda j: (0, 0)
    state_blk = pl.BlockSpec((CONV_W - 1, 2, n_seq, FF_TILE), lambda j: (0, 0, 0, j))
    return pl.pallas_call(
        functools.partial(_ffn_sample_kernel, n_tok),
        grid=(N_FF_TILES,),
        in_specs=[pl.BlockSpec((rows, D_MODEL), full), pl.BlockSpec((rows, D_MODEL), full),
                  pl.BlockSpec((n_seq, D_MODEL), lambda j: (0, 5)),
                  pl.BlockSpec((D_MODEL, FF_TILE), lambda j: (0, j)),
                  pl.BlockSpec((D_MODEL, FF_TILE), lambda j: (0, j)),
                  pl.BlockSpec((FF_TILE, D_MODEL), lambda j: (j, 0)),
                  pl.BlockSpec((2, CONV_W, FF_TILE), lambda j: (0, 0, j)),
                  pl.BlockSpec((2, FF_TILE), lambda j: (0, j)),
                  state_blk,
                  pl.BlockSpec((1, D_MODEL), full), pl.BlockSpec((1, D_MODEL), full)],
        out_specs=(pl.BlockSpec((n_seq, n_tok * D_MODEL), full), state_blk),
        out_shape=(jax.ShapeDtypeStruct((n_seq, n_tok * D_MODEL), F32),
                   jax.ShapeDtypeStruct((CONV_W - 1, 2, n_seq, D_FF_PAD), F32)),
        scratch_shapes=[pltpu.VMEM((rows, D_MODEL), F32)],
        compiler_params=_params(1),
        name="ffn_sample",
    )(h2, x1, mod_s, w_up_a, w_up_b, w_down, w_conv, b_conv, past, ln2_g, ln2_b)


def _t5_bucket_table(n_rel):
    max_exact = NUM_BUCKETS // 2
    ratio = MAX_DISTANCE // max_exact
    log_ratio = ratio.bit_length() - 1
    assert 1 << log_ratio == ratio
    steps = NUM_BUCKETS - max_exact
    out = np.zeros((n_rel,), np.int32)
    for n in range(n_rel):
        if n < max_exact:
            out[n] = n
            continue
        k = 0
        while k + 1 < steps and n ** steps >= (max_exact ** steps) << (log_ratio * (k + 1)):
            k += 1
        out[n] = min(max_exact + k, NUM_BUCKETS - 1)
    return out


def _bias_tables(rel_bias, seq, past, n_tok):
    n_rel = max(seq, past + n_tok)
    tab = rel_bias.astype(F32).T[:, _t5_bucket_table(n_rel)]
    n_blocks = seq // MOBA_BLOCK
    idx = (np.arange(n_blocks)[:, None] + 1) * MOBA_BLOCK - np.arange(2 * MOBA_BLOCK)[None, :]
    rev = jnp.where((idx >= 0)[None], tab[:, np.clip(idx, 0, n_rel - 1)], 0.0)
    t = np.tile(np.arange(n_tok), N_HEADS_B)
    hh = np.repeat(np.arange(N_HEADS_B), n_tok)
    idx_past = past + t[:, None] - np.arange(past)[None, :]
    idx_new = np.clip(t[:, None] - np.arange(n_tok)[None, :], 0, None)
    return rev, tab[hh[:, None], idx_past], tab[hh[:, None], idx_new]


def _pad_ff(a, axis):
    pad = [(0, 0)] * a.ndim
    pad[axis] = (0, D_FF_PAD - D_FF)
    return jnp.pad(a, pad)


def kernel(x_prompt, x_sample, cache_k, cache_v, state_ffn_conv, page_table, c_prompt, c_sample,
           w_ada, b_ada, w_in, ln_sgu_g, ln_sgu_b, w_s, b_s, rel_bias, norm_a_g, norm_b_g, w_out,
           ln1_g, ln1_b, w_up, w_conv, b_conv, w_down, ln2_g, ln2_b):
    n_p, seq, _ = x_prompt.shape
    n_s, n_tok, _ = x_sample.shape
    n_pages = page_table.shape[1]
    past = n_pages * PAGE_SIZE
    assert w_ada.shape[0] == 1 and seq % MOBA_BLOCK == 0 and past % MOBA_BLOCK == 0 and n_tok <= CHUNK
    n_phys = cache_k.shape[1]

    w_in_bf = w_in[0].astype(BF16)
    w_out_bf = w_out[0].astype(BF16)
    w_up_a = _pad_ff(w_up[0][:, :D_FF], 1).astype(BF16)
    w_up_b = _pad_ff(w_up[0][:, D_FF:], 1).astype(BF16)
    w_down_bf = _pad_ff(w_down[0], 0).astype(BF16)
    w_conv_p = _pad_ff(jnp.transpose(w_conv[0].reshape(CONV_W, 2, D_FF), (1, 0, 2)), 2)
    b_conv_p = _pad_ff(b_conv[0].reshape(2, D_FF), 1)
    row = lambda a: a.reshape(1, -1)
    ln_g, ln_b = row(ln_sgu_g[0]), row(ln_sgu_b[0])
    nag, nbg = row(norm_a_g[0]), row(norm_b_g[0])
    l1g, l1b, l2g, l2b = row(ln1_g[0]), row(ln1_b[0]), row(ln2_g[0]), row(ln2_b[0])
    bs_full = jnp.broadcast_to(b_s[0][:, :, None], (N_GROUPS_A, CHUNK, HEAD_DIM))
    tril_tok = np.tril(np.ones((n_tok, n_tok), np.float32))
    ws_tok = jnp.repeat(jnp.transpose(w_s[0][:, :n_tok, :n_tok] * tril_tok, (1, 2, 0)), HEAD_DIM, axis=2)
    bs_tok = jnp.repeat(b_s[0][:, :n_tok].T, HEAD_DIM, axis=1).reshape(n_tok, 1, W_A)
    rev_tab, bias_past, bias_new = _bias_tables(rel_bias, seq, past, n_tok)

    pad_rows = (-(n_s + n_p)) % SUBLANES
    c_all = jnp.concatenate([c_sample, c_prompt, jnp.zeros((pad_rows, D_MODEL), F32)], axis=0)
    mod = _ada(c_all, w_ada[0], b_ada[0])
    mod_s = mod[:n_s]
    mod_p = mod[n_s:n_s + n_p].reshape(n_p, 1, 6 * D_MODEL)

    xp = x_prompt.reshape(n_p * seq, D_MODEL)
    q_hi, q_lo, k_p, v_p, k_bf, v_bf, k_mean, gm_n = _proj_prompt(
        xp, mod_p, w_in_bf, ln_g, ln_b, w_s[0], bs_full, nag, tm=512, seq=seq)
    attn_p = _attn_prompt(q_hi, q_lo, k_bf, v_bf, k_mean.reshape(n_p, seq // MOBA_BLOCK, W_B), rev_tab,
                          n_seq=n_p, seq=seq)
    x1_p, h2_p = _mix_prompt(xp, gm_n, attn_p, mod_p, w_out_bf, nbg, l1g, l1b, tm=512, seq=seq)
    y_p, st_p = _ffn_prompt(h2_p, x1_p, mod_p, w_up_a, w_up_b, w_down_bf, w_conv_p, b_conv_p, l2g, l2b,
                            tm=512, seq=seq)
    conv_p = st_p[:, :, SUBLANES - (CONV_W - 1):, :D_FF]
    conv_p = jnp.transpose(conv_p, (0, 2, 1, 3)).reshape(1, n_p, CONV_W - 1, 2 * D_FF)

    xs = x_sample.reshape(n_s, n_tok * D_MODEL)
    q_s, k_s, v_s, u_s, va_s = _proj_sample(xs, mod_s, w_in_bf, ln_g, ln_b, n_seq=n_s, n_tok=n_tok)
    tok3 = lambda a: a.reshape(n_s, n_tok, W_B)
    attn_s = _attn_sample(page_table, tok3(q_s), tok3(k_s), tok3(v_s), bias_past, bias_new,
                          cache_k[0].reshape(n_phys, PAGE_SIZE, W_B), cache_v[0].reshape(n_phys, PAGE_SIZE, W_B))
    x1_s, h2_s = _mix_sample(xs, u_s, va_s, ws_tok, bs_tok, nag, attn_s.reshape(n_s, n_tok * W_B), mod_s,
                             w_out_bf, nbg, l1g, l1b, n_seq=n_s, n_tok=n_tok)
    past_up = state_ffn_conv[0].reshape(n_s, CONV_W - 1, 2, D_FF)
    past_up = _pad_ff(jnp.transpose(past_up, (1, 2, 0, 3)), 3)
    y_s, st_s = _ffn_sample(h2_s, x1_s, mod_s, w_up_a, w_up_b, w_down_bf, w_conv_p, b_conv_p, past_up,
                            l2g, l2b, n_seq=n_s, n_tok=n_tok)
    conv_s = jnp.transpose(st_s[..., :D_FF], (2, 0, 1, 3)).reshape(1, n_s, CONV_W - 1, 2 * D_FF)

    heads_p = (1, n_p, seq, N_HEADS_B, HEAD_DIM)
    heads_s = (1, n_s, n_tok, N_HEADS_B, HEAD_DIM)
    return (y_p.reshape(n_p, seq, D_MODEL), y_s.reshape(n_s, n_tok, D_MODEL),
            k_p.reshape(heads_p), v_p.reshape(heads_p), conv_p,
            k_s.reshape(heads_s), v_s.reshape(heads_s), va_s.reshape(1, n_s, n_tok, W_A), conv_s)
```

```python
import functools

import numpy as np
import jax
import jax.numpy as jnp
from jax import lax
from jax.experimental import pallas as pl
from jax.experimental.pallas import tpu as pltpu

F32 = jnp.float32
BF16 = jnp.bfloat16

D_MODEL = 2048
HEAD_DIM = 128
N_HEADS_B = 8
N_GROUPS_A = 8
W_B = N_HEADS_B * HEAD_DIM
W_A = N_GROUPS_A * HEAD_DIM
CHUNK = 128
MOBA_BLOCK = 256
MOBA_TOPK = 3
NUM_BUCKETS = 32
MAX_DISTANCE = 1024
CONV_W = 3
D_FF = 5504
PAGE_SIZE = 128
ALPHA = 2.0 ** 0.25
LN_EPS = 1e-5
NEG_INF = -1e30

FF_TILE = 512
D_FF_PAD = ((D_FF + FF_TILE - 1) // FF_TILE) * FF_TILE
N_FF_TILES = D_FF_PAD // FF_TILE
SUBLANES = 8
VMEM_LIMIT_BYTES = 56 * 1024 * 1024

_NT = (((1,), (1,)), ((), ()))


def _params(n_axes):
    return pltpu.CompilerParams(dimension_semantics=("arbitrary",) * n_axes,
                                vmem_limit_bytes=VMEM_LIMIT_BYTES)


def _layer_norm(x, g, b):
    mu = jnp.mean(x, axis=-1, keepdims=True)
    xc = x - mu
    var = jnp.mean(xc * xc, axis=-1, keepdims=True)
    return xc * lax.rsqrt(var + LN_EPS) * g + b


def _rms_norm(x, g):
    return x * lax.rsqrt(jnp.mean(x * x, axis=-1, keepdims=True) + LN_EPS) * g


def _split_bf16(x):
    hi = x.astype(BF16)
    lo = (x - hi.astype(F32)).astype(BF16)
    return hi, lo


def _dot(a, b):
    return jnp.dot(a, b, preferred_element_type=F32)


def _dot_nt(a, b):
    return lax.dot_general(a, b, _NT, preferred_element_type=F32)


def _block_scores(q_hi, q_lo, km):
    km_hi, km_lo = _split_bf16(km)
    return _dot_nt(q_hi, km_hi) + _dot_nt(q_hi, km_lo) + _dot_nt(q_lo, km_hi)


def _topk_select(s, n_valid, axis):
    idx = lax.broadcasted_iota(jnp.int32, s.shape, axis)
    rank = jnp.zeros(s.shape, F32)
    for i in range(n_valid):
        si = lax.slice_in_dim(s, i, i + 1, axis=axis)
        beats = (si > s) | ((si == s) & (i < idx))
        rank = rank + beats.astype(F32)
    return ((rank < MOBA_TOPK) & (idx < n_valid)).astype(F32)


def _ada_kernel(c_ref, w_ref, b_ref, o_ref):
    c = c_ref[...]
    a = jax.nn.silu(c).astype(BF16)
    o_ref[...] = _dot(a, w_ref[...].astype(BF16)) + b_ref[...]


def _ada(c_all, w_ada, b_ada):
    rows = c_all.shape[0]
    n_out = w_ada.shape[1]
    tn = 512
    return pl.pallas_call(
        _ada_kernel,
        grid=(n_out // tn,),
        in_specs=[pl.BlockSpec((rows, D_MODEL), lambda j: (0, 0)),
                  pl.BlockSpec((D_MODEL, tn), lambda j: (0, j)),
                  pl.BlockSpec((1, tn), lambda j: (0, j))],
        out_specs=pl.BlockSpec((rows, tn), lambda j: (0, j)),
        out_shape=jax.ShapeDtypeStruct((rows, n_out), F32),
        compiler_params=_params(1),
        name="ada",
    )(c_all, w_ada, b_ada.reshape(1, n_out))


def _proj_prompt_kernel(x_ref, sc_ref, sh_ref, w_ref, lng_ref, lnb_ref, ws_ref, bs_ref, nag_ref,
                        qhi_ref, qlo_ref, k_ref, v_ref, kbf_ref, vbf_ref, km_ref, gm_ref,
                        h_scr, u_scr, gm_scr):
    j = pl.program_id(1)
    tm = x_ref.shape[0]

    @pl.when(j == 0)
    def _():
        h_scr[...] = (x_ref[...] * (1.0 + sc_ref[...]) + sh_ref[...]).astype(BF16)

    acc = _dot(h_scr[...], w_ref[...])

    @pl.when(j == 0)
    def _():
        q_hi, q_lo = _split_bf16(acc * (HEAD_DIM ** -0.5))
        qhi_ref[...] = q_hi
        qlo_ref[...] = q_lo

    @pl.when(j == 1)
    def _():
        k_ref[...] = acc
        kbf_ref[...] = acc.astype(BF16)
        km_ref[...] = jnp.mean(acc.reshape(tm // MOBA_BLOCK, MOBA_BLOCK, W_B), axis=1)

    @pl.when(j == 2)
    def _():
        v_ref[...] = acc
        vbf_ref[...] = acc.astype(BF16)

    @pl.when(j == 3)
    def _():
        u_scr[...] = jax.nn.gelu(acc)

    @pl.when(j == 4)
    def _():
        va = _layer_norm(jax.nn.gelu(acc), lng_ref[...], lnb_ref[...]).astype(BF16)
        n_chunks = tm // CHUNK
        row = lax.broadcasted_iota(jnp.int32, (CHUNK, CHUNK), 0)
        col = lax.broadcasted_iota(jnp.int32, (CHUNK, CHUNK), 1)
        for g in range(N_GROUPS_A):
            gs = slice(g * HEAD_DIM, (g + 1) * HEAD_DIM)
            w_g = jnp.where(row >= col, ws_ref[g], 0.0).astype(BF16)
            v_g = jnp.concatenate([va[c * CHUNK:(c + 1) * CHUNK, gs] for c in range(n_chunks)], axis=1)
            m_g = _dot(w_g, v_g)
            for c in range(n_chunks):
                rs = slice(c * CHUNK, (c + 1) * CHUNK)
                mixed = m_g[:, c * HEAD_DIM:(c + 1) * HEAD_DIM] + bs_ref[g]
                gm_scr[rs, gs] = u_scr[rs, gs] * mixed
        gm_ref[...] = _rms_norm(gm_scr[...], nag_ref[...]).astype(BF16)


def _proj_prompt(x, mod_p, w_in, ln_g, ln_b, w_s, bs_full, norm_a_g, *, tm, seq):
    rows = x.shape[0]
    tiles_per_seq = seq // tm
    n_col = w_in.shape[1] // W_B
    row_blk = lambda i, j: (i, 0)
    vec = lambda i, j: (0, 0)
    out_shape = (
        jax.ShapeDtypeStruct((rows, W_B), BF16),
        jax.ShapeDtypeStruct((rows, W_B), BF16),
        jax.ShapeDtypeStruct((rows, W_B), F32),
        jax.ShapeDtypeStruct((rows, W_B), F32),
        jax.ShapeDtypeStruct((rows, W_B), BF16),
        jax.ShapeDtypeStruct((rows, W_B), BF16),
        jax.ShapeDtypeStruct((rows // tm, tm // MOBA_BLOCK, W_B), F32),
        jax.ShapeDtypeStruct((rows, W_A), BF16),
    )
    out_specs = (
        pl.BlockSpec((tm, W_B), row_blk), pl.BlockSpec((tm, W_B), row_blk),
        pl.BlockSpec((tm, W_B), row_blk), pl.BlockSpec((tm, W_B), row_blk),
        pl.BlockSpec((tm, W_B), row_blk), pl.BlockSpec((tm, W_B), row_blk),
        pl.BlockSpec((None, tm // MOBA_BLOCK, W_B), lambda i, j: (i, 0, 0)),
        pl.BlockSpec((tm, W_A), row_blk),
    )
    return pl.pallas_call(
        _proj_prompt_kernel,
        grid=(rows // tm, n_col),
        in_specs=[pl.BlockSpec((tm, D_MODEL), row_blk),
                  pl.BlockSpec((None, 1, D_MODEL), lambda i, j: (i // tiles_per_seq, 0, 1)),
                  pl.BlockSpec((None, 1, D_MODEL), lambda i, j: (i // tiles_per_seq, 0, 0)),
                  pl.BlockSpec((D_MODEL, W_B), lambda i, j: (0, j)),
                  pl.BlockSpec((1, W_A), vec), pl.BlockSpec((1, W_A), vec),
                  pl.BlockSpec((N_GROUPS_A, CHUNK, CHUNK), lambda i, j: (0, 0, 0)),
                  pl.BlockSpec((N_GROUPS_A, CHUNK, HEAD_DIM), lambda i, j: (0, 0, 0)),
                  pl.BlockSpec((1, W_A), vec)],
        out_specs=out_specs,
        out_shape=out_shape,
        scratch_shapes=[pltpu.VMEM((tm, D_MODEL), BF16), pltpu.VMEM((tm, W_A), F32),
                        pltpu.VMEM((tm, W_A), F32)],
        compiler_params=_params(2),
        name="proj_prompt",
    )(x, mod_p, mod_p, w_in, ln_g, ln_b, w_s, bs_full, norm_a_g)


def _proj_sample_kernel(x_ref, sc_ref, sh_ref, w_ref, lng_ref, lnb_ref,
                        q_ref, k_ref, v_ref, u_ref, va_ref, h_scr):
    j = pl.program_id(1)

    @pl.when(j == 0)
    def _():
        h_scr[...] = (x_ref[...] * (1.0 + sc_ref[...]) + sh_ref[...]).astype(BF16)

    acc = _dot(h_scr[...], w_ref[...])

    @pl.when(j == 0)
    def _():
        q_ref[...] = acc * (HEAD_DIM ** -0.5)

    @pl.when(j == 1)
    def _():
        k_ref[...] = acc

    @pl.when(j == 2)
    def _():
        v_ref[...] = acc

    @pl.when(j == 3)
    def _():
        u_ref[...] = jax.nn.gelu(acc)

    @pl.when(j == 4)
    def _():
        va_ref[...] = _layer_norm(jax.nn.gelu(acc), lng_ref[...], lnb_ref[...])


def _proj_sample(x2d, mod_s, w_in, ln_g, ln_b, *, n_seq, n_tok):
    n_col = w_in.shape[1] // W_B
    tok_blk = lambda t, j: (0, t)
    vec = lambda t, j: (0, 0)
    out = jax.ShapeDtypeStruct((n_seq, n_tok * W_B), F32)
    return pl.pallas_call(
        _proj_sample_kernel,
        grid=(n_tok, n_col),
        in_specs=[pl.BlockSpec((n_seq, D_MODEL), tok_blk),
                  pl.BlockSpec((n_seq, D_MODEL), lambda t, j: (0, 1)),
                  pl.BlockSpec((n_seq, D_MODEL), lambda t, j: (0, 0)),
                  pl.BlockSpec((D_MODEL, W_B), lambda t, j: (0, j)),
                  pl.BlockSpec((1, W_A), vec), pl.BlockSpec((1, W_A), vec)],
        out_specs=tuple(pl.BlockSpec((n_seq, W_B), tok_blk) for _ in range(5)),
        out_shape=(out,) * 5,
        scratch_shapes=[pltpu.VMEM((n_seq, D_MODEL), BF16)],
        compiler_params=_params(2),
        name="proj_sample",
    )(x2d, mod_s, mod_s, w_in, ln_g, ln_b)


def _attn_prompt_kernel(qhi_ref, qlo_ref, k_ref, v_ref, km_ref, rev_ref, o_ref, bias_scr):
    n_blocks = k_ref.shape[0] // MOBA_BLOCK

    @pl.when(pl.program_id(1) == 0)
    def _():
        for d in range(n_blocks):
            r = jnp.broadcast_to(rev_ref[d:d + 1, :], (MOBA_BLOCK, 2 * MOBA_BLOCK))
            rolled = pltpu.roll(r, 0, 1, stride=1, stride_axis=0)
            bias_scr[d] = rolled[:, MOBA_BLOCK:]

    km = km_ref[...]
    row = lax.broadcasted_iota(jnp.int32, (MOBA_BLOCK, MOBA_BLOCK), 0)
    col = lax.broadcasted_iota(jnp.int32, (MOBA_BLOCK, MOBA_BLOCK), 1)
    for qb in range(n_blocks):
        qs = slice(qb * MOBA_BLOCK, (qb + 1) * MOBA_BLOCK)
        n_keys = (qb + 1) * MOBA_BLOCK
        q_hi = qhi_ref[qs, :]
        logits = _dot_nt(q_hi, k_ref[0:n_keys, :])
        sel = None
        if qb > MOBA_TOPK:
            sel = _topk_select(_block_scores(q_hi, qlo_ref[qs, :], km), qb, axis=1)
        pieces = []
        for jb in range(qb + 1):
            lj = logits[:, jb * MOBA_BLOCK:(jb + 1) * MOBA_BLOCK] + bias_scr[qb - jb]
            if jb == qb:
                lj = jnp.where(row >= col, lj, NEG_INF)
            elif sel is not None:
                lj = jnp.where(sel[:, jb:jb + 1] > 0.0, lj, NEG_INF)
            pieces.append(lj)
        lg = jnp.concatenate(pieces, axis=1) if len(pieces) > 1 else pieces[0]
        m = jnp.max(lg, axis=-1, keepdims=True)
        p = jnp.exp(lg - m)
        z = jnp.sum(p, axis=-1, keepdims=True)
        o_ref[qs, :] = _dot(p.astype(BF16), v_ref[0:n_keys, :]) / z


def _attn_prompt(q_hi, q_lo, k_bf, v_bf, k_mean, rev_tab, *, n_seq, seq):
    n_blocks = seq // MOBA_BLOCK
    qkv = pl.BlockSpec((seq, HEAD_DIM), lambda h, n: (n, h))
    return pl.pallas_call(
        _attn_prompt_kernel,
        grid=(N_HEADS_B, n_seq),
        in_specs=[qkv, qkv, qkv, qkv,
                  pl.BlockSpec((None, n_blocks, HEAD_DIM), lambda h, n: (n, 0, h)),
                  pl.BlockSpec((None, n_blocks, 2 * MOBA_BLOCK), lambda h, n: (h, 0, 0))],
        out_specs=pl.BlockSpec((seq, HEAD_DIM), lambda h, n: (n, h)),
        out_shape=jax.ShapeDtypeStruct((n_seq * seq, W_B), F32),
        scratch_shapes=[pltpu.VMEM((n_blocks, MOBA_BLOCK, MOBA_BLOCK), F32)],
        compiler_params=_params(2),
        name="attn_prompt",
    )(q_hi, q_lo, k_bf, v_bf, k_mean, rev_tab)


def _attn_sample_kernel(n_pages, pt_ref, q_ref, kn_ref, vn_ref, bias_ref, biasn_ref, *refs):
    k_pages = refs[:n_pages]
    v_pages = refs[n_pages:2 * n_pages]
    o_ref = refs[2 * n_pages]
    n_tok = q_ref.shape[0]
    rows = n_tok * N_HEADS_B
    cols = PAGE_SIZE * N_HEADS_B
    pages_per_block = MOBA_BLOCK // PAGE_SIZE
    n_blocks = n_pages // pages_per_block

    q = q_ref[...].reshape(rows, HEAD_DIM)
    q_bf = q.astype(BF16)

    logit_pages = []
    k_sums = []
    for p in range(n_pages):
        kp = k_pages[p][...]
        k_sums.append(jnp.sum(kp, axis=0))
        logit_pages.append(_dot_nt(q_bf, kp.reshape(cols, HEAD_DIM).astype(BF16)))

    blk_lane = lax.broadcasted_iota(jnp.int32, (rows, n_blocks), 1)
    scores = jnp.zeros((rows, n_blocks), F32)
    for b in range(n_blocks):
        km = sum(k_sums[b * pages_per_block:(b + 1) * pages_per_block]) * (1.0 / MOBA_BLOCK)
        s_b = jnp.sum(q * jnp.concatenate([km] * n_tok, axis=0), axis=1, keepdims=True)
        scores = jnp.where(blk_lane == b, s_b, scores)
    sel = _topk_select(scores, n_blocks, axis=1)

    masked = []
    for p in range(n_pages):
        b = p // pages_per_block
        lp = logit_pages[p] + bias_ref[:, p * cols:(p + 1) * cols]
        masked.append(jnp.where(sel[:, b:b + 1] > 0.0, lp, NEG_INF))
    ln = _dot_nt(q_bf, kn_ref[...].reshape(rows, HEAD_DIM).astype(BF16)) + biasn_ref[...]

    m = jnp.maximum(jnp.max(functools.reduce(jnp.maximum, masked), axis=-1, keepdims=True),
                    jnp.max(ln, axis=-1, keepdims=True))
    p_new = jnp.exp(ln - m)
    acc = _dot(p_new.astype(BF16), vn_ref[...].reshape(rows, HEAD_DIM).astype(BF16))
    p_sum = None
    for p in range(n_pages):
        pp = jnp.exp(masked[p] - m)
        p_sum = pp if p_sum is None else p_sum + pp
        acc = acc + _dot(pp.astype(BF16), v_pages[p][...].reshape(cols, HEAD_DIM).astype(BF16))
    z = jnp.sum(p_sum, axis=-1, keepdims=True) + jnp.sum(p_new, axis=-1, keepdims=True)
    o_ref[...] = (acc / z).reshape(n_tok, N_HEADS_B, HEAD_DIM)


def _attn_sample(page_table, q, k_new, v_new, bias_past, bias_new, cache_k, cache_v):
    n_seq, n_pages = page_table.shape
    n_tok = q.shape[1]
    tok = pl.BlockSpec((None, n_tok, N_HEADS_B, HEAD_DIM), lambda n, pt: (n, 0, 0, 0))

    def page_spec(p):
        return pl.BlockSpec((None, None, PAGE_SIZE, N_HEADS_B, HEAD_DIM),
                            lambda n, pt: (0, pt[n * n_pages + p], 0, 0, 0))

    grid_spec = pltpu.PrefetchScalarGridSpec(
        num_scalar_prefetch=1,
        grid=(n_seq,),
        in_specs=[tok, tok, tok,
                  pl.BlockSpec(bias_past.shape, lambda n, pt: (0, 0)),
                  pl.BlockSpec(bias_new.shape, lambda n, pt: (0, 0))]
                 + [page_spec(p) for p in range(n_pages)] * 2,
        out_specs=tok,
    )
    return pl.pallas_call(
        functools.partial(_attn_sample_kernel, n_pages),
        grid_spec=grid_spec,
        out_shape=jax.ShapeDtypeStruct((n_seq, n_tok, N_HEADS_B, HEAD_DIM), F32),
        compiler_params=_params(1),
        name="attn_sample",
    )(page_table.reshape(-1), q, k_new, v_new, bias_past, bias_new,
      *([cache_k] * n_pages), *([cache_v] * n_pages))


def _mix_tail(x, gm_n, attn, g1, sc2, sh2, wo_ref, nbg_ref, l1g_ref, l1b_ref, x1_ref, h2_ref):
    a_n = _rms_norm(attn, nbg_ref[...]).astype(BF16)
    o = _dot(gm_n, wo_ref[0:W_A, :]) + _dot(a_n, wo_ref[W_A:W_A + W_B, :])
    x1 = _layer_norm(ALPHA * x + g1 * o, l1g_ref[...], l1b_ref[...])
    x1_ref[...] = x1
    h2_ref[...] = (x1 * (1.0 + sc2) + sh2).astype(BF16)


def _mix_prompt_kernel(x_ref, gm_ref, at_ref, g1_ref, sc2_ref, sh2_ref, wo_ref, nbg_ref, l1g_ref, l1b_ref,
                       x1_ref, h2_ref):
    _mix_tail(x_ref[...], gm_ref[...], at_ref[...], g1_ref[...], sc2_ref[...], sh2_ref[...],
              wo_ref, nbg_ref, l1g_ref, l1b_ref, x1_ref, h2_ref)


def _mix_prompt(x, gm_n, attn, mod_p, w_out, norm_b_g, ln1_g, ln1_b, *, tm, seq):
    rows = x.shape[0]
    tiles_per_seq = seq // tm
    row_blk = lambda i: (i, 0)
    vec = lambda i: (0, 0)
    mod = lambda c: pl.BlockSpec((None, 1, D_MODEL), lambda i: (i // tiles_per_seq, 0, c))
    return pl.pallas_call(
        _mix_prompt_kernel,
        grid=(rows // tm,),
        in_specs=[pl.BlockSpec((tm, D_MODEL), row_blk), pl.BlockSpec((tm, W_A), row_blk),
                  pl.BlockSpec((tm, W_B), row_blk), mod(2), mod(4), mod(3),
                  pl.BlockSpec((D_MODEL, D_MODEL), vec), pl.BlockSpec((1, W_B), vec),
                  pl.BlockSpec((1, D_MODEL), vec), pl.BlockSpec((1, D_MODEL), vec)],
        out_specs=(pl.BlockSpec((tm, D_MODEL), row_blk), pl.BlockSpec((tm, D_MODEL), row_blk)),
        out_shape=(jax.ShapeDtypeStruct((rows, D_MODEL), F32), jax.ShapeDtypeStruct((rows, D_MODEL), BF16)),
        compiler_params=_params(1),
        name="mix_prompt",
    )(x, gm_n, attn, mod_p, mod_p, mod_p, w_out, norm_b_g, ln1_g, ln1_b)


def _mix_sample_kernel(x_ref, u_ref, va_ref, ws_ref, bs_ref, nag_ref, at_ref, g1_ref, sc2_ref, sh2_ref,
                       wo_ref, nbg_ref, l1g_ref, l1b_ref, x1_ref, h2_ref):
    n_tok = ws_ref.shape[0]
    mixed = bs_ref[...]
    for s in range(n_tok):
        mixed = mixed + ws_ref[s:s + 1, :] * va_ref[:, s * W_A:(s + 1) * W_A]
    gm_n = _rms_norm(u_ref[...] * mixed, nag_ref[...]).astype(BF16)
    _mix_tail(x_ref[...], gm_n, at_ref[...], g1_ref[...], sc2_ref[...], sh2_ref[...],
              wo_ref, nbg_ref, l1g_ref, l1b_ref, x1_ref, h2_ref)


def _mix_sample(x2d, u, va, ws_tok, bs_tok, norm_a_g, attn2d, mod_s, w_out, norm_b_g, ln1_g, ln1_b,
                *, n_seq, n_tok):
    tok_blk = lambda t: (0, t)
    vec = lambda t: (0, 0)
    mod = lambda c: pl.BlockSpec((n_seq, D_MODEL), lambda t: (0, c))
    out_blk = pl.BlockSpec((n_seq, D_MODEL), lambda t: (t, 0))
    return pl.pallas_call(
        _mix_sample_kernel,
        grid=(n_tok,),
        in_specs=[pl.BlockSpec((n_seq, D_MODEL), tok_blk), pl.BlockSpec((n_seq, W_A), tok_blk),
                  pl.BlockSpec((n_seq, n_tok * W_A), vec),
                  pl.BlockSpec((None, n_tok, W_A), lambda t: (t, 0, 0)),
                  pl.BlockSpec((None, 1, W_A), lambda t: (t, 0, 0)),
                  pl.BlockSpec((1, W_A), vec),
                  pl.BlockSpec((n_seq, W_B), tok_blk), mod(2), mod(4), mod(3),
                  pl.BlockSpec((D_MODEL, D_MODEL), vec), pl.BlockSpec((1, W_B), vec),
                  pl.BlockSpec((1, D_MODEL), vec), pl.BlockSpec((1, D_MODEL), vec)],
        out_specs=(out_blk, out_blk),
        out_shape=(jax.ShapeDtypeStruct((n_tok * n_seq, D_MODEL), F32),
                   jax.ShapeDtypeStruct((n_tok * n_seq, D_MODEL), BF16)),
        compiler_params=_params(1),
        name="mix_sample",
    )(x2d, u, va, ws_tok, bs_tok, norm_a_g, attn2d, mod_s, mod_s, mod_s, w_out, norm_b_g, ln1_g, ln1_b)


def _ffn_act(conv_a, conv_b):
    return (jax.nn.silu(conv_a) * conv_b).astype(BF16)


def _ffn_prompt_kernel(tiles_per_seq, h_ref, x1_ref, g2_ref, wa_ref, wb_ref, wd_ref, wc_ref, bc_ref,
                       l2g_ref, l2b_ref, y_ref, st_ref, acc_scr, ext_scr, carry_scr):
    i = pl.program_id(0)
    j = pl.program_id(1)
    tm = h_ref.shape[0]
    h = h_ref[...]
    seq_start = (i % tiles_per_seq) == 0

    def conv(half, w_ref):
        up = _dot(h, w_ref[...])
        @pl.when(seq_start)
        def _():
            ext_scr[half, 0:SUBLANES, :] = jnp.zeros((SUBLANES, FF_TILE), F32)

        @pl.when(jnp.logical_not(seq_start))
        def _():
            ext_scr[half, 0:SUBLANES, :] = carry_scr[half, j]

        ext_scr[half, SUBLANES:SUBLANES + tm, :] = up
        tail = up[tm - SUBLANES:tm, :]
        carry_scr[half, j] = tail
        st_ref[half] = tail
        w = wc_ref[half]
        return (bc_ref[half:half + 1, :]
                + ext_scr[half, SUBLANES - 2:SUBLANES - 2 + tm, :] * w[0:1, :]
                + ext_scr[half, SUBLANES - 1:SUBLANES - 1 + tm, :] * w[1:2, :]
                + up * w[2:3, :])

    part = _dot(_ffn_act(conv(0, wa_ref), conv(1, wb_ref)), wd_ref[...])

    @pl.when(j == 0)
    def _():
        acc_scr[...] = part

    @pl.when(j > 0)
    def _():
        acc_scr[...] += part

    @pl.when(j == pl.num_programs(1) - 1)
    def _():
        y_ref[...] = _layer_norm(ALPHA * x1_ref[...] + g2_ref[...] * acc_scr[...], l2g_ref[...], l2b_ref[...])


def _ffn_prompt(h2, x1, mod_p, w_up_a, w_up_b, w_down, w_conv, b_conv, ln2_g, ln2_b, *, tm, seq):
    rows = h2.shape[0]
    tiles_per_seq = seq // tm
    row_blk = lambda i, j: (i, 0)
    vec = lambda i, j: (0, 0)
    return pl.pallas_call(
        functools.partial(_ffn_prompt_kernel, tiles_per_seq),
        grid=(rows // tm, N_FF_TILES),
        in_specs=[pl.BlockSpec((tm, D_MODEL), row_blk), pl.BlockSpec((tm, D_MODEL), row_blk),
                  pl.BlockSpec((None, 1, D_MODEL), lambda i, j: (i // tiles_per_seq, 0, 5)),
                  pl.BlockSpec((D_MODEL, FF_TILE), lambda i, j: (0, j)),
                  pl.BlockSpec((D_MODEL, FF_TILE), lambda i, j: (0, j)),
                  pl.BlockSpec((FF_TILE, D_MODEL), lambda i, j: (j, 0)),
                  pl.BlockSpec((2, CONV_W, FF_TILE), lambda i, j: (0, 0, j)),
                  pl.BlockSpec((2, FF_TILE), lambda i, j: (0, j)),
                  pl.BlockSpec((1, D_MODEL), vec), pl.BlockSpec((1, D_MODEL), vec)],
        out_specs=(pl.BlockSpec((tm, D_MODEL), row_blk),
                   pl.BlockSpec((None, 2, SUBLANES, FF_TILE), lambda i, j: (i, 0, 0, j))),
        out_shape=(jax.ShapeDtypeStruct((rows, D_MODEL), F32),
                   jax.ShapeDtypeStruct((rows // tm, 2, SUBLANES, D_FF_PAD), F32)),
        scratch_shapes=[pltpu.VMEM((tm, D_MODEL), F32),
                        pltpu.VMEM((2, SUBLANES + tm, FF_TILE), F32),
                        pltpu.VMEM((2, N_FF_TILES, SUBLANES, FF_TILE), F32)],
        compiler_params=_params(2),
        name="ffn_prompt",
    )(h2, x1, mod_p, w_up_a, w_up_b, w_down, w_conv, b_conv, ln2_g, ln2_b)


def _ffn_sample_kernel(n_tok, h_ref, x1_ref, g2_ref, wa_ref, wb_ref, wd_ref, wc_ref, bc_ref, past_ref,
                       l2g_ref, l2b_ref, y_ref, st_ref, acc_scr):
    j = pl.program_id(0)
    n_seq = g2_ref.shape[0]
    h = h_ref[...]

    def conv(half, w_ref):
        up = _dot(h, w_ref[...])
        full = [past_ref[r, half] for r in range(CONV_W - 1)]
        full += [up[t * n_seq:(t + 1) * n_seq, :] for t in range(n_tok)]
        for r in range(CONV_W - 1):
            st_ref[r, half] = full[n_tok + r]
        w = wc_ref[half]
        b = bc_ref[half:half + 1, :]
        return jnp.concatenate(
            [b + sum(full[t + c] * w[c:c + 1, :] for c in range(CONV_W)) for t in range(n_tok)], axis=0)

    part = _dot(_ffn_act(conv(0, wa_ref), conv(1, wb_ref)), wd_ref[...])

    @pl.when(j == 0)
    def _():
        acc_scr[...] = part

    @pl.when(j > 0)
    def _():
        acc_scr[...] += part

    @pl.when(j == pl.num_programs(0) - 1)
    def _():
        g2 = g2_ref[...]
        for t in range(n_tok):
            rs = slice(t * n_seq, (t + 1) * n_seq)
            y_ref[:, t * D_MODEL:(t + 1) * D_MODEL] = _layer_norm(
                ALPHA * x1_ref[rs, :] + g2 * acc_scr[rs, :], l2g_ref[...], l2b_ref[...])


def _ffn_sample(h2, x1, mod_s, w_up_a, w_up_b, w_down, w_conv, b_conv, past, ln2_g, ln2_b, *, n_seq, n_tok):
    rows = n_seq * n_tok
    full = lambda j: (0, 0)
    state_blk = pl.BlockSpec((CONV_W - 1, 2, n_seq, FF_TILE), lambda j: (0, 0, 0, j))
    return pl.pallas_call(
        functools.partial(_ffn_sample_kernel, n_tok),
        grid=(N_FF_TILES,),
        in_specs=[pl.BlockSpec((rows, D_MODEL), full), pl.BlockSpec((rows, D_MODEL), full),
                  pl.BlockSpec((n_seq, D_MODEL), lambda j: (0, 5)),
                  pl.BlockSpec((D_MODEL, FF_TILE), lambda j: (0, j)),
                  pl.BlockSpec((D_MODEL, FF_TILE), lambda j: (0, j)),
                  pl.BlockSpec((FF_TILE, D_MODEL), lambda j: (j, 0)),
                  pl.BlockSpec((2, CONV_W, FF_TILE), lambda j: (0, 0, j)),
                  pl.BlockSpec((2, FF_TILE), lambda j: (0, j)),
                  state_blk,
                  pl.BlockSpec((1, D_MODEL), full), pl.BlockSpec((1, D_MODEL), full)],
        out_specs=(pl.BlockSpec((n_seq, n_tok * D_MODEL), full), state_blk),
        out_shape=(jax.ShapeDtypeStruct((n_seq, n_tok * D_MODEL), F32),
                   jax.ShapeDtypeStruct((CONV_W - 1, 2, n_seq, D_FF_PAD), F32)),
        scratch_shapes=[pltpu.VMEM((rows, D_MODEL), F32)],
        compiler_params=_params(1),
        name="ffn_sample",
    )(h2, x1, mod_s, w_up_a, w_up_b, w_down, w_conv, b_conv, past, ln2_g, ln2_b)


def _t5_bucket_table(n_rel):
    max_exact = NUM_BUCKETS // 2
    ratio = MAX_DISTANCE // max_exact
    log_ratio = ratio.bit_length() - 1
    assert 1 << log_ratio == ratio
    steps = NUM_BUCKETS - max_exact
    out = np.zeros((n_rel,), np.int32)
    for n in range(n_rel):
        if n < max_exact:
            out[n] = n
            continue
        k = 0
        while k + 1 < steps and n ** steps >= (max_exact ** steps) << (log_ratio * (k + 1)):
            k += 1
        out[n] = min(max_exact + k, NUM_BUCKETS - 1)
    return out


def _bias_tables(rel_bias, seq, past, n_tok):
    n_rel = max(seq, past + n_tok)
    onehot = np.zeros((NUM_BUCKETS, n_rel), np.float32)
    onehot[_t5_bucket_table(n_rel), np.arange(n_rel)] = 1.0
    tab = jnp.dot(rel_bias.astype(F32).T, onehot, precision=lax.Precision.HIGHEST)
    n_blocks = seq // MOBA_BLOCK
    tab_pad = jnp.pad(tab, ((0, 0), (MOBA_BLOCK, 1)))
    rev = jnp.stack([tab_pad[:, d * MOBA_BLOCK + 1:(d + 2) * MOBA_BLOCK + 1][:, ::-1]
                     for d in range(n_blocks)], axis=1)
    same_head = np.eye(N_HEADS_B, dtype=bool)
    past_rows = []
    for t in range(n_tok):
        by_key = tab[:, t + 1:t + 1 + past][:, ::-1]
        full = jnp.where(same_head[:, None, :], by_key[:, :, None], NEG_INF)
        past_rows.append(full.reshape(N_HEADS_B, past * N_HEADS_B))
    bias_past = jnp.stack(past_rows, axis=0).reshape(n_tok * N_HEADS_B, past * N_HEADS_B)
    new_rows = []
    for t in range(n_tok):
        by_tok = jnp.stack([tab[:, max(t - s, 0)] for s in range(n_tok)], axis=1)
        ok = same_head[:, None, :] & (np.arange(n_tok) <= t)[None, :, None]
        new_rows.append(jnp.where(ok, by_tok[:, :, None], NEG_INF).reshape(N_HEADS_B, n_tok * N_HEADS_B))
    bias_new = jnp.stack(new_rows, axis=0).reshape(n_tok * N_HEADS_B, n_tok * N_HEADS_B)
    return rev, bias_past, bias_new


def _pad_ff(a, axis):
    pad = [(0, 0)] * a.ndim
    pad[axis] = (0, D_FF_PAD - D_FF)
    return jnp.pad(a, pad)


def kernel(x_prompt, x_sample, cache_k, cache_v, state_ffn_conv, page_table, c_prompt, c_sample,
           w_ada, b_ada, w_in, ln_sgu_g, ln_sgu_b, w_s, b_s, rel_bias, norm_a_g, norm_b_g, w_out,
           ln1_g, ln1_b, w_up, w_conv, b_conv, w_down, ln2_g, ln2_b):
    n_p, seq, _ = x_prompt.shape
    n_s, n_tok, _ = x_sample.shape
    n_pages = page_table.shape[1]
    past = n_pages * PAGE_SIZE
    assert w_ada.shape[0] == 1 and seq % MOBA_BLOCK == 0 and past % MOBA_BLOCK == 0 and n_tok <= CHUNK

    w_in_bf = w_in[0].astype(BF16)
    w_out_bf = w_out[0].astype(BF16)
    w_up_a = _pad_ff(w_up[0][:, :D_FF], 1).astype(BF16)
    w_up_b = _pad_ff(w_up[0][:, D_FF:], 1).astype(BF16)
    w_down_bf = _pad_ff(w_down[0], 0).astype(BF16)
    w_conv_p = _pad_ff(jnp.transpose(w_conv[0].reshape(CONV_W, 2, D_FF), (1, 0, 2)), 2)
    b_conv_p = _pad_ff(b_conv[0].reshape(2, D_FF), 1)
    row = lambda a: a.reshape(1, -1)
    ln_g, ln_b = row(ln_sgu_g[0]), row(ln_sgu_b[0])
    nag, nbg = row(norm_a_g[0]), row(norm_b_g[0])
    l1g, l1b, l2g, l2b = row(ln1_g[0]), row(ln1_b[0]), row(ln2_g[0]), row(ln2_b[0])
    bs_full = jnp.broadcast_to(b_s[0][:, :, None], (N_GROUPS_A, CHUNK, HEAD_DIM))
    tril_tok = np.tril(np.ones((n_tok, n_tok), np.float32))
    ws_tok = jnp.repeat(jnp.transpose(w_s[0][:, :n_tok, :n_tok] * tril_tok, (1, 2, 0)), HEAD_DIM, axis=2)
    bs_tok = jnp.repeat(b_s[0][:, :n_tok].T, HEAD_DIM, axis=1).reshape(n_tok, 1, W_A)
    rev_tab, bias_past, bias_new = _bias_tables(rel_bias, seq, past, n_tok)

    pad_rows = (-(n_s + n_p)) % SUBLANES
    c_all = jnp.concatenate([c_sample, c_prompt, jnp.zeros((pad_rows, D_MODEL), F32)], axis=0)
    mod = _ada(c_all, w_ada[0], b_ada[0])
    mod_s = mod[:n_s]
    mod_p = mod[n_s:n_s + n_p].reshape(n_p, 1, 6 * D_MODEL)

    xp = x_prompt.reshape(n_p * seq, D_MODEL)
    q_hi, q_lo, k_p, v_p, k_bf, v_bf, k_mean, gm_n = _proj_prompt(
        xp, mod_p, w_in_bf, ln_g, ln_b, w_s[0], bs_full, nag, tm=512, seq=seq)
    attn_p = _attn_prompt(q_hi, q_lo, k_bf, v_bf, k_mean.reshape(n_p, seq // MOBA_BLOCK, W_B), rev_tab,
                          n_seq=n_p, seq=seq)
    x1_p, h2_p = _mix_prompt(xp, gm_n, attn_p, mod_p, w_out_bf, nbg, l1g, l1b, tm=512, seq=seq)
    y_p, st_p = _ffn_prompt(h2_p, x1_p, mod_p, w_up_a, w_up_b, w_down_bf, w_conv_p, b_conv_p, l2g, l2b,
                            tm=512, seq=seq)
    tiles_per_seq = st_p.shape[0] // n_p
    conv_p = st_p[tiles_per_seq - 1::tiles_per_seq, :, SUBLANES - (CONV_W - 1):, :D_FF]
    conv_p = jnp.transpose(conv_p, (0, 2, 1, 3)).reshape(1, n_p, CONV_W - 1, 2 * D_FF)

    xs = x_sample.reshape(n_s, n_tok * D_MODEL)
    q_s, k_s, v_s, u_s, va_s = _proj_sample(xs, mod_s, w_in_bf, ln_g, ln_b, n_seq=n_s, n_tok=n_tok)
    heads = lambda a: a.reshape(n_s, n_tok, N_HEADS_B, HEAD_DIM)
    attn_s = _attn_sample(page_table, heads(q_s), heads(k_s), heads(v_s), bias_past, bias_new, cache_k, cache_v)
    x1_s, h2_s = _mix_sample(xs, u_s, va_s, ws_tok, bs_tok, nag, attn_s.reshape(n_s, n_tok * W_B), mod_s,
                             w_out_bf, nbg, l1g, l1b, n_seq=n_s, n_tok=n_tok)
    past_up = state_ffn_conv[0].reshape(n_s, CONV_W - 1, 2, D_FF)
    past_up = _pad_ff(jnp.transpose(past_up, (1, 2, 0, 3)), 3)
    y_s, st_s = _ffn_sample(h2_s, x1_s, mod_s, w_up_a, w_up_b, w_down_bf, w_conv_p, b_conv_p, past_up,
                            l2g, l2b, n_seq=n_s, n_tok=n_tok)
    conv_s = jnp.transpose(st_s[..., :D_FF], (2, 0, 1, 3)).reshape(1, n_s, CONV_W - 1, 2 * D_FF)

    heads_p = (1, n_p, seq, N_HEADS_B, HEAD_DIM)
    heads_s = (1, n_s, n_tok, N_HEADS_B, HEAD_DIM)
    return (y_p.reshape(n_p, seq, D_MODEL), y_s.reshape(n_s, n_tok, D_MODEL),
            k_p.reshape(heads_p), v_p.reshape(heads_p), conv_p,
            k_s.reshape(heads_s), v_s.reshape(heads_s), va_s.reshape(1, n_s, n_tok, W_A), conv_s)
```

```python
import functools

import numpy as np
import jax
import jax.numpy as jnp
from jax import lax
from jax.experimental import pallas as pl
from jax.experimental.pallas import tpu as pltpu

F32 = jnp.float32
BF16 = jnp.bfloat16

D_MODEL = 2048
HEAD_DIM = 128
N_HEADS_B = 8
N_GROUPS_A = 8
W_B = N_HEADS_B * HEAD_DIM
W_A = N_GROUPS_A * HEAD_DIM
CHUNK = 128
MOBA_BLOCK = 256
MOBA_TOPK = 3
NUM_BUCKETS = 32
MAX_DISTANCE = 1024
CONV_W = 3
D_FF = 5504
PAGE_SIZE = 128
ALPHA = 2.0 ** 0.25
LN_EPS = 1e-5
NEG_INF = -1e30

FF_TILE = 512
D_FF_PAD = ((D_FF + FF_TILE - 1) // FF_TILE) * FF_TILE
N_FF_TILES = D_FF_PAD // FF_TILE
SUBLANES = 8
BF16_ROWS = 16
MXU_COLS = 256
VMEM_LIMIT_BYTES = 56 * 1024 * 1024

_NT = (((1,), (1,)), ((), ()))


def _params(n_axes):
    return pltpu.CompilerParams(dimension_semantics=("arbitrary",) * n_axes,
                                vmem_limit_bytes=VMEM_LIMIT_BYTES)


def _layer_norm(x, g, b):
    mu = jnp.mean(x, axis=-1, keepdims=True)
    xc = x - mu
    var = jnp.mean(xc * xc, axis=-1, keepdims=True)
    return xc * lax.rsqrt(var + LN_EPS) * g + b


def _rms_norm(x, g):
    return x * lax.rsqrt(jnp.mean(x * x, axis=-1, keepdims=True) + LN_EPS) * g


def _split_bf16(x):
    hi = x.astype(BF16)
    lo = (x - hi.astype(F32)).astype(BF16)
    return hi, lo


def _dot(a, b):
    return jnp.dot(a, b, preferred_element_type=F32)


def _dot_nt(a, b):
    return lax.dot_general(a, b, _NT, preferred_element_type=F32)


def _block_scores(q_hi, q_lo, km):
    km_hi, km_lo = _split_bf16(km)
    return _dot_nt(q_hi, km_hi) + _dot_nt(q_hi, km_lo) + _dot_nt(q_lo, km_hi)


def _topk_select(s, n_valid, axis):
    idx = lax.broadcasted_iota(jnp.int32, s.shape, axis)
    rank = jnp.zeros(s.shape, F32)
    for i in range(n_valid):
        si = lax.slice_in_dim(s, i, i + 1, axis=axis)
        beats = (si > s) | ((si == s) & (i < idx))
        rank = rank + beats.astype(F32)
    return ((rank < MOBA_TOPK) & (idx < n_valid)).astype(F32)


def _ada_kernel(c_ref, w_ref, b_ref, o_ref):
    c = c_ref[...]
    a = jax.nn.silu(c).astype(BF16)
    o_ref[...] = _dot(a, w_ref[...].astype(BF16)) + b_ref[...]


def _ada(c_all, w_ada, b_ada):
    rows = c_all.shape[0]
    n_out = w_ada.shape[1]
    tn = 512
    return pl.pallas_call(
        _ada_kernel,
        grid=(n_out // tn,),
        in_specs=[pl.BlockSpec((rows, D_MODEL), lambda j: (0, 0)),
                  pl.BlockSpec((D_MODEL, tn), lambda j: (0, j)),
                  pl.BlockSpec((1, tn), lambda j: (0, j))],
        out_specs=pl.BlockSpec((rows, tn), lambda j: (0, j)),
        out_shape=jax.ShapeDtypeStruct((rows, n_out), F32),
        compiler_params=_params(1),
        name="ada",
    )(c_all, w_ada, b_ada.reshape(1, n_out))


def _proj_prompt_kernel(x_ref, sc_ref, sh_ref, w_ref, lng_ref, lnb_ref, ws_ref, bs_ref, nag_ref,
                        qhi_ref, qlo_ref, k_ref, v_ref, kbf_ref, vbf_ref, km_ref, gm_ref,
                        h_scr, u_scr, gm_scr):
    j = pl.program_id(1)
    tm = x_ref.shape[0]

    @pl.when(j == 0)
    def _():
        h_scr[...] = (x_ref[...] * (1.0 + sc_ref[...]) + sh_ref[...]).astype(BF16)

    acc = _dot(h_scr[...], w_ref[...])

    @pl.when(j == 0)
    def _():
        q_hi, q_lo = _split_bf16(acc * (HEAD_DIM ** -0.5))
        qhi_ref[...] = q_hi
        qlo_ref[...] = q_lo

    @pl.when(j == 1)
    def _():
        k_ref[...] = acc
        kbf_ref[...] = acc.astype(BF16)
        km_ref[...] = jnp.mean(acc.reshape(tm // MOBA_BLOCK, MOBA_BLOCK, W_B), axis=1)

    @pl.when(j == 2)
    def _():
        v_ref[...] = acc
        vbf_ref[...] = acc.astype(BF16)

    @pl.when(j == 3)
    def _():
        u_scr[...] = jax.nn.gelu(acc)

    @pl.when(j == 4)
    def _():
        va = _layer_norm(jax.nn.gelu(acc), lng_ref[...], lnb_ref[...]).astype(BF16)
        n_chunks = tm // CHUNK
        row = lax.broadcasted_iota(jnp.int32, (CHUNK, CHUNK), 0)
        col = lax.broadcasted_iota(jnp.int32, (CHUNK, CHUNK), 1)
        for g in range(N_GROUPS_A):
            gs = slice(g * HEAD_DIM, (g + 1) * HEAD_DIM)
            w_g = jnp.where(row >= col, ws_ref[g], 0.0).astype(BF16)
            v_g = jnp.concatenate([va[c * CHUNK:(c + 1) * CHUNK, gs] for c in range(n_chunks)], axis=1)
            m_g = _dot(w_g, v_g)
            for c in range(n_chunks):
                rs = slice(c * CHUNK, (c + 1) * CHUNK)
                mixed = m_g[:, c * HEAD_DIM:(c + 1) * HEAD_DIM] + bs_ref[g]
                gm_scr[rs, gs] = u_scr[rs, gs] * mixed
        gm_ref[...] = _rms_norm(gm_scr[...], nag_ref[...]).astype(BF16)


def _proj_prompt(x, mod_p, w_in, ln_g, ln_b, w_s, bs_full, norm_a_g, *, tm, seq):
    rows = x.shape[0]
    tiles_per_seq = seq // tm
    n_col = w_in.shape[1] // W_B
    row_blk = lambda i, j: (i, 0)
    vec = lambda i, j: (0, 0)
    out_shape = (
        jax.ShapeDtypeStruct((rows, W_B), BF16),
        jax.ShapeDtypeStruct((rows, W_B), BF16),
        jax.ShapeDtypeStruct((rows, W_B), F32),
        jax.ShapeDtypeStruct((rows, W_B), F32),
        jax.ShapeDtypeStruct((rows, W_B), BF16),
        jax.ShapeDtypeStruct((rows, W_B), BF16),
        jax.ShapeDtypeStruct((rows // tm, tm // MOBA_BLOCK, W_B), F32),
        jax.ShapeDtypeStruct((rows, W_A), BF16),
    )
    out_specs = (
        pl.BlockSpec((tm, W_B), row_blk), pl.BlockSpec((tm, W_B), row_blk),
        pl.BlockSpec((tm, W_B), row_blk), pl.BlockSpec((tm, W_B), row_blk),
        pl.BlockSpec((tm, W_B), row_blk), pl.BlockSpec((tm, W_B), row_blk),
        pl.BlockSpec((None, tm // MOBA_BLOCK, W_B), lambda i, j: (i, 0, 0)),
        pl.BlockSpec((tm, W_A), row_blk),
    )
    return pl.pallas_call(
        _proj_prompt_kernel,
        grid=(rows // tm, n_col),
        in_specs=[pl.BlockSpec((tm, D_MODEL), row_blk),
                  pl.BlockSpec((None, 1, D_MODEL), lambda i, j: (i // tiles_per_seq, 0, 1)),
                  pl.BlockSpec((None, 1, D_MODEL), lambda i, j: (i // tiles_per_seq, 0, 0)),
                  pl.BlockSpec((D_MODEL, W_B), lambda i, j: (0, j)),
                  pl.BlockSpec((1, W_A), vec), pl.BlockSpec((1, W_A), vec),
                  pl.BlockSpec((N_GROUPS_A, CHUNK, CHUNK), lambda i, j: (0, 0, 0)),
                  pl.BlockSpec((N_GROUPS_A, CHUNK, HEAD_DIM), lambda i, j: (0, 0, 0)),
                  pl.BlockSpec((1, W_A), vec)],
        out_specs=out_specs,
        out_shape=out_shape,
        scratch_shapes=[pltpu.VMEM((tm, D_MODEL), BF16), pltpu.VMEM((tm, W_A), F32),
                        pltpu.VMEM((tm, W_A), F32)],
        compiler_params=_params(2),
        name="proj_prompt",
    )(x, mod_p, mod_p, w_in, ln_g, ln_b, w_s, bs_full, norm_a_g)


def _proj_sample_kernel(x_ref, sc_ref, sh_ref, w_ref, lng_ref, lnb_ref,
                        q_ref, k_ref, v_ref, u_ref, va_ref, h_scr):
    j = pl.program_id(1)

    @pl.when(j == 0)
    def _():
        h_scr[...] = (x_ref[...] * (1.0 + sc_ref[...]) + sh_ref[...]).astype(BF16)

    acc = _dot(h_scr[...], w_ref[...])

    @pl.when(j == 0)
    def _():
        q_ref[...] = acc * (HEAD_DIM ** -0.5)

    @pl.when(j == 1)
    def _():
        k_ref[...] = acc

    @pl.when(j == 2)
    def _():
        v_ref[...] = acc

    @pl.when(j == 3)
    def _():
        u_ref[...] = jax.nn.gelu(acc)

    @pl.when(j == 4)
    def _():
        va_ref[...] = _layer_norm(jax.nn.gelu(acc), lng_ref[...], lnb_ref[...])


def _proj_sample(x2d, mod_s, w_in, ln_g, ln_b, *, n_seq, n_tok):
    n_col = w_in.shape[1] // W_B
    tok_blk = lambda t, j: (0, t)
    vec = lambda t, j: (0, 0)
    out = jax.ShapeDtypeStruct((n_seq, n_tok * W_B), F32)
    return pl.pallas_call(
        _proj_sample_kernel,
        grid=(n_tok, n_col),
        in_specs=[pl.BlockSpec((n_seq, D_MODEL), tok_blk),
                  pl.BlockSpec((n_seq, D_MODEL), lambda t, j: (0, 1)),
                  pl.BlockSpec((n_seq, D_MODEL), lambda t, j: (0, 0)),
                  pl.BlockSpec((D_MODEL, W_B), lambda t, j: (0, j)),
                  pl.BlockSpec((1, W_A), vec), pl.BlockSpec((1, W_A), vec)],
        out_specs=tuple(pl.BlockSpec((n_seq, W_B), tok_blk) for _ in range(5)),
        out_shape=(out,) * 5,
        scratch_shapes=[pltpu.VMEM((n_seq, D_MODEL), BF16)],
        compiler_params=_params(2),
        name="proj_sample",
    )(x2d, mod_s, mod_s, w_in, ln_g, ln_b)


def _attn_prompt_kernel(qhi_ref, qlo_ref, k_ref, v_ref, km_ref, rev_ref, o_ref, bias_scr):
    n_blocks = k_ref.shape[0] // MOBA_BLOCK

    @pl.when(pl.program_id(1) == 0)
    def _():
        for d in range(n_blocks):
            r = jnp.broadcast_to(rev_ref[d:d + 1, :], (MOBA_BLOCK, 2 * MOBA_BLOCK))
            rolled = pltpu.roll(r, 0, 1, stride=1, stride_axis=0)
            bias_scr[d] = rolled[:, MOBA_BLOCK:]

    km = km_ref[...]
    row = lax.broadcasted_iota(jnp.int32, (MOBA_BLOCK, MOBA_BLOCK), 0)
    col = lax.broadcasted_iota(jnp.int32, (MOBA_BLOCK, MOBA_BLOCK), 1)
    for qb in range(n_blocks):
        qs = slice(qb * MOBA_BLOCK, (qb + 1) * MOBA_BLOCK)
        n_keys = (qb + 1) * MOBA_BLOCK
        q_hi = qhi_ref[qs, :]
        logits = _dot_nt(q_hi, k_ref[0:n_keys, :])
        sel = None
        if qb > MOBA_TOPK:
            sel = _topk_select(_block_scores(q_hi, qlo_ref[qs, :], km), qb, axis=1)
        pieces = []
        for jb in range(qb + 1):
            lj = logits[:, jb * MOBA_BLOCK:(jb + 1) * MOBA_BLOCK] + bias_scr[qb - jb]
            if jb == qb:
                lj = jnp.where(row >= col, lj, NEG_INF)
            elif sel is not None:
                lj = jnp.where(sel[:, jb:jb + 1] > 0.0, lj, NEG_INF)
            pieces.append(lj)
        lg = jnp.concatenate(pieces, axis=1) if len(pieces) > 1 else pieces[0]
        m = jnp.max(lg, axis=-1, keepdims=True)
        p = jnp.exp(lg - m)
        z = jnp.sum(p, axis=-1, keepdims=True)
        o_ref[qs, :] = _dot(p.astype(BF16), v_ref[0:n_keys, :]) / z


def _attn_prompt(q_hi, q_lo, k_bf, v_bf, k_mean, rev_tab, *, n_seq, seq):
    n_blocks = seq // MOBA_BLOCK
    qkv = pl.BlockSpec((seq, HEAD_DIM), lambda h, n: (n, h))
    return pl.pallas_call(
        _attn_prompt_kernel,
        grid=(N_HEADS_B, n_seq),
        in_specs=[qkv, qkv, qkv, qkv,
                  pl.BlockSpec((None, n_blocks, HEAD_DIM), lambda h, n: (n, 0, h)),
                  pl.BlockSpec((None, n_blocks, 2 * MOBA_BLOCK), lambda h, n: (h, 0, 0))],
        out_specs=pl.BlockSpec((seq, HEAD_DIM), lambda h, n: (n, h)),
        out_shape=jax.ShapeDtypeStruct((n_seq * seq, W_B), F32),
        scratch_shapes=[pltpu.VMEM((n_blocks, MOBA_BLOCK, MOBA_BLOCK), F32)],
        compiler_params=_params(2),
        name="attn_prompt",
    )(q_hi, q_lo, k_bf, v_bf, k_mean, rev_tab)


def _attn_sample_kernel(n_pages, pt_ref, q_ref, kn_ref, vn_ref, bias_ref, biasn_ref, *refs):
    k_pages = refs[:n_pages]
    v_pages = refs[n_pages:2 * n_pages]
    o_ref = refs[2 * n_pages]
    n_tok = q_ref.shape[0]
    rows = n_tok * N_HEADS_B
    cols = PAGE_SIZE * N_HEADS_B
    pages_per_block = MOBA_BLOCK // PAGE_SIZE
    n_blocks = n_pages // pages_per_block

    q = q_ref[...].reshape(rows, HEAD_DIM)
    q_bf = q.astype(BF16)

    logit_pages = []
    k_sums = []
    for p in range(n_pages):
        kp = k_pages[p][...]
        k_sums.append(jnp.sum(kp, axis=0))
        logit_pages.append(_dot_nt(q_bf, kp.reshape(cols, HEAD_DIM).astype(BF16)))

    blk_lane = lax.broadcasted_iota(jnp.int32, (rows, n_blocks), 1)
    scores = jnp.zeros((rows, n_blocks), F32)
    for b in range(n_blocks):
        km = sum(k_sums[b * pages_per_block:(b + 1) * pages_per_block]) * (1.0 / MOBA_BLOCK)
        s_b = jnp.sum(q * jnp.concatenate([km] * n_tok, axis=0), axis=1, keepdims=True)
        scores = jnp.where(blk_lane == b, s_b, scores)
    sel = _topk_select(scores, n_blocks, axis=1)

    masked = []
    for p in range(n_pages):
        b = p // pages_per_block
        lp = logit_pages[p] + bias_ref[:, p * cols:(p + 1) * cols]
        masked.append(jnp.where(sel[:, b:b + 1] > 0.0, lp, NEG_INF))
    ln = _dot_nt(q_bf, kn_ref[...].reshape(rows, HEAD_DIM).astype(BF16)) + biasn_ref[...]

    m = jnp.maximum(jnp.max(functools.reduce(jnp.maximum, masked), axis=-1, keepdims=True),
                    jnp.max(ln, axis=-1, keepdims=True))
    p_new = jnp.exp(ln - m)
    acc = _dot(p_new.astype(BF16), vn_ref[...].reshape(rows, HEAD_DIM).astype(BF16))
    p_sum = None
    for p in range(n_pages):
        pp = jnp.exp(masked[p] - m)
        p_sum = pp if p_sum is None else p_sum + pp
        acc = acc + _dot(pp.astype(BF16), v_pages[p][...].reshape(cols, HEAD_DIM).astype(BF16))
    z = jnp.sum(p_sum, axis=-1, keepdims=True) + jnp.sum(p_new, axis=-1, keepdims=True)
    o_ref[...] = (acc / z).reshape(n_tok, N_HEADS_B, HEAD_DIM)


def _attn_sample(page_table, q, k_new, v_new, bias_past, bias_new, cache_k, cache_v):
    n_seq, n_pages = page_table.shape
    n_tok = q.shape[1]
    tok = pl.BlockSpec((None, n_tok, N_HEADS_B, HEAD_DIM), lambda n, pt: (n, 0, 0, 0))

    def page_spec(p):
        return pl.BlockSpec((None, None, PAGE_SIZE, N_HEADS_B, HEAD_DIM),
                            lambda n, pt: (0, pt[n * n_pages + p], 0, 0, 0))

    grid_spec = pltpu.PrefetchScalarGridSpec(
        num_scalar_prefetch=1,
        grid=(n_seq,),
        in_specs=[tok, tok, tok,
                  pl.BlockSpec(bias_past.shape, lambda n, pt: (0, 0)),
                  pl.BlockSpec(bias_new.shape, lambda n, pt: (0, 0))]
                 + [page_spec(p) for p in range(n_pages)] * 2,
        out_specs=tok,
    )
    return pl.pallas_call(
        functools.partial(_attn_sample_kernel, n_pages),
        grid_spec=grid_spec,
        out_shape=jax.ShapeDtypeStruct((n_seq, n_tok, N_HEADS_B, HEAD_DIM), F32),
        compiler_params=_params(1),
        name="attn_sample",
    )(page_table.reshape(-1), q, k_new, v_new, bias_past, bias_new,
      *([cache_k] * n_pages), *([cache_v] * n_pages))


def _mix_tail(x, gm_n, attn, g1, sc2, sh2, wo_ref, nbg_ref, l1g_ref, l1b_ref, x1_ref, h2_ref):
    a_n = _rms_norm(attn, nbg_ref[...]).astype(BF16)
    o = _dot(gm_n, wo_ref[0:W_A, :]) + _dot(a_n, wo_ref[W_A:W_A + W_B, :])
    x1 = _layer_norm(ALPHA * x + g1 * o, l1g_ref[...], l1b_ref[...])
    x1_ref[...] = x1
    h2_ref[...] = (x1 * (1.0 + sc2) + sh2).astype(BF16)


def _mix_prompt_kernel(x_ref, gm_ref, at_ref, g1_ref, sc2_ref, sh2_ref, wo_ref, nbg_ref, l1g_ref, l1b_ref,
                       x1_ref, h2_ref):
    _mix_tail(x_ref[...], gm_ref[...], at_ref[...], g1_ref[...], sc2_ref[...], sh2_ref[...],
              wo_ref, nbg_ref, l1g_ref, l1b_ref, x1_ref, h2_ref)


def _mix_prompt(x, gm_n, attn, mod_p, w_out, norm_b_g, ln1_g, ln1_b, *, tm, seq):
    rows = x.shape[0]
    tiles_per_seq = seq // tm
    row_blk = lambda i: (i, 0)
    vec = lambda i: (0, 0)
    mod = lambda c: pl.BlockSpec((None, 1, D_MODEL), lambda i: (i // tiles_per_seq, 0, c))
    return pl.pallas_call(
        _mix_prompt_kernel,
        grid=(rows // tm,),
        in_specs=[pl.BlockSpec((tm, D_MODEL), row_blk), pl.BlockSpec((tm, W_A), row_blk),
                  pl.BlockSpec((tm, W_B), row_blk), mod(2), mod(4), mod(3),
                  pl.BlockSpec((D_MODEL, D_MODEL), vec), pl.BlockSpec((1, W_B), vec),
                  pl.BlockSpec((1, D_MODEL), vec), pl.BlockSpec((1, D_MODEL), vec)],
        out_specs=(pl.BlockSpec((tm, D_MODEL), row_blk), pl.BlockSpec((tm, D_MODEL), row_blk)),
        out_shape=(jax.ShapeDtypeStruct((rows, D_MODEL), F32), jax.ShapeDtypeStruct((rows, D_MODEL), BF16)),
        compiler_params=_params(1),
        name="mix_prompt",
    )(x, gm_n, attn, mod_p, mod_p, mod_p, w_out, norm_b_g, ln1_g, ln1_b)


def _mix_sample_kernel(x_ref, u_ref, va_ref, ws_ref, bs_ref, nag_ref, at_ref, g1_ref, sc2_ref, sh2_ref,
                       wo_ref, nbg_ref, l1g_ref, l1b_ref, x1_ref, h2_ref):
    n_tok = ws_ref.shape[0]
    mixed = bs_ref[...]
    for s in range(n_tok):
        mixed = mixed + ws_ref[s:s + 1, :] * va_ref[:, s * W_A:(s + 1) * W_A]
    gm_n = _rms_norm(u_ref[...] * mixed, nag_ref[...]).astype(BF16)
    _mix_tail(x_ref[...], gm_n, at_ref[...], g1_ref[...], sc2_ref[...], sh2_ref[...],
              wo_ref, nbg_ref, l1g_ref, l1b_ref, x1_ref, h2_ref)


def _mix_sample(x2d, u, va, ws_tok, bs_tok, norm_a_g, attn2d, mod_s, w_out, norm_b_g, ln1_g, ln1_b,
                *, n_seq, n_tok):
    tok_blk = lambda t: (0, t)
    vec = lambda t: (0, 0)
    mod = lambda c: pl.BlockSpec((n_seq, D_MODEL), lambda t: (0, c))
    out_blk = pl.BlockSpec((n_seq, D_MODEL), lambda t: (t, 0))
    return pl.pallas_call(
        _mix_sample_kernel,
        grid=(n_tok,),
        in_specs=[pl.BlockSpec((n_seq, D_MODEL), tok_blk), pl.BlockSpec((n_seq, W_A), tok_blk),
                  pl.BlockSpec((n_seq, n_tok * W_A), vec),
                  pl.BlockSpec((None, n_tok, W_A), lambda t: (t, 0, 0)),
                  pl.BlockSpec((None, 1, W_A), lambda t: (t, 0, 0)),
                  pl.BlockSpec((1, W_A), vec),
                  pl.BlockSpec((n_seq, W_B), tok_blk), mod(2), mod(4), mod(3),
                  pl.BlockSpec((D_MODEL, D_MODEL), vec), pl.BlockSpec((1, W_B), vec),
                  pl.BlockSpec((1, D_MODEL), vec), pl.BlockSpec((1, D_MODEL), vec)],
        out_specs=(out_blk, out_blk),
        out_shape=(jax.ShapeDtypeStruct((n_tok * n_seq, D_MODEL), F32),
                   jax.ShapeDtypeStruct((n_tok * n_seq, D_MODEL), BF16)),
        compiler_params=_params(1),
        name="mix_sample",
    )(x2d, u, va, ws_tok, bs_tok, norm_a_g, attn2d, mod_s, mod_s, mod_s, w_out, norm_b_g, ln1_g, ln1_b)


def _ffn_act(conv_a, conv_b):
    return (jax.nn.silu(conv_a) * conv_b).astype(BF16)


def _ffn_prompt_kernel(tiles_per_seq, h_ref, halo_ref, x1_ref, g2_ref, wa_ref, wb_ref, wd_ref, wc_ref, bc_ref,
                       l2g_ref, l2b_ref, y_ref, st_ref, acc_scr, h_scr):
    i = pl.program_id(0)
    j = pl.program_id(1)
    tm = h_ref.shape[0]
    n_halo = halo_ref.shape[0]

    @pl.when(j == 0)
    def _():
        seq_start = (i % tiles_per_seq) == 0
        h_scr[0:n_halo, :] = jnp.where(seq_start, jnp.zeros_like(halo_ref[...]), halo_ref[...])
        h_scr[n_halo:n_halo + tm, :] = h_ref[...]
        acc_scr[...] = jnp.zeros_like(acc_scr)

    h = h_scr[...]

    def conv(half, w_ref, cs):
        up = _dot(h, w_ref[:, cs])
        st_ref[half, :, cs] = up[n_halo + tm - SUBLANES:n_halo + tm, :]
        out = bc_ref[half:half + 1, cs]
        for c in range(CONV_W):
            lag = CONV_W - 1 - c
            out = out + up[n_halo - lag:n_halo - lag + tm, :] * wc_ref[half, c:c + 1, cs]
        return out

    acts = []
    for c0 in range(0, FF_TILE, MXU_COLS):
        cs = slice(c0, c0 + MXU_COLS)
        acts.append(_ffn_act(conv(0, wa_ref, cs), conv(1, wb_ref, cs)))
    acc_scr[...] += _dot(jnp.concatenate(acts, axis=1), wd_ref[...])

    @pl.when(j == pl.num_programs(1) - 1)
    def _():
        y_ref[...] = _layer_norm(ALPHA * x1_ref[...] + g2_ref[...] * acc_scr[...], l2g_ref[...], l2b_ref[...])


def _ffn_prompt(h2, x1, mod_p, w_up_a, w_up_b, w_down, w_conv, b_conv, ln2_g, ln2_b, *, tm, seq):
    rows = h2.shape[0]
    tiles_per_seq = seq // tm
    row_blk = lambda i, j: (i, 0)
    vec = lambda i, j: (0, 0)
    halo_blk = lambda i, j: (jnp.maximum(i * (tm // BF16_ROWS) - 1, 0), 0)
    return pl.pallas_call(
        functools.partial(_ffn_prompt_kernel, tiles_per_seq),
        grid=(rows // tm, N_FF_TILES),
        in_specs=[pl.BlockSpec((tm, D_MODEL), row_blk), pl.BlockSpec((BF16_ROWS, D_MODEL), halo_blk),
                  pl.BlockSpec((tm, D_MODEL), row_blk),
                  pl.BlockSpec((None, 1, D_MODEL), lambda i, j: (i // tiles_per_seq, 0, 5)),
                  pl.BlockSpec((D_MODEL, FF_TILE), lambda i, j: (0, j)),
                  pl.BlockSpec((D_MODEL, FF_TILE), lambda i, j: (0, j)),
                  pl.BlockSpec((FF_TILE, D_MODEL), lambda i, j: (j, 0)),
                  pl.BlockSpec((2, CONV_W, FF_TILE), lambda i, j: (0, 0, j)),
                  pl.BlockSpec((2, FF_TILE), lambda i, j: (0, j)),
                  pl.BlockSpec((1, D_MODEL), vec), pl.BlockSpec((1, D_MODEL), vec)],
        out_specs=(pl.BlockSpec((tm, D_MODEL), row_blk),
                   pl.BlockSpec((None, 2, SUBLANES, FF_TILE), lambda i, j: (i, 0, 0, j))),
        out_shape=(jax.ShapeDtypeStruct((rows, D_MODEL), F32),
                   jax.ShapeDtypeStruct((rows // tm, 2, SUBLANES, D_FF_PAD), F32)),
        scratch_shapes=[pltpu.VMEM((tm, D_MODEL), F32), pltpu.VMEM((BF16_ROWS + tm, D_MODEL), BF16)],
        compiler_params=_params(2),
        name="ffn_prompt",
    )(h2, h2, x1, mod_p, w_up_a, w_up_b, w_down, w_conv, b_conv, ln2_g, ln2_b)


def _ffn_sample_kernel(n_tok, h_ref, x1_ref, g2_ref, wa_ref, wb_ref, wd_ref, wc_ref, bc_ref, past_ref,
                       l2g_ref, l2b_ref, y_ref, st_ref, acc_scr):
    j = pl.program_id(0)
    n_seq = g2_ref.shape[0]
    h = h_ref[...]

    def conv(half, w_ref):
        up = _dot(h, w_ref[...])
        full = [past_ref[r, half] for r in range(CONV_W - 1)]
        full += [up[t * n_seq:(t + 1) * n_seq, :] for t in range(n_tok)]
        for r in range(CONV_W - 1):
            st_ref[r, half] = full[n_tok + r]
        w = wc_ref[half]
        b = bc_ref[half:half + 1, :]
        return jnp.concatenate(
            [b + sum(full[t + c] * w[c:c + 1, :] for c in range(CONV_W)) for t in range(n_tok)], axis=0)

    part = _dot(_ffn_act(conv(0, wa_ref), conv(1, wb_ref)), wd_ref[...])

    @pl.when(j == 0)
    def _():
        acc_scr[...] = part

    @pl.when(j > 0)
    def _():
        acc_scr[...] += part

    @pl.when(j == pl.num_programs(0) - 1)
    def _():
        g2 = g2_ref[...]
        for t in range(n_tok):
            rs = slice(t * n_seq, (t + 1) * n_seq)
            y_ref[:, t * D_MODEL:(t + 1) * D_MODEL] = _layer_norm(
                ALPHA * x1_ref[rs, :] + g2 * acc_scr[rs, :], l2g_ref[...], l2b_ref[...])


def _ffn_sample(h2, x1, mod_s, w_up_a, w_up_b, w_down, w_conv, b_conv, past, ln2_g, ln2_b, *, n_seq, n_tok):
    rows = n_seq * n_tok
    full = lambda j: (0, 0)
    state_blk = pl.BlockSpec((CONV_W - 1, 2, n_seq, FF_TILE), lambda j: (0, 0, 0, j))
    return pl.pallas_call(
        functools.partial(_ffn_sample_kernel, n_tok),
        grid=(N_FF_TILES,),
        in_specs=[pl.BlockSpec((rows, D_MODEL), full), pl.BlockSpec((rows, D_MODEL), full),
                  pl.BlockSpec((n_seq, D_MODEL), lambda j: (0, 5)),
                  pl.BlockSpec((D_MODEL, FF_TILE), lambda j: (0, j)),
                  pl.BlockSpec((D_MODEL, FF_TILE), lambda j: (0, j)),
                  pl.BlockSpec((FF_TILE, D_MODEL), lambda j: (j, 0)),
                  pl.BlockSpec((2, CONV_W, FF_TILE), lambda j: (0, 0, j)),
                  pl.BlockSpec((2, FF_TILE), lambda j: (0, j)),
                  state_blk,
                  pl.BlockSpec((1, D_MODEL), full), pl.BlockSpec((1, D_MODEL), full)],
        out_specs=(pl.BlockSpec((n_seq, n_tok * D_MODEL), full), state_blk),
        out_shape=(jax.ShapeDtypeStruct((n_seq, n_tok * D_MODEL), F32),
                   jax.ShapeDtypeStruct((CONV_W - 1, 2, n_seq, D_FF_PAD), F32)),
        scratch_shapes=[pltpu.VMEM((rows, D_MODEL), F32)],
        compiler_params=_params(1),
        name="ffn_sample",
    )(h2, x1, mod_s, w_up_a, w_up_b, w_down, w_conv, b_conv, past, ln2_g, ln2_b)


def _t5_bucket_table(n_rel):
    max_exact = NUM_BUCKETS // 2
    ratio = MAX_DISTANCE // max_exact
    log_ratio = ratio.bit_length() - 1
    assert 1 << log_ratio == ratio
    steps = NUM_BUCKETS - max_exact
    out = np.zeros((n_rel,), np.int32)
    for n in range(n_rel):
        if n < max_exact:
            out[n] = n
            continue
        k = 0
        while k + 1 < steps and n ** steps >= (max_exact ** steps) << (log_ratio * (k + 1)):
            k += 1
        out[n] = min(max_exact + k, NUM_BUCKETS - 1)
    return out


def _bias_tables(rel_bias, seq, past, n_tok):
    n_rel = max(seq, past + n_tok)
    onehot = np.zeros((NUM_BUCKETS, n_rel), np.float32)
    onehot[_t5_bucket_table(n_rel), np.arange(n_rel)] = 1.0
    tab = jnp.dot(rel_bias.astype(F32).T, onehot, precision=lax.Precision.HIGHEST)
    n_blocks = seq // MOBA_BLOCK
    tab_pad = jnp.pad(tab, ((0, 0), (MOBA_BLOCK, 1)))
    rev = jnp.stack([tab_pad[:, d * MOBA_BLOCK + 1:(d + 2) * MOBA_BLOCK + 1][:, ::-1]
                     for d in range(n_blocks)], axis=1)
    same_head = np.eye(N_HEADS_B, dtype=bool)
    past_rows = []
    for t in range(n_tok):
        by_key = tab[:, t + 1:t + 1 + past][:, ::-1]
        full = jnp.where(same_head[:, None, :], by_key[:, :, None], NEG_INF)
        past_rows.append(full.reshape(N_HEADS_B, past * N_HEADS_B))
    bias_past = jnp.stack(past_rows, axis=0).reshape(n_tok * N_HEADS_B, past * N_HEADS_B)
    new_rows = []
    for t in range(n_tok):
        by_tok = jnp.stack([tab[:, max(t - s, 0)] for s in range(n_tok)], axis=1)
        ok = same_head[:, None, :] & (np.arange(n_tok) <= t)[None, :, None]
        new_rows.append(jnp.where(ok, by_tok[:, :, None], NEG_INF).reshape(N_HEADS_B, n_tok * N_HEADS_B))
    bias_new = jnp.stack(new_rows, axis=0).reshape(n_tok * N_HEADS_B, n_tok * N_HEADS_B)
    return rev, bias_past, bias_new


def _pad_ff(a, axis):
    pad = [(0, 0)] * a.ndim
    pad[axis] = (0, D_FF_PAD - D_FF)
    return jnp.pad(a, pad)


def kernel(x_prompt, x_sample, cache_k, cache_v, state_ffn_conv, page_table, c_prompt, c_sample,
           w_ada, b_ada, w_in, ln_sgu_g, ln_sgu_b, w_s, b_s, rel_bias, norm_a_g, norm_b_g, w_out,
           ln1_g, ln1_b, w_up, w_conv, b_conv, w_down, ln2_g, ln2_b):
    n_p, seq, _ = x_prompt.shape
    n_s, n_tok, _ = x_sample.shape
    n_pages = page_table.shape[1]
    past = n_pages * PAGE_SIZE
    assert w_ada.shape[0] == 1 and seq % MOBA_BLOCK == 0 and past % MOBA_BLOCK == 0 and n_tok <= CHUNK

    w_in_bf = w_in[0].astype(BF16)
    w_out_bf = w_out[0].astype(BF16)
    w_up_a = _pad_ff(w_up[0][:, :D_FF], 1).astype(BF16)
    w_up_b = _pad_ff(w_up[0][:, D_FF:], 1).astype(BF16)
    w_down_bf = _pad_ff(w_down[0], 0).astype(BF16)
    w_conv_p = _pad_ff(jnp.transpose(w_conv[0].reshape(CONV_W, 2, D_FF), (1, 0, 2)), 2)
    b_conv_p = _pad_ff(b_conv[0].reshape(2, D_FF), 1)
    row = lambda a: a.reshape(1, -1)
    ln_g, ln_b = row(ln_sgu_g[0]), row(ln_sgu_b[0])
    nag, nbg = row(norm_a_g[0]), row(norm_b_g[0])
    l1g, l1b, l2g, l2b = row(ln1_g[0]), row(ln1_b[0]), row(ln2_g[0]), row(ln2_b[0])
    bs_full = jnp.broadcast_to(b_s[0][:, :, None], (N_GROUPS_A, CHUNK, HEAD_DIM))
    tril_tok = np.tril(np.ones((n_tok, n_tok), np.float32))
    ws_tok = jnp.repeat(jnp.transpose(w_s[0][:, :n_tok, :n_tok] * tril_tok, (1, 2, 0)), HEAD_DIM, axis=2)
    bs_tok = jnp.repeat(b_s[0][:, :n_tok].T, HEAD_DIM, axis=1).reshape(n_tok, 1, W_A)
    rev_tab, bias_past, bias_new = _bias_tables(rel_bias, seq, past, n_tok)

    pad_rows = (-(n_s + n_p)) % SUBLANES
    c_all = jnp.concatenate([c_sample, c_prompt, jnp.zeros((pad_rows, D_MODEL), F32)], axis=0)
    mod = _ada(c_all, w_ada[0], b_ada[0])
    mod_s = mod[:n_s]
    mod_p = mod[n_s:n_s + n_p].reshape(n_p, 1, 6 * D_MODEL)

    xp = x_prompt.reshape(n_p * seq, D_MODEL)
    q_hi, q_lo, k_p, v_p, k_bf, v_bf, k_mean, gm_n = _proj_prompt(
        xp, mod_p, w_in_bf, ln_g, ln_b, w_s[0], bs_full, nag, tm=512, seq=seq)
    attn_p = _attn_prompt(q_hi, q_lo, k_bf, v_bf, k_mean.reshape(n_p, seq // MOBA_BLOCK, W_B), rev_tab,
                          n_seq=n_p, seq=seq)
    x1_p, h2_p = _mix_prompt(xp, gm_n, attn_p, mod_p, w_out_bf, nbg, l1g, l1b, tm=512, seq=seq)
    y_p, st_p = _ffn_prompt(h2_p, x1_p, mod_p, w_up_a, w_up_b, w_down_bf, w_conv_p, b_conv_p, l2g, l2b,
                            tm=512, seq=seq)
    tiles_per_seq = st_p.shape[0] // n_p
    conv_p = st_p[tiles_per_seq - 1::tiles_per_seq, :, SUBLANES - (CONV_W - 1):, :D_FF]
    conv_p = jnp.transpose(conv_p, (0, 2, 1, 3)).reshape(1, n_p, CONV_W - 1, 2 * D_FF)

    xs = x_sample.reshape(n_s, n_tok * D_MODEL)
    q_s, k_s, v_s, u_s, va_s = _proj_sample(xs, mod_s, w_in_bf, ln_g, ln_b, n_seq=n_s, n_tok=n_tok)
    heads = lambda a: a.reshape(n_s, n_tok, N_HEADS_B, HEAD_DIM)
    attn_s = _attn_sample(page_table, heads(q_s), heads(k_s), heads(v_s), bias_past, bias_new, cache_k, cache_v)
    x1_s, h2_s = _mix_sample(xs, u_s, va_s, ws_tok, bs_tok, nag, attn_s.reshape(n_s, n_tok * W_B), mod_s,
                             w_out_bf, nbg, l1g, l1b, n_seq=n_s, n_tok=n_tok)
    past_up = state_ffn_conv[0].reshape(n_s, CONV_W - 1, 2, D_FF)
    past_up = _pad_ff(jnp.transpose(past_up, (1, 2, 0, 3)), 3)
    y_s, st_s = _ffn_sample(h2_s, x1_s, mod_s, w_up_a, w_up_b, w_down_bf, w_conv_p, b_conv_p, past_up,
                            l2g, l2b, n_seq=n_s, n_tok=n_tok)
    conv_s = jnp.transpose(st_s[..., :D_FF], (2, 0, 1, 3)).reshape(1, n_s, CONV_W - 1, 2 * D_FF)

    heads_p = (1, n_p, seq, N_HEADS_B, HEAD_DIM)
    heads_s = (1, n_s, n_tok, N_HEADS_B, HEAD_DIM)
    return (y_p.reshape(n_p, seq, D_MODEL), y_s.reshape(n_s, n_tok, D_MODEL),
            k_p.reshape(heads_p), v_p.reshape(heads_p), conv_p,
            k_s.reshape(heads_s), v_s.reshape(heads_s), va_s.reshape(1, n_s, n_tok, W_A), conv_s)
```

```python
import functools

import numpy as np
import jax
import jax.numpy as jnp
from jax import lax
from jax.experimental import pallas as pl
from jax.experimental.pallas import tpu as pltpu

F32 = jnp.float32
BF16 = jnp.bfloat16

D_MODEL = 2048
HEAD_DIM = 128
N_HEADS_B = 8
N_GROUPS_A = 8
W_B = N_HEADS_B * HEAD_DIM
W_A = N_GROUPS_A * HEAD_DIM
CHUNK = 128
MOBA_BLOCK = 256
MOBA_TOPK = 3
NUM_BUCKETS = 32
MAX_DISTANCE = 1024
CONV_W = 3
D_FF = 5504
PAGE_SIZE = 128
ALPHA = 2.0 ** 0.25
LN_EPS = 1e-5
NEG_INF = -1e30

FF_TILE = 512
D_FF_PAD = ((D_FF + FF_TILE - 1) // FF_TILE) * FF_TILE
N_FF_TILES = D_FF_PAD // FF_TILE
SUBLANES = 8
BF16_ROWS = 16
MXU_COLS = 256
VMEM_LIMIT_BYTES = 56 * 1024 * 1024

_NT = (((1,), (1,)), ((), ()))


def _params(n_axes):
    return pltpu.CompilerParams(dimension_semantics=("arbitrary",) * n_axes,
                                vmem_limit_bytes=VMEM_LIMIT_BYTES)


def _layer_norm(x, g, b):
    mu = jnp.mean(x, axis=-1, keepdims=True)
    xc = x - mu
    var = jnp.mean(xc * xc, axis=-1, keepdims=True)
    return xc * lax.rsqrt(var + LN_EPS) * g + b


def _rms_norm(x, g):
    return x * lax.rsqrt(jnp.mean(x * x, axis=-1, keepdims=True) + LN_EPS) * g


def _split_bf16(x):
    hi = x.astype(BF16)
    lo = (x - hi.astype(F32)).astype(BF16)
    return hi, lo


def _dot(a, b):
    return jnp.dot(a, b, preferred_element_type=F32)


def _dot_nt(a, b):
    return lax.dot_general(a, b, _NT, preferred_element_type=F32)


def _block_scores(q_hi, q_lo, km):
    km_hi, km_lo = _split_bf16(km)
    return _dot_nt(q_hi, km_hi) + _dot_nt(q_hi, km_lo) + _dot_nt(q_lo, km_hi)


def _topk_select(s, n_valid, axis):
    idx = lax.broadcasted_iota(jnp.int32, s.shape, axis)
    rank = jnp.zeros(s.shape, F32)
    for i in range(n_valid):
        si = lax.slice_in_dim(s, i, i + 1, axis=axis)
        beats = (si > s) | ((si == s) & (i < idx))
        rank = rank + beats.astype(F32)
    return ((rank < MOBA_TOPK) & (idx < n_valid)).astype(F32)


def _ada_kernel(c_ref, w_ref, b_ref, o_ref):
    c = c_ref[...]
    a = jax.nn.silu(c).astype(BF16)
    o_ref[...] = _dot(a, w_ref[...].astype(BF16)) + b_ref[...]


def _ada(c_all, w_ada, b_ada):
    rows = c_all.shape[0]
    n_out = w_ada.shape[1]
    tn = 512
    return pl.pallas_call(
        _ada_kernel,
        grid=(n_out // tn,),
        in_specs=[pl.BlockSpec((rows, D_MODEL), lambda j: (0, 0)),
                  pl.BlockSpec((D_MODEL, tn), lambda j: (0, j)),
                  pl.BlockSpec((1, tn), lambda j: (0, j))],
        out_specs=pl.BlockSpec((rows, tn), lambda j: (0, j)),
        out_shape=jax.ShapeDtypeStruct((rows, n_out), F32),
        compiler_params=_params(1),
        name="ada",
    )(c_all, w_ada, b_ada.reshape(1, n_out))


def _proj_prompt_kernel(x_ref, sc_ref, sh_ref, w_ref, lng_ref, lnb_ref, ws_ref, bs_ref, nag_ref,
                        qhi_ref, qlo_ref, k_ref, v_ref, kbf_ref, vbf_ref, km_ref, gm_ref,
                        h_scr, u_scr, gm_scr):
    j = pl.program_id(1)
    tm = x_ref.shape[0]

    @pl.when(j == 0)
    def _():
        h_scr[...] = (x_ref[...] * (1.0 + sc_ref[...]) + sh_ref[...]).astype(BF16)

    acc = _dot(h_scr[...], w_ref[...])

    @pl.when(j == 0)
    def _():
        q_hi, q_lo = _split_bf16(acc * (HEAD_DIM ** -0.5))
        qhi_ref[...] = q_hi
        qlo_ref[...] = q_lo

    @pl.when(j == 1)
    def _():
        k_ref[...] = acc
        kbf_ref[...] = acc.astype(BF16)
        km_ref[...] = jnp.mean(acc.reshape(tm // MOBA_BLOCK, MOBA_BLOCK, W_B), axis=1)

    @pl.when(j == 2)
    def _():
        v_ref[...] = acc
        vbf_ref[...] = acc.astype(BF16)

    @pl.when(j == 3)
    def _():
        u_scr[...] = jax.nn.gelu(acc)

    @pl.when(j == 4)
    def _():
        va = _layer_norm(jax.nn.gelu(acc), lng_ref[...], lnb_ref[...]).astype(BF16)
        n_chunks = tm // CHUNK
        row = lax.broadcasted_iota(jnp.int32, (CHUNK, CHUNK), 0)
        col = lax.broadcasted_iota(jnp.int32, (CHUNK, CHUNK), 1)
        for g in range(N_GROUPS_A):
            gs = slice(g * HEAD_DIM, (g + 1) * HEAD_DIM)
            w_g = jnp.where(row >= col, ws_ref[g], 0.0).astype(BF16)
            v_g = jnp.concatenate([va[c * CHUNK:(c + 1) * CHUNK, gs] for c in range(n_chunks)], axis=1)
            m_g = _dot(w_g, v_g)
            for c in range(n_chunks):
                rs = slice(c * CHUNK, (c + 1) * CHUNK)
                mixed = m_g[:, c * HEAD_DIM:(c + 1) * HEAD_DIM] + bs_ref[g]
                gm_scr[rs, gs] = u_scr[rs, gs] * mixed
        gm_ref[...] = _rms_norm(gm_scr[...], nag_ref[...]).astype(BF16)


def _proj_prompt(x, mod_p, w_in, ln_g, ln_b, w_s, bs_full, norm_a_g, *, tm, seq):
    rows = x.shape[0]
    tiles_per_seq = seq // tm
    n_col = w_in.shape[1] // W_B
    row_blk = lambda i, j: (i, 0)
    vec = lambda i, j: (0, 0)
    out_shape = (
        jax.ShapeDtypeStruct((rows, W_B), BF16),
        jax.ShapeDtypeStruct((rows, W_B), BF16),
        jax.ShapeDtypeStruct((rows, W_B), F32),
        jax.ShapeDtypeStruct((rows, W_B), F32),
        jax.ShapeDtypeStruct((rows, W_B), BF16),
        jax.ShapeDtypeStruct((rows, W_B), BF16),
        jax.ShapeDtypeStruct((rows // tm, tm // MOBA_BLOCK, W_B), F32),
        jax.ShapeDtypeStruct((rows, W_A), BF16),
    )
    out_specs = (
        pl.BlockSpec((tm, W_B), row_blk), pl.BlockSpec((tm, W_B), row_blk),
        pl.BlockSpec((tm, W_B), row_blk), pl.BlockSpec((tm, W_B), row_blk),
        pl.BlockSpec((tm, W_B), row_blk), pl.BlockSpec((tm, W_B), row_blk),
        pl.BlockSpec((None, tm // MOBA_BLOCK, W_B), lambda i, j: (i, 0, 0)),
        pl.BlockSpec((tm, W_A), row_blk),
    )
    return pl.pallas_call(
        _proj_prompt_kernel,
        grid=(rows // tm, n_col),
        in_specs=[pl.BlockSpec((tm, D_MODEL), row_blk),
                  pl.BlockSpec((None, 1, D_MODEL), lambda i, j: (i // tiles_per_seq, 0, 1)),
                  pl.BlockSpec((None, 1, D_MODEL), lambda i, j: (i // tiles_per_seq, 0, 0)),
                  pl.BlockSpec((D_MODEL, W_B), lambda i, j: (0, j)),
                  pl.BlockSpec((1, W_A), vec), pl.BlockSpec((1, W_A), vec),
                  pl.BlockSpec((N_GROUPS_A, CHUNK, CHUNK), lambda i, j: (0, 0, 0)),
                  pl.BlockSpec((N_GROUPS_A, CHUNK, HEAD_DIM), lambda i, j: (0, 0, 0)),
                  pl.BlockSpec((1, W_A), vec)],
        out_specs=out_specs,
        out_shape=out_shape,
        scratch_shapes=[pltpu.VMEM((tm, D_MODEL), BF16), pltpu.VMEM((tm, W_A), F32),
                        pltpu.VMEM((tm, W_A), F32)],
        compiler_params=_params(2),
        name="proj_prompt",
    )(x, mod_p, mod_p, w_in, ln_g, ln_b, w_s, bs_full, norm_a_g)


def _proj_sample_kernel(x_ref, sc_ref, sh_ref, w_ref, lng_ref, lnb_ref,
                        q_ref, k_ref, v_ref, u_ref, va_ref, h_scr):
    j = pl.program_id(1)

    @pl.when(j == 0)
    def _():
        h_scr[...] = (x_ref[...] * (1.0 + sc_ref[...]) + sh_ref[...]).astype(BF16)

    acc = _dot(h_scr[...], w_ref[...])

    @pl.when(j == 0)
    def _():
        q_ref[...] = acc * (HEAD_DIM ** -0.5)

    @pl.when(j == 1)
    def _():
        k_ref[...] = acc

    @pl.when(j == 2)
    def _():
        v_ref[...] = acc

    @pl.when(j == 3)
    def _():
        u_ref[...] = jax.nn.gelu(acc)

    @pl.when(j == 4)
    def _():
        va_ref[...] = _layer_norm(jax.nn.gelu(acc), lng_ref[...], lnb_ref[...])


def _proj_sample(x2d, mod_s, w_in, ln_g, ln_b, *, n_seq, n_tok):
    n_col = w_in.shape[1] // W_B
    tok_blk = lambda t, j: (0, t)
    vec = lambda t, j: (0, 0)
    out = jax.ShapeDtypeStruct((n_seq, n_tok * W_B), F32)
    return pl.pallas_call(
        _proj_sample_kernel,
        grid=(n_tok, n_col),
        in_specs=[pl.BlockSpec((n_seq, D_MODEL), tok_blk),
                  pl.BlockSpec((n_seq, D_MODEL), lambda t, j: (0, 1)),
                  pl.BlockSpec((n_seq, D_MODEL), lambda t, j: (0, 0)),
                  pl.BlockSpec((D_MODEL, W_B), lambda t, j: (0, j)),
                  pl.BlockSpec((1, W_A), vec), pl.BlockSpec((1, W_A), vec)],
        out_specs=tuple(pl.BlockSpec((n_seq, W_B), tok_blk) for _ in range(5)),
        out_shape=(out,) * 5,
        scratch_shapes=[pltpu.VMEM((n_seq, D_MODEL), BF16)],
        compiler_params=_params(2),
        name="proj_sample",
    )(x2d, mod_s, mod_s, w_in, ln_g, ln_b)


def _attn_prompt_kernel(qhi_ref, qlo_ref, k_ref, v_ref, km_ref, rev_ref, o_ref, bias_scr):
    n_blocks = k_ref.shape[0] // MOBA_BLOCK

    @pl.when(pl.program_id(1) == 0)
    def _():
        for d in range(n_blocks):
            r = jnp.broadcast_to(rev_ref[d:d + 1, :], (MOBA_BLOCK, 2 * MOBA_BLOCK))
            rolled = pltpu.roll(r, 0, 1, stride=1, stride_axis=0)
            bias_scr[d] = rolled[:, MOBA_BLOCK:]

    km_hi, km_lo = _split_bf16(km_ref[...])
    row = lax.broadcasted_iota(jnp.int32, (MOBA_BLOCK, MOBA_BLOCK), 0)
    col = lax.broadcasted_iota(jnp.int32, (MOBA_BLOCK, MOBA_BLOCK), 1)

    def rows_of(qb):
        return slice(qb * MOBA_BLOCK, (qb + 1) * MOBA_BLOCK)

    def qk(qb):
        return _dot_nt(qhi_ref[rows_of(qb), :], k_ref[0:(qb + 1) * MOBA_BLOCK, :])

    def softmax(qb, logits):
        sel = None
        if qb > MOBA_TOPK:
            q_hi, q_lo = qhi_ref[rows_of(qb), :], qlo_ref[rows_of(qb), :]
            s_t = _dot_nt(km_hi, q_hi) + _dot_nt(km_lo, q_hi) + _dot_nt(km_hi, q_lo)
            sel = _topk_select(s_t, qb, axis=0).T
        pieces = []
        for jb in range(qb + 1):
            lj = logits[:, jb * MOBA_BLOCK:(jb + 1) * MOBA_BLOCK] + bias_scr[qb - jb]
            if jb == qb:
                lj = jnp.where(row >= col, lj, NEG_INF)
            elif sel is not None:
                lj = jnp.where(sel[:, jb:jb + 1] > 0.0, lj, NEG_INF)
            pieces.append(lj)
        lg = jnp.concatenate(pieces, axis=1) if len(pieces) > 1 else pieces[0]
        m = jnp.max(lg, axis=-1, keepdims=True)
        p = jnp.exp(lg - m)
        return p.astype(BF16), jnp.sum(p, axis=-1, keepdims=True)

    def pv(qb, p, z):
        o_ref[rows_of(qb), :] = _dot(p, v_ref[0:(qb + 1) * MOBA_BLOCK, :]) / z

    logits = qk(0)
    prev = None
    for qb in range(n_blocks):
        nxt = qk(qb + 1) if qb + 1 < n_blocks else None
        p, z = softmax(qb, logits)
        if prev is not None:
            pv(*prev)
        prev = (qb, p, z)
        logits = nxt
    pv(*prev)


def _attn_prompt(q_hi, q_lo, k_bf, v_bf, k_mean, rev_tab, *, n_seq, seq):
    n_blocks = seq // MOBA_BLOCK
    qkv = pl.BlockSpec((seq, HEAD_DIM), lambda h, n: (n, h))
    return pl.pallas_call(
        _attn_prompt_kernel,
        grid=(N_HEADS_B, n_seq),
        in_specs=[qkv, qkv, qkv, qkv,
                  pl.BlockSpec((None, n_blocks, HEAD_DIM), lambda h, n: (n, 0, h)),
                  pl.BlockSpec((None, n_blocks, 2 * MOBA_BLOCK), lambda h, n: (h, 0, 0))],
        out_specs=pl.BlockSpec((seq, HEAD_DIM), lambda h, n: (n, h)),
        out_shape=jax.ShapeDtypeStruct((n_seq * seq, W_B), F32),
        scratch_shapes=[pltpu.VMEM((n_blocks, MOBA_BLOCK, MOBA_BLOCK), F32)],
        compiler_params=_params(2),
        name="attn_prompt",
    )(q_hi, q_lo, k_bf, v_bf, k_mean, rev_tab)


def _attn_sample_kernel(n_pages, pt_ref, q_ref, kn_ref, vn_ref, bias_ref, biasn_ref, *refs):
    k_pages = refs[:n_pages]
    v_pages = refs[n_pages:2 * n_pages]
    o_ref = refs[2 * n_pages]
    n_tok = q_ref.shape[0]
    rows = n_tok * N_HEADS_B
    cols = PAGE_SIZE * N_HEADS_B
    pages_per_block = MOBA_BLOCK // PAGE_SIZE
    n_blocks = n_pages // pages_per_block

    q = q_ref[...].reshape(rows, HEAD_DIM)
    q_bf = q.astype(BF16)

    logit_pages = []
    k_sums = []
    for p in range(n_pages):
        kp = k_pages[p][...]
        k_sums.append(jnp.sum(kp, axis=0))
        logit_pages.append(_dot_nt(q_bf, kp.reshape(cols, HEAD_DIM).astype(BF16)))

    blk_lane = lax.broadcasted_iota(jnp.int32, (rows, n_blocks), 1)
    scores = jnp.zeros((rows, n_blocks), F32)
    for b in range(n_blocks):
        km = sum(k_sums[b * pages_per_block:(b + 1) * pages_per_block]) * (1.0 / MOBA_BLOCK)
        s_b = jnp.sum(q * jnp.concatenate([km] * n_tok, axis=0), axis=1, keepdims=True)
        scores = jnp.where(blk_lane == b, s_b, scores)
    sel = _topk_select(scores, n_blocks, axis=1)

    masked = []
    for p in range(n_pages):
        b = p // pages_per_block
        lp = logit_pages[p] + bias_ref[:, p * cols:(p + 1) * cols]
        masked.append(jnp.where(sel[:, b:b + 1] > 0.0, lp, NEG_INF))
    ln = _dot_nt(q_bf, kn_ref[...].reshape(rows, HEAD_DIM).astype(BF16)) + biasn_ref[...]

    m = jnp.maximum(jnp.max(functools.reduce(jnp.maximum, masked), axis=-1, keepdims=True),
                    jnp.max(ln, axis=-1, keepdims=True))
    p_new = jnp.exp(ln - m)
    acc = _dot(p_new.astype(BF16), vn_ref[...].reshape(rows, HEAD_DIM).astype(BF16))
    p_sum = None
    for p in range(n_pages):
        pp = jnp.exp(masked[p] - m)
        p_sum = pp if p_sum is None else p_sum + pp
        acc = acc + _dot(pp.astype(BF16), v_pages[p][...].reshape(cols, HEAD_DIM).astype(BF16))
    z = jnp.sum(p_sum, axis=-1, keepdims=True) + jnp.sum(p_new, axis=-1, keepdims=True)
    o_ref[...] = (acc / z).reshape(n_tok, N_HEADS_B, HEAD_DIM)


def _attn_sample(page_table, q, k_new, v_new, bias_past, bias_new, cache_k, cache_v):
    n_seq, n_pages = page_table.shape
    n_tok = q.shape[1]
    tok = pl.BlockSpec((None, n_tok, N_HEADS_B, HEAD_DIM), lambda n, pt: (n, 0, 0, 0))

    def page_spec(p):
        return pl.BlockSpec((None, None, PAGE_SIZE, N_HEADS_B, HEAD_DIM),
                            lambda n, pt: (0, pt[n * n_pages + p], 0, 0, 0))

    grid_spec = pltpu.PrefetchScalarGridSpec(
        num_scalar_prefetch=1,
        grid=(n_seq,),
        in_specs=[tok, tok, tok,
                  pl.BlockSpec(bias_past.shape, lambda n, pt: (0, 0)),
                  pl.BlockSpec(bias_new.shape, lambda n, pt: (0, 0))]
                 + [page_spec(p) for p in range(n_pages)] * 2,
        out_specs=tok,
    )
    return pl.pallas_call(
        functools.partial(_attn_sample_kernel, n_pages),
        grid_spec=grid_spec,
        out_shape=jax.ShapeDtypeStruct((n_seq, n_tok, N_HEADS_B, HEAD_DIM), F32),
        compiler_params=_params(1),
        name="attn_sample",
    )(page_table.reshape(-1), q, k_new, v_new, bias_past, bias_new,
      *([cache_k] * n_pages), *([cache_v] * n_pages))


def _mix_tail(x, gm_n, attn, g1, sc2, sh2, wo_ref, nbg_ref, l1g_ref, l1b_ref, x1_ref, h2_ref):
    a_n = _rms_norm(attn, nbg_ref[...]).astype(BF16)
    o = _dot(gm_n, wo_ref[0:W_A, :]) + _dot(a_n, wo_ref[W_A:W_A + W_B, :])
    x1 = _layer_norm(ALPHA * x + g1 * o, l1g_ref[...], l1b_ref[...])
    x1_ref[...] = x1
    h2_ref[...] = (x1 * (1.0 + sc2) + sh2).astype(BF16)


def _mix_prompt_kernel(x_ref, gm_ref, at_ref, g1_ref, sc2_ref, sh2_ref, wo_ref, nbg_ref, l1g_ref, l1b_ref,
                       x1_ref, h2_ref):
    _mix_tail(x_ref[...], gm_ref[...], at_ref[...], g1_ref[...], sc2_ref[...], sh2_ref[...],
              wo_ref, nbg_ref, l1g_ref, l1b_ref, x1_ref, h2_ref)


def _mix_prompt(x, gm_n, attn, mod_p, w_out, norm_b_g, ln1_g, ln1_b, *, tm, seq):
    rows = x.shape[0]
    tiles_per_seq = seq // tm
    row_blk = lambda i: (i, 0)
    vec = lambda i: (0, 0)
    mod = lambda c: pl.BlockSpec((None, 1, D_MODEL), lambda i: (i // tiles_per_seq, 0, c))
    return pl.pallas_call(
        _mix_prompt_kernel,
        grid=(rows // tm,),
        in_specs=[pl.BlockSpec((tm, D_MODEL), row_blk), pl.BlockSpec((tm, W_A), row_blk),
                  pl.BlockSpec((tm, W_B), row_blk), mod(2), mod(4), mod(3),
                  pl.BlockSpec((D_MODEL, D_MODEL), vec), pl.BlockSpec((1, W_B), vec),
                  pl.BlockSpec((1, D_MODEL), vec), pl.BlockSpec((1, D_MODEL), vec)],
        out_specs=(pl.BlockSpec((tm, D_MODEL), row_blk), pl.BlockSpec((tm, D_MODEL), row_blk)),
        out_shape=(jax.ShapeDtypeStruct((rows, D_MODEL), F32), jax.ShapeDtypeStruct((rows, D_MODEL), BF16)),
        compiler_params=_params(1),
        name="mix_prompt",
    )(x, gm_n, attn, mod_p, mod_p, mod_p, w_out, norm_b_g, ln1_g, ln1_b)


def _mix_sample_kernel(x_ref, u_ref, va_ref, ws_ref, bs_ref, nag_ref, at_ref, g1_ref, sc2_ref, sh2_ref,
                       wo_ref, nbg_ref, l1g_ref, l1b_ref, x1_ref, h2_ref):
    n_tok = ws_ref.shape[0]
    mixed = bs_ref[...]
    for s in range(n_tok):
        mixed = mixed + ws_ref[s:s + 1, :] * va_ref[:, s * W_A:(s + 1) * W_A]
    gm_n = _rms_norm(u_ref[...] * mixed, nag_ref[...]).astype(BF16)
    _mix_tail(x_ref[...], gm_n, at_ref[...], g1_ref[...], sc2_ref[...], sh2_ref[...],
              wo_ref, nbg_ref, l1g_ref, l1b_ref, x1_ref, h2_ref)


def _mix_sample(x2d, u, va, ws_tok, bs_tok, norm_a_g, attn2d, mod_s, w_out, norm_b_g, ln1_g, ln1_b,
                *, n_seq, n_tok):
    tok_blk = lambda t: (0, t)
    vec = lambda t: (0, 0)
    mod = lambda c: pl.BlockSpec((n_seq, D_MODEL), lambda t: (0, c))
    out_blk = pl.BlockSpec((n_seq, D_MODEL), lambda t: (t, 0))
    return pl.pallas_call(
        _mix_sample_kernel,
        grid=(n_tok,),
        in_specs=[pl.BlockSpec((n_seq, D_MODEL), tok_blk), pl.BlockSpec((n_seq, W_A), tok_blk),
                  pl.BlockSpec((n_seq, n_tok * W_A), vec),
                  pl.BlockSpec((None, n_tok, W_A), lambda t: (t, 0, 0)),
                  pl.BlockSpec((None, 1, W_A), lambda t: (t, 0, 0)),
                  pl.BlockSpec((1, W_A), vec),
                  pl.BlockSpec((n_seq, W_B), tok_blk), mod(2), mod(4), mod(3),
                  pl.BlockSpec((D_MODEL, D_MODEL), vec), pl.BlockSpec((1, W_B), vec),
                  pl.BlockSpec((1, D_MODEL), vec), pl.BlockSpec((1, D_MODEL), vec)],
        out_specs=(out_blk, out_blk),
        out_shape=(jax.ShapeDtypeStruct((n_tok * n_seq, D_MODEL), F32),
                   jax.ShapeDtypeStruct((n_tok * n_seq, D_MODEL), BF16)),
        compiler_params=_params(1),
        name="mix_sample",
    )(x2d, u, va, ws_tok, bs_tok, norm_a_g, attn2d, mod_s, mod_s, mod_s, w_out, norm_b_g, ln1_g, ln1_b)


def _ffn_act(conv_a, conv_b):
    return (jax.nn.silu(conv_a) * conv_b).astype(BF16)


def _ffn_prompt_kernel(tiles_per_seq, h_ref, halo_ref, x1_ref, g2_ref, wa_ref, wb_ref, wd_ref, wc_ref, bc_ref,
                       l2g_ref, l2b_ref, y_ref, st_ref, acc_scr, h_scr):
    i = pl.program_id(0)
    j = pl.program_id(1)
    tm = h_ref.shape[0]
    n_halo = halo_ref.shape[0]

    @pl.when(j == 0)
    def _():
        seq_start = (i % tiles_per_seq) == 0
        h_scr[0:n_halo, :] = jnp.where(seq_start, jnp.zeros_like(halo_ref[...]), halo_ref[...])
        h_scr[n_halo:n_halo + tm, :] = h_ref[...]
        acc_scr[...] = jnp.zeros_like(acc_scr)

    h = h_scr[...]

    def conv(half, up, cs):
        st_ref[half, :, cs] = up[n_halo + tm - SUBLANES:n_halo + tm, :]
        out = bc_ref[half:half + 1, cs]
        for c in range(CONV_W):
            lag = CONV_W - 1 - c
            out = out + up[n_halo - lag:n_halo - lag + tm, :] * wc_ref[half, c:c + 1, cs]
        return out

    chunks = [slice(c0, c0 + MXU_COLS) for c0 in range(0, FF_TILE, MXU_COLS)]
    ups = [(_dot(h, wa_ref[:, cs]), _dot(h, wb_ref[:, cs])) for cs in chunks]
    acts = [_ffn_act(conv(0, up_a, cs), conv(1, up_b, cs)) for cs, (up_a, up_b) in zip(chunks, ups)]
    acc_scr[...] += _dot(jnp.concatenate(acts, axis=1), wd_ref[...])

    @pl.when(j == pl.num_programs(1) - 1)
    def _():
        y_ref[...] = _layer_norm(ALPHA * x1_ref[...] + g2_ref[...] * acc_scr[...], l2g_ref[...], l2b_ref[...])


def _ffn_prompt(h2, x1, mod_p, w_up, w_down, w_conv, b_conv, ln2_g, ln2_b, *, tm, seq):
    rows = h2.shape[0]
    tiles_per_seq = seq // tm
    row_blk = lambda i, j: (i, 0)
    vec = lambda i, j: (0, 0)
    halo_blk = lambda i, j: (jnp.maximum(i * (tm // BF16_ROWS) - 1, 0), 0)
    return pl.pallas_call(
        functools.partial(_ffn_prompt_kernel, tiles_per_seq),
        grid=(rows // tm, N_FF_TILES),
        in_specs=[pl.BlockSpec((tm, D_MODEL), row_blk), pl.BlockSpec((BF16_ROWS, D_MODEL), halo_blk),
                  pl.BlockSpec((tm, D_MODEL), row_blk),
                  pl.BlockSpec((None, 1, D_MODEL), lambda i, j: (i // tiles_per_seq, 0, 5)),
                  pl.BlockSpec((D_MODEL, FF_TILE), lambda i, j: (0, j)),
                  pl.BlockSpec((D_MODEL, FF_TILE), lambda i, j: (0, N_FF_TILES + j)),
                  pl.BlockSpec((FF_TILE, D_MODEL), lambda i, j: (j, 0)),
                  pl.BlockSpec((2, CONV_W, FF_TILE), lambda i, j: (0, 0, j)),
                  pl.BlockSpec((2, FF_TILE), lambda i, j: (0, j)),
                  pl.BlockSpec((1, D_MODEL), vec), pl.BlockSpec((1, D_MODEL), vec)],
        out_specs=(pl.BlockSpec((tm, D_MODEL), row_blk),
                   pl.BlockSpec((None, 2, SUBLANES, FF_TILE), lambda i, j: (i, 0, 0, j))),
        out_shape=(jax.ShapeDtypeStruct((rows, D_MODEL), F32),
                   jax.ShapeDtypeStruct((rows // tm, 2, SUBLANES, D_FF_PAD), F32)),
        scratch_shapes=[pltpu.VMEM((tm, D_MODEL), F32), pltpu.VMEM((BF16_ROWS + tm, D_MODEL), BF16)],
        compiler_params=_params(2),
        name="ffn_prompt",
    )(h2, h2, x1, mod_p, w_up, w_up, w_down, w_conv, b_conv, ln2_g, ln2_b)


def _ffn_sample_kernel(n_tok, h_ref, x1_ref, g2_ref, wa_ref, wb_ref, wd_ref, wc_ref, bc_ref, past_ref,
                       l2g_ref, l2b_ref, y_ref, st_ref, acc_scr):
    j = pl.program_id(0)
    n_seq = g2_ref.shape[0]
    h = h_ref[...]

    def conv(half, w_ref):
        up = _dot(h, w_ref[...])
        full = [past_ref[r, half] for r in range(CONV_W - 1)]
        full += [up[t * n_seq:(t + 1) * n_seq, :] for t in range(n_tok)]
        for r in range(CONV_W - 1):
            st_ref[r, half] = full[n_tok + r]
        w = wc_ref[half]
        b = bc_ref[half:half + 1, :]
        return jnp.concatenate(
            [b + sum(full[t + c] * w[c:c + 1, :] for c in range(CONV_W)) for t in range(n_tok)], axis=0)

    part = _dot(_ffn_act(conv(0, wa_ref), conv(1, wb_ref)), wd_ref[...])

    @pl.when(j == 0)
    def _():
        acc_scr[...] = part

    @pl.when(j > 0)
    def _():
        acc_scr[...] += part

    @pl.when(j == pl.num_programs(0) - 1)
    def _():
        g2 = g2_ref[...]
        for t in range(n_tok):
            rs = slice(t * n_seq, (t + 1) * n_seq)
            y_ref[:, t * D_MODEL:(t + 1) * D_MODEL] = _layer_norm(
                ALPHA * x1_ref[rs, :] + g2 * acc_scr[rs, :], l2g_ref[...], l2b_ref[...])


def _ffn_sample(h2, x1, mod_s, w_up, w_down, w_conv, b_conv, past, ln2_g, ln2_b, *, n_seq, n_tok):
    rows = n_seq * n_tok
    full = lambda j: (0, 0)
    state_blk = pl.BlockSpec((CONV_W - 1, 2, n_seq, FF_TILE), lambda j: (0, 0, 0, j))
    return pl.pallas_call(
        functools.partial(_ffn_sample_kernel, n_tok),
        grid=(N_FF_TILES,),
        in_specs=[pl.BlockSpec((rows, D_MODEL), full), pl.BlockSpec((rows, D_MODEL), full),
                  pl.BlockSpec((n_seq, D_MODEL), lambda j: (0, 5)),
                  pl.BlockSpec((D_MODEL, FF_TILE), lambda j: (0, j)),
                  pl.BlockSpec((D_MODEL, FF_TILE), lambda j: (0, N_FF_TILES + j)),
                  pl.BlockSpec((FF_TILE, D_MODEL), lambda j: (j, 0)),
                  pl.BlockSpec((2, CONV_W, FF_TILE), lambda j: (0, 0, j)),
                  pl.BlockSpec((2, FF_TILE), lambda j: (0, j)),
                  state_blk,
                  pl.BlockSpec((1, D_MODEL), full), pl.BlockSpec((1, D_MODEL), full)],
        out_specs=(pl.BlockSpec((n_seq, n_tok * D_MODEL), full), state_blk),
        out_shape=(jax.ShapeDtypeStruct((n_seq, n_tok * D_MODEL), F32),
                   jax.ShapeDtypeStruct((CONV_W - 1, 2, n_seq, D_FF_PAD), F32)),
        scratch_shapes=[pltpu.VMEM((rows, D_MODEL), F32)],
        compiler_params=_params(1),
        name="ffn_sample",
    )(h2, x1, mod_s, w_up, w_up, w_down, w_conv, b_conv, past, ln2_g, ln2_b)


def _t5_bucket_table(n_rel):
    max_exact = NUM_BUCKETS // 2
    ratio = MAX_DISTANCE // max_exact
    log_ratio = ratio.bit_length() - 1
    assert 1 << log_ratio == ratio
    steps = NUM_BUCKETS - max_exact
    out = np.zeros((n_rel,), np.int32)
    for n in range(n_rel):
        if n < max_exact:
            out[n] = n
            continue
        k = 0
        while k + 1 < steps and n ** steps >= (max_exact ** steps) << (log_ratio * (k + 1)):
            k += 1
        out[n] = min(max_exact + k, NUM_BUCKETS - 1)
    return out


def _bias_tables(rel_bias, seq, past, n_tok):
    n_rel = max(seq, past + n_tok)
    onehot = np.zeros((NUM_BUCKETS, n_rel), np.float32)
    onehot[_t5_bucket_table(n_rel), np.arange(n_rel)] = 1.0
    tab = jnp.dot(rel_bias.astype(F32).T, onehot, precision=lax.Precision.HIGHEST)
    n_blocks = seq // MOBA_BLOCK
    tab_pad = jnp.pad(tab, ((0, 0), (MOBA_BLOCK, 1)))
    rev = jnp.stack([tab_pad[:, d * MOBA_BLOCK + 1:(d + 2) * MOBA_BLOCK + 1][:, ::-1]
                     for d in range(n_blocks)], axis=1)
    same_head = np.eye(N_HEADS_B, dtype=bool)
    by_key = jnp.stack([tab[:, t + 1:t + 1 + past][:, ::-1] for t in range(n_tok)], axis=0)
    n_pages = past // PAGE_SIZE
    spread = np.repeat(np.eye(PAGE_SIZE, dtype=np.float32), N_HEADS_B, axis=1)
    by_col = jnp.einsum('rpk,kc->rpc', by_key.reshape(n_tok * N_HEADS_B, n_pages, PAGE_SIZE), spread,
                        precision=lax.Precision.HIGHEST).reshape(n_tok * N_HEADS_B, past * N_HEADS_B)
    row_head = np.tile(np.arange(N_HEADS_B), n_tok)[:, None]
    col_head = np.tile(np.arange(N_HEADS_B), past)[None, :]
    bias_past = jnp.where(row_head == col_head, by_col, NEG_INF)
    new_rows = []
    for t in range(n_tok):
        by_tok = jnp.stack([tab[:, max(t - s, 0)] for s in range(n_tok)], axis=1)
        ok = same_head[:, None, :] & (np.arange(n_tok) <= t)[None, :, None]
        new_rows.append(jnp.where(ok, by_tok[:, :, None], NEG_INF).reshape(N_HEADS_B, n_tok * N_HEADS_B))
    bias_new = jnp.stack(new_rows, axis=0).reshape(n_tok * N_HEADS_B, n_tok * N_HEADS_B)
    return rev, bias_past, bias_new


def _pad_ff(a, axis):
    pad = [(0, 0)] * a.ndim
    pad[axis] = (0, D_FF_PAD - D_FF)
    return jnp.pad(a, pad)


def kernel(x_prompt, x_sample, cache_k, cache_v, state_ffn_conv, page_table, c_prompt, c_sample,
           w_ada, b_ada, w_in, ln_sgu_g, ln_sgu_b, w_s, b_s, rel_bias, norm_a_g, norm_b_g, w_out,
           ln1_g, ln1_b, w_up, w_conv, b_conv, w_down, ln2_g, ln2_b):
    n_p, seq, _ = x_prompt.shape
    n_s, n_tok, _ = x_sample.shape
    n_pages = page_table.shape[1]
    past = n_pages * PAGE_SIZE
    assert w_ada.shape[0] == 1 and seq % MOBA_BLOCK == 0 and past % MOBA_BLOCK == 0 and n_tok <= CHUNK

    w_in_bf = w_in[0].astype(BF16)
    w_out_bf = w_out[0].astype(BF16)
    ff_pad = D_FF_PAD - D_FF
    w_up_bf = jnp.concatenate([w_up[0][:, :D_FF].astype(BF16), jnp.zeros((D_MODEL, ff_pad), BF16),
                               w_up[0][:, D_FF:].astype(BF16), jnp.zeros((D_MODEL, ff_pad), BF16)], axis=1)
    w_down_bf = jnp.concatenate([w_down[0].astype(BF16), jnp.zeros((ff_pad, D_MODEL), BF16)], axis=0)
    w_conv_p = _pad_ff(jnp.transpose(w_conv[0].reshape(CONV_W, 2, D_FF), (1, 0, 2)), 2)
    b_conv_p = _pad_ff(b_conv[0].reshape(2, D_FF), 1)
    row = lambda a: a.reshape(1, -1)
    ln_g, ln_b = row(ln_sgu_g[0]), row(ln_sgu_b[0])
    nag, nbg = row(norm_a_g[0]), row(norm_b_g[0])
    l1g, l1b, l2g, l2b = row(ln1_g[0]), row(ln1_b[0]), row(ln2_g[0]), row(ln2_b[0])
    bs_full = jnp.broadcast_to(b_s[0][:, :, None], (N_GROUPS_A, CHUNK, HEAD_DIM))
    tril_tok = np.tril(np.ones((n_tok, n_tok), np.float32))
    ws_tok = jnp.repeat(jnp.transpose(w_s[0][:, :n_tok, :n_tok] * tril_tok, (1, 2, 0)), HEAD_DIM, axis=2)
    bs_tok = jnp.repeat(b_s[0][:, :n_tok].T, HEAD_DIM, axis=1).reshape(n_tok, 1, W_A)
    rev_tab, bias_past, bias_new = _bias_tables(rel_bias, seq, past, n_tok)

    pad_rows = (-(n_s + n_p)) % SUBLANES
    c_all = jnp.concatenate([c_sample, c_prompt, jnp.zeros((pad_rows, D_MODEL), F32)], axis=0)
    mod = _ada(c_all, w_ada[0], b_ada[0])
    mod_s = mod[:n_s]
    mod_p = mod[n_s:n_s + n_p].reshape(n_p, 1, 6 * D_MODEL)

    xp = x_prompt.reshape(n_p * seq, D_MODEL)
    q_hi, q_lo, k_p, v_p, k_bf, v_bf, k_mean, gm_n = _proj_prompt(
        xp, mod_p, w_in_bf, ln_g, ln_b, w_s[0], bs_full, nag, tm=512, seq=seq)
    attn_p = _attn_prompt(q_hi, q_lo, k_bf, v_bf, k_mean.reshape(n_p, seq // MOBA_BLOCK, W_B), rev_tab,
                          n_seq=n_p, seq=seq)
    x1_p, h2_p = _mix_prompt(xp, gm_n, attn_p, mod_p, w_out_bf, nbg, l1g, l1b, tm=512, seq=seq)
    y_p, st_p = _ffn_prompt(h2_p, x1_p, mod_p, w_up_bf, w_down_bf, w_conv_p, b_conv_p, l2g, l2b,
                            tm=512, seq=seq)
    tiles_per_seq = st_p.shape[0] // n_p
    conv_p = st_p[tiles_per_seq - 1::tiles_per_seq, :, SUBLANES - (CONV_W - 1):, :D_FF]
    conv_p = jnp.transpose(conv_p, (0, 2, 1, 3)).reshape(1, n_p, CONV_W - 1, 2 * D_FF)

    xs = x_sample.reshape(n_s, n_tok * D_MODEL)
    q_s, k_s, v_s, u_s, va_s = _proj_sample(xs, mod_s, w_in_bf, ln_g, ln_b, n_seq=n_s, n_tok=n_tok)
    heads = lambda a: a.reshape(n_s, n_tok, N_HEADS_B, HEAD_DIM)
    attn_s = _attn_sample(page_table, heads(q_s), heads(k_s), heads(v_s), bias_past, bias_new, cache_k, cache_v)
    x1_s, h2_s = _mix_sample(xs, u_s, va_s, ws_tok, bs_tok, nag, attn_s.reshape(n_s, n_tok * W_B), mod_s,
                             w_out_bf, nbg, l1g, l1b, n_seq=n_s, n_tok=n_tok)
    past_up = state_ffn_conv[0].reshape(n_s, CONV_W - 1, 2, D_FF)
    past_up = _pad_ff(jnp.transpose(past_up, (1, 2, 0, 3)), 3)
    y_s, st_s = _ffn_sample(h2_s, x1_s, mod_s, w_up_bf, w_down_bf, w_conv_p, b_conv_p, past_up,
                            l2g, l2b, n_seq=n_s, n_tok=n_tok)
    conv_s = jnp.transpose(st_s[..., :D_FF], (2, 0, 1, 3)).reshape(1, n_s, CONV_W - 1, 2 * D_FF)

    heads_p = (1, n_p, seq, N_HEADS_B, HEAD_DIM)
    heads_s = (1, n_s, n_tok, N_HEADS_B, HEAD_DIM)
    return (y_p.reshape(n_p, seq, D_MODEL), y_s.reshape(n_s, n_tok, D_MODEL),
            k_p.reshape(heads_p), v_p.reshape(heads_p), conv_p,
            k_s.reshape(heads_s), v_s.reshape(heads_s), va_s.reshape(1, n_s, n_tok, W_A), conv_s)
```

```python
import functools

import numpy as np
import jax
import jax.numpy as jnp
from jax import lax
from jax.experimental import pallas as pl
from jax.experimental.pallas import tpu as pltpu

F32 = jnp.float32
BF16 = jnp.bfloat16

D_MODEL = 2048
HEAD_DIM = 128
N_HEADS_B = 8
N_GROUPS_A = 8
W_B = N_HEADS_B * HEAD_DIM
W_A = N_GROUPS_A * HEAD_DIM
CHUNK = 128
MOBA_BLOCK = 256
MOBA_TOPK = 3
NUM_BUCKETS = 32
MAX_DISTANCE = 1024
CONV_W = 3
D_FF = 5504
PAGE_SIZE = 128
ALPHA = 2.0 ** 0.25
LN_EPS = 1e-5
NEG_INF = -1e30

FF_TILE = 512
D_FF_PAD = ((D_FF + FF_TILE - 1) // FF_TILE) * FF_TILE
N_FF_TILES = D_FF_PAD // FF_TILE
SUBLANES = 8
BF16_ROWS = 16
MXU_COLS = 256
VMEM_LIMIT_BYTES = 56 * 1024 * 1024

_NT = (((1,), (1,)), ((), ()))


def _params(n_axes, flags=None):
    return pltpu.CompilerParams(dimension_semantics=("arbitrary",) * n_axes,
                                vmem_limit_bytes=VMEM_LIMIT_BYTES, flags=flags)


def _layer_norm(x, g, b):
    mu = jnp.mean(x, axis=-1, keepdims=True)
    xc = x - mu
    var = jnp.mean(xc * xc, axis=-1, keepdims=True)
    return xc * lax.rsqrt(var + LN_EPS) * g + b


def _rms_norm(x, g):
    return x * lax.rsqrt(jnp.mean(x * x, axis=-1, keepdims=True) + LN_EPS) * g


def _split_bf16(x):
    hi = x.astype(BF16)
    lo = (x - hi.astype(F32)).astype(BF16)
    return hi, lo


def _dot(a, b):
    return jnp.dot(a, b, preferred_element_type=F32)


def _dot_nt(a, b):
    return lax.dot_general(a, b, _NT, preferred_element_type=F32)


def _block_scores(q_hi, q_lo, km):
    km_hi, km_lo = _split_bf16(km)
    return _dot_nt(q_hi, km_hi) + _dot_nt(q_hi, km_lo) + _dot_nt(q_lo, km_hi)


def _topk_select(s, n_valid, axis):
    idx = lax.broadcasted_iota(jnp.int32, s.shape, axis)
    rank = jnp.zeros(s.shape, F32)
    for i in range(n_valid):
        si = lax.slice_in_dim(s, i, i + 1, axis=axis)
        beats = (si > s) | ((si == s) & (i < idx))
        rank = rank + beats.astype(F32)
    return ((rank < MOBA_TOPK) & (idx < n_valid)).astype(F32)


def _cast_pad_kernel(n_valid, axis, *refs):
    *in_refs, o_ref = refs
    x = in_refs[0][...] if len(in_refs) == 1 else jnp.concatenate([r[...] for r in in_refs], axis=1)
    pos = pl.program_id(0) * o_ref.shape[axis] + lax.broadcasted_iota(jnp.int32, o_ref.shape, axis)
    o_ref[...] = jnp.where(pos < n_valid, x, 0.0).astype(BF16)


def _cast_pad_cols(w, col0, n_valid, n_out, tile):
    rows, n_cols = w.shape
    lanes = 128
    assert col0 % lanes == 0 and n_cols % lanes == 0 and tile % lanes == 0 and n_out % tile == 0
    last = n_cols // lanes - 1
    pieces = tile // lanes

    def piece(q):
        return pl.BlockSpec((rows, lanes), lambda j: (0, jnp.minimum(col0 // lanes + j * pieces + q, last)))

    return pl.pallas_call(
        functools.partial(_cast_pad_kernel, n_valid, 1),
        grid=(n_out // tile,),
        in_specs=[piece(q) for q in range(pieces)],
        out_specs=pl.BlockSpec((rows, tile), lambda j: (0, j)),
        out_shape=jax.ShapeDtypeStruct((rows, n_out), BF16),
        compiler_params=_params(1),
        name="cast_pad_cols",
    )(*([w] * pieces))


def _cast_pad_rows(w, n_out, tile):
    n_rows, cols = w.shape
    assert n_out % tile == 0 and (n_out - tile) < n_rows
    return pl.pallas_call(
        functools.partial(_cast_pad_kernel, n_rows, 0),
        grid=(n_out // tile,),
        in_specs=[pl.BlockSpec((tile, cols), lambda j: (j, 0))],
        out_specs=pl.BlockSpec((tile, cols), lambda j: (j, 0)),
        out_shape=jax.ShapeDtypeStruct((n_out, cols), BF16),
        compiler_params=_params(1),
        name="cast_pad_rows",
    )(w)


def _ada_kernel(c_ref, w_ref, b_ref, o_ref):
    c = c_ref[...]
    a = jax.nn.silu(c).astype(BF16)
    o_ref[...] = _dot(a, w_ref[...].astype(BF16)) + b_ref[...]


def _ada(c_all, w_ada, b_ada):
    rows = c_all.shape[0]
    n_out = w_ada.shape[1]
    tn = 512
    return pl.pallas_call(
        _ada_kernel,
        grid=(n_out // tn,),
        in_specs=[pl.BlockSpec((rows, D_MODEL), lambda j: (0, 0)),
                  pl.BlockSpec((D_MODEL, tn), lambda j: (0, j)),
                  pl.BlockSpec((1, tn), lambda j: (0, j))],
        out_specs=pl.BlockSpec((rows, tn), lambda j: (0, j)),
        out_shape=jax.ShapeDtypeStruct((rows, n_out), F32),
        compiler_params=_params(1),
        name="ada",
    )(c_all, w_ada, b_ada.reshape(1, n_out))


_COL_Q, _COL_K, _COL_V, _COL_U, _COL_VA = range(5)


def _proj_prompt_kernel(x_ref, sc_ref, sh_ref, w_ref, lng_ref, lnb_ref, ws_ref, bs_ref, nag_ref,
                        qhi_ref, qlo_ref, k_ref, v_ref, kbf_ref, vbf_ref, km_ref, gm_ref):
    tm = x_ref.shape[0]
    h = (x_ref[...] * (1.0 + sc_ref[...]) + sh_ref[...]).astype(BF16)

    def proj(c):
        return _dot(h, w_ref[:, c * W_B:(c + 1) * W_B])

    def gmlp(u, acc_va):
        va = _layer_norm(jax.nn.gelu(acc_va), lng_ref[...], lnb_ref[...]).astype(BF16)
        n_chunks = tm // CHUNK
        row = lax.broadcasted_iota(jnp.int32, (CHUNK, CHUNK), 0)
        col = lax.broadcasted_iota(jnp.int32, (CHUNK, CHUNK), 1)
        groups = []
        for g in range(N_GROUPS_A):
            gs = slice(g * HEAD_DIM, (g + 1) * HEAD_DIM)
            w_g = jnp.where(row >= col, ws_ref[g], 0.0).astype(BF16)
            v_g = jnp.concatenate([va[c * CHUNK:(c + 1) * CHUNK, gs] for c in range(n_chunks)], axis=1)
            m_g = _dot(w_g, v_g)
            mixed = jnp.concatenate([m_g[:, c * HEAD_DIM:(c + 1) * HEAD_DIM] + bs_ref[g]
                                     for c in range(n_chunks)], axis=0)
            groups.append(u[:, gs] * mixed)
        gm_ref[...] = _rms_norm(jnp.concatenate(groups, axis=1), nag_ref[...]).astype(BF16)

    def put_q(acc):
        q_hi, q_lo = _split_bf16(acc * (HEAD_DIM ** -0.5))
        qhi_ref[...] = q_hi
        qlo_ref[...] = q_lo

    def put_k(acc):
        k_ref[...] = acc
        kbf_ref[...] = acc.astype(BF16)
        km_ref[...] = jnp.mean(acc.reshape(tm // MOBA_BLOCK, MOBA_BLOCK, W_B), axis=1)

    def put_v(acc):
        v_ref[...] = acc
        vbf_ref[...] = acc.astype(BF16)

    acc_u = proj(_COL_U)
    acc_va = proj(_COL_VA)
    u = jax.nn.gelu(acc_u)
    acc_q = proj(_COL_Q)
    gmlp(u, acc_va)
    acc_k = proj(_COL_K)
    put_q(acc_q)
    acc_v = proj(_COL_V)
    put_k(acc_k)
    put_v(acc_v)


def _proj_prompt(x, mod_p, w_in, ln_g, ln_b, w_s, bs_full, norm_a_g, *, tm, seq):
    rows = x.shape[0]
    tiles_per_seq = seq // tm
    row_blk = lambda i: (i, 0)
    vec = lambda i: (0, 0)
    whole3 = lambda i: (0, 0, 0)
    out_shape = (
        jax.ShapeDtypeStruct((rows, W_B), BF16),
        jax.ShapeDtypeStruct((rows, W_B), BF16),
        jax.ShapeDtypeStruct((rows, W_B), F32),
        jax.ShapeDtypeStruct((rows, W_B), F32),
        jax.ShapeDtypeStruct((rows, W_B), BF16),
        jax.ShapeDtypeStruct((rows, W_B), BF16),
        jax.ShapeDtypeStruct((rows // tm, tm // MOBA_BLOCK, W_B), F32),
        jax.ShapeDtypeStruct((rows, W_A), BF16),
    )
    out_specs = (
        pl.BlockSpec((tm, W_B), row_blk), pl.BlockSpec((tm, W_B), row_blk),
        pl.BlockSpec((tm, W_B), row_blk), pl.BlockSpec((tm, W_B), row_blk),
        pl.BlockSpec((tm, W_B), row_blk), pl.BlockSpec((tm, W_B), row_blk),
        pl.BlockSpec((None, tm // MOBA_BLOCK, W_B), lambda i: (i, 0, 0)),
        pl.BlockSpec((tm, W_A), row_blk),
    )
    return pl.pallas_call(
        _proj_prompt_kernel,
        grid=(rows // tm,),
        in_specs=[pl.BlockSpec((tm, D_MODEL), row_blk),
                  pl.BlockSpec((None, 1, D_MODEL), lambda i: (i // tiles_per_seq, 0, 1)),
                  pl.BlockSpec((None, 1, D_MODEL), lambda i: (i // tiles_per_seq, 0, 0)),
                  pl.BlockSpec(w_in.shape, vec, pipeline_mode=pl.Buffered(1)),
                  pl.BlockSpec((1, W_A), vec), pl.BlockSpec((1, W_A), vec),
                  pl.BlockSpec((N_GROUPS_A, CHUNK, CHUNK), whole3),
                  pl.BlockSpec((N_GROUPS_A, CHUNK, HEAD_DIM), whole3),
                  pl.BlockSpec((1, W_A), vec)],
        out_specs=out_specs,
        out_shape=out_shape,
        compiler_params=_params(1),
        name="proj_prompt",
    )(x, mod_p, mod_p, w_in, ln_g, ln_b, w_s, bs_full, norm_a_g)


def _proj_sample_kernel(x_ref, sc_ref, sh_ref, w_ref, lng_ref, lnb_ref,
                        q_ref, k_ref, v_ref, u_ref, va_ref, h_scr):
    j = pl.program_id(1)

    @pl.when(j == 0)
    def _():
        h_scr[...] = (x_ref[...] * (1.0 + sc_ref[...]) + sh_ref[...]).astype(BF16)

    acc = _dot(h_scr[...], w_ref[...])

    @pl.when(j == 0)
    def _():
        q_ref[...] = acc * (HEAD_DIM ** -0.5)

    @pl.when(j == 1)
    def _():
        k_ref[...] = acc

    @pl.when(j == 2)
    def _():
        v_ref[...] = acc

    @pl.when(j == 3)
    def _():
        u_ref[...] = jax.nn.gelu(acc)

    @pl.when(j == 4)
    def _():
        va_ref[...] = _layer_norm(jax.nn.gelu(acc), lng_ref[...], lnb_ref[...])


def _proj_sample(x2d, mod_s, w_in, ln_g, ln_b, *, n_seq, n_tok):
    n_col = w_in.shape[1] // W_B
    tok_blk = lambda t, j: (0, t)
    vec = lambda t, j: (0, 0)
    out = jax.ShapeDtypeStruct((n_seq, n_tok * W_B), F32)
    return pl.pallas_call(
        _proj_sample_kernel,
        grid=(n_tok, n_col),
        in_specs=[pl.BlockSpec((n_seq, D_MODEL), tok_blk),
                  pl.BlockSpec((n_seq, D_MODEL), lambda t, j: (0, 1)),
                  pl.BlockSpec((n_seq, D_MODEL), lambda t, j: (0, 0)),
                  pl.BlockSpec((D_MODEL, W_B), lambda t, j: (0, j)),
                  pl.BlockSpec((1, W_A), vec), pl.BlockSpec((1, W_A), vec)],
        out_specs=tuple(pl.BlockSpec((n_seq, W_B), tok_blk) for _ in range(5)),
        out_shape=(out,) * 5,
        scratch_shapes=[pltpu.VMEM((n_seq, D_MODEL), BF16)],
        compiler_params=_params(2),
        name="proj_sample",
    )(x2d, mod_s, mod_s, w_in, ln_g, ln_b)


def _attn_prompt_kernel(qhi_ref, qlo_ref, k_ref, v_ref, km_ref, rev_ref, o_ref, bias_scr):
    n_blocks = k_ref.shape[0] // MOBA_BLOCK

    @pl.when(pl.program_id(1) == 0)
    def _():
        for d in range(n_blocks):
            r = jnp.broadcast_to(rev_ref[d:d + 1, :], (MOBA_BLOCK, 2 * MOBA_BLOCK))
            rolled = pltpu.roll(r, 0, 1, stride=1, stride_axis=0)
            bias_scr[d] = rolled[:, MOBA_BLOCK:]

    km_hi, km_lo = _split_bf16(km_ref[...])
    row = lax.broadcasted_iota(jnp.int32, (MOBA_BLOCK, MOBA_BLOCK), 0)
    col = lax.broadcasted_iota(jnp.int32, (MOBA_BLOCK, MOBA_BLOCK), 1)

    def rows_of(qb):
        return slice(qb * MOBA_BLOCK, (qb + 1) * MOBA_BLOCK)

    def qk(qb):
        return _dot_nt(qhi_ref[rows_of(qb), :], k_ref[0:(qb + 1) * MOBA_BLOCK, :])

    def softmax(qb, logits):
        sel = None
        if qb > MOBA_TOPK:
            q_hi, q_lo = qhi_ref[rows_of(qb), :], qlo_ref[rows_of(qb), :]
            s_t = _dot_nt(km_hi, q_hi) + _dot_nt(km_lo, q_hi) + _dot_nt(km_hi, q_lo)
            sel = _topk_select(s_t, qb, axis=0).T
        pieces = []
        for jb in range(qb + 1):
            lj = logits[:, jb * MOBA_BLOCK:(jb + 1) * MOBA_BLOCK] + bias_scr[qb - jb]
            if jb == qb:
                lj = jnp.where(row >= col, lj, NEG_INF)
            elif sel is not None:
                lj = jnp.where(sel[:, jb:jb + 1] > 0.0, lj, NEG_INF)
            pieces.append(lj)
        lg = jnp.concatenate(pieces, axis=1) if len(pieces) > 1 else pieces[0]
        m = jnp.max(lg, axis=-1, keepdims=True)
        p = jnp.exp(lg - m)
        return p.astype(BF16), jnp.sum(p, axis=-1, keepdims=True)

    def pv(qb, p, z):
        o_ref[rows_of(qb), :] = _dot(p, v_ref[0:(qb + 1) * MOBA_BLOCK, :]) / z

    logits = qk(0)
    prev = None
    for qb in range(n_blocks):
        nxt = qk(qb + 1) if qb + 1 < n_blocks else None
        p, z = softmax(qb, logits)
        if prev is not None:
            pv(*prev)
        prev = (qb, p, z)
        logits = nxt
    pv(*prev)


def _attn_prompt(q_hi, q_lo, k_bf, v_bf, k_mean, rev_tab, *, n_seq, seq):
    n_blocks = seq // MOBA_BLOCK
    qkv = pl.BlockSpec((seq, HEAD_DIM), lambda h, n: (n, h))
    return pl.pallas_call(
        _attn_prompt_kernel,
        grid=(N_HEADS_B, n_seq),
        in_specs=[qkv, qkv, qkv, qkv,
                  pl.BlockSpec((None, n_blocks, HEAD_DIM), lambda h, n: (n, 0, h)),
                  pl.BlockSpec((None, n_blocks, 2 * MOBA_BLOCK), lambda h, n: (h, 0, 0))],
        out_specs=pl.BlockSpec((seq, HEAD_DIM), lambda h, n: (n, h)),
        out_shape=jax.ShapeDtypeStruct((n_seq * seq, W_B), F32),
        scratch_shapes=[pltpu.VMEM((n_blocks, MOBA_BLOCK, MOBA_BLOCK), F32)],
        compiler_params=_params(2),
        name="attn_prompt",
    )(q_hi, q_lo, k_bf, v_bf, k_mean, rev_tab)


def _attn_sample_kernel(n_pages, pt_ref, q_ref, kn_ref, vn_ref, bias_ref, biasn_ref, *refs):
    k_pages = refs[:n_pages]
    v_pages = refs[n_pages:2 * n_pages]
    o_ref = refs[2 * n_pages]
    n_tok = q_ref.shape[0]
    rows = n_tok * N_HEADS_B
    cols = PAGE_SIZE * N_HEADS_B
    pages_per_block = MOBA_BLOCK // PAGE_SIZE
    n_blocks = n_pages // pages_per_block

    q = q_ref[...].reshape(rows, HEAD_DIM)
    q_bf = q.astype(BF16)

    logit_pages = []
    k_sums = []
    for p in range(n_pages):
        kp = k_pages[p][...]
        k_sums.append(jnp.sum(kp, axis=0))
        logit_pages.append(_dot_nt(q_bf, kp.reshape(cols, HEAD_DIM).astype(BF16)))

    blk_lane = lax.broadcasted_iota(jnp.int32, (rows, n_blocks), 1)
    scores = jnp.zeros((rows, n_blocks), F32)
    for b in range(n_blocks):
        km = sum(k_sums[b * pages_per_block:(b + 1) * pages_per_block]) * (1.0 / MOBA_BLOCK)
        s_b = jnp.sum(q * jnp.concatenate([km] * n_tok, axis=0), axis=1, keepdims=True)
        scores = jnp.where(blk_lane == b, s_b, scores)
    sel = _topk_select(scores, n_blocks, axis=1)

    masked = []
    for p in range(n_pages):
        b = p // pages_per_block
        lp = logit_pages[p] + bias_ref[:, p * cols:(p + 1) * cols]
        masked.append(jnp.where(sel[:, b:b + 1] > 0.0, lp, NEG_INF))
    ln = _dot_nt(q_bf, kn_ref[...].reshape(rows, HEAD_DIM).astype(BF16)) + biasn_ref[...]

    m = jnp.maximum(jnp.max(functools.reduce(jnp.maximum, masked), axis=-1, keepdims=True),
                    jnp.max(ln, axis=-1, keepdims=True))
    p_new = jnp.exp(ln - m)
    acc = _dot(p_new.astype(BF16), vn_ref[...].reshape(rows, HEAD_DIM).astype(BF16))
    p_sum = None
    for p in range(n_pages):
        pp = jnp.exp(masked[p] - m)
        p_sum = pp if p_sum is None else p_sum + pp
        acc = acc + _dot(pp.astype(BF16), v_pages[p][...].reshape(cols, HEAD_DIM).astype(BF16))
    z = jnp.sum(p_sum, axis=-1, keepdims=True) + jnp.sum(p_new, axis=-1, keepdims=True)
    o_ref[...] = (acc / z).reshape(n_tok, N_HEADS_B, HEAD_DIM)


def _attn_sample(page_table, q, k_new, v_new, bias_past, bias_new, cache_k, cache_v):
    n_seq, n_pages = page_table.shape
    n_tok = q.shape[1]
    tok = pl.BlockSpec((None, n_tok, N_HEADS_B, HEAD_DIM), lambda n, pt: (n, 0, 0, 0))

    def page_spec(p):
        return pl.BlockSpec((None, None, PAGE_SIZE, N_HEADS_B, HEAD_DIM),
                            lambda n, pt: (0, pt[n * n_pages + p], 0, 0, 0))

    grid_spec = pltpu.PrefetchScalarGridSpec(
        num_scalar_prefetch=1,
        grid=(n_seq,),
        in_specs=[tok, tok, tok,
                  pl.BlockSpec(bias_past.shape, lambda n, pt: (0, 0)),
                  pl.BlockSpec(bias_new.shape, lambda n, pt: (0, 0))]
                 + [page_spec(p) for p in range(n_pages)] * 2,
        out_specs=tok,
    )
    return pl.pallas_call(
        functools.partial(_attn_sample_kernel, n_pages),
        grid_spec=grid_spec,
        out_shape=jax.ShapeDtypeStruct((n_seq, n_tok, N_HEADS_B, HEAD_DIM), F32),
        compiler_params=_params(1),
        name="attn_sample",
    )(page_table.reshape(-1), q, k_new, v_new, bias_past, bias_new,
      *([cache_k] * n_pages), *([cache_v] * n_pages))


def _mix_tail(x, gm_n, attn, g1, sc2, sh2, wo_ref, nbg_ref, l1g_ref, l1b_ref, x1_ref, h2_ref):
    a_n = _rms_norm(attn, nbg_ref[...]).astype(BF16)
    o = _dot(gm_n, wo_ref[0:W_A, :]) + _dot(a_n, wo_ref[W_A:W_A + W_B, :])
    x1 = _layer_norm(ALPHA * x + g1 * o, l1g_ref[...], l1b_ref[...])
    x1_ref[...] = x1
    h2_ref[...] = (x1 * (1.0 + sc2) + sh2).astype(BF16)


def _mix_prompt_kernel(x_ref, gm_ref, at_ref, g1_ref, sc2_ref, sh2_ref, wo_ref, nbg_ref, l1g_ref, l1b_ref,
                       x1_ref, h2_ref):
    _mix_tail(x_ref[...], gm_ref[...], at_ref[...], g1_ref[...], sc2_ref[...], sh2_ref[...],
              wo_ref, nbg_ref, l1g_ref, l1b_ref, x1_ref, h2_ref)


def _mix_prompt(x, gm_n, attn, mod_p, w_out, norm_b_g, ln1_g, ln1_b, *, tm, seq):
    rows = x.shape[0]
    tiles_per_seq = seq // tm
    row_blk = lambda i: (i, 0)
    vec = lambda i: (0, 0)
    mod = lambda c: pl.BlockSpec((None, 1, D_MODEL), lambda i: (i // tiles_per_seq, 0, c))
    return pl.pallas_call(
        _mix_prompt_kernel,
        grid=(rows // tm,),
        in_specs=[pl.BlockSpec((tm, D_MODEL), row_blk), pl.BlockSpec((tm, W_A), row_blk),
                  pl.BlockSpec((tm, W_B), row_blk), mod(2), mod(4), mod(3),
                  pl.BlockSpec((D_MODEL, D_MODEL), vec), pl.BlockSpec((1, W_B), vec),
                  pl.BlockSpec((1, D_MODEL), vec), pl.BlockSpec((1, D_MODEL), vec)],
        out_specs=(pl.BlockSpec((tm, D_MODEL), row_blk), pl.BlockSpec((tm, D_MODEL), row_blk)),
        out_shape=(jax.ShapeDtypeStruct((rows, D_MODEL), F32), jax.ShapeDtypeStruct((rows, D_MODEL), BF16)),
        compiler_params=_params(1),
        name="mix_prompt",
    )(x, gm_n, attn, mod_p, mod_p, mod_p, w_out, norm_b_g, ln1_g, ln1_b)


def _mix_sample_kernel(x_ref, u_ref, va_ref, ws_ref, bs_ref, nag_ref, at_ref, g1_ref, sc2_ref, sh2_ref,
                       wo_ref, nbg_ref, l1g_ref, l1b_ref, x1_ref, h2_ref):
    n_tok = ws_ref.shape[0]
    mixed = bs_ref[...]
    for s in range(n_tok):
        mixed = mixed + ws_ref[s:s + 1, :] * va_ref[:, s * W_A:(s + 1) * W_A]
    gm_n = _rms_norm(u_ref[...] * mixed, nag_ref[...]).astype(BF16)
    _mix_tail(x_ref[...], gm_n, at_ref[...], g1_ref[...], sc2_ref[...], sh2_ref[...],
              wo_ref, nbg_ref, l1g_ref, l1b_ref, x1_ref, h2_ref)


def _mix_sample(x2d, u, va, ws_tok, bs_tok, norm_a_g, attn2d, mod_s, w_out, norm_b_g, ln1_g, ln1_b,
                *, n_seq, n_tok):
    tok_blk = lambda t: (0, t)
    vec = lambda t: (0, 0)
    mod = lambda c: pl.BlockSpec((n_seq, D_MODEL), lambda t: (0, c))
    out_blk = pl.BlockSpec((n_seq, D_MODEL), lambda t: (t, 0))
    return pl.pallas_call(
        _mix_sample_kernel,
        grid=(n_tok,),
        in_specs=[pl.BlockSpec((n_seq, D_MODEL), tok_blk), pl.BlockSpec((n_seq, W_A), tok_blk),
                  pl.BlockSpec((n_seq, n_tok * W_A), vec),
                  pl.BlockSpec((None, n_tok, W_A), lambda t: (t, 0, 0)),
                  pl.BlockSpec((None, 1, W_A), lambda t: (t, 0, 0)),
                  pl.BlockSpec((1, W_A), vec),
                  pl.BlockSpec((n_seq, W_B), tok_blk), mod(2), mod(4), mod(3),
                  pl.BlockSpec((D_MODEL, D_MODEL), vec), pl.BlockSpec((1, W_B), vec),
                  pl.BlockSpec((1, D_MODEL), vec), pl.BlockSpec((1, D_MODEL), vec)],
        out_specs=(out_blk, out_blk),
        out_shape=(jax.ShapeDtypeStruct((n_tok * n_seq, D_MODEL), F32),
                   jax.ShapeDtypeStruct((n_tok * n_seq, D_MODEL), BF16)),
        compiler_params=_params(1),
        name="mix_sample",
    )(x2d, u, va, ws_tok, bs_tok, norm_a_g, attn2d, mod_s, mod_s, mod_s, w_out, norm_b_g, ln1_g, ln1_b)


def _ffn_act(conv_a, conv_b):
    return (jax.nn.silu(conv_a) * conv_b).astype(BF16)


def _ffn_prompt_kernel(tiles_per_seq, h_ref, halo_ref, x1_ref, g2_ref, wa_ref, wb_ref, wd_ref, wc_ref, bc_ref,
                       l2g_ref, l2b_ref, y_ref, st_ref, acc_scr, h_scr):
    i = pl.program_id(0)
    j = pl.program_id(1)
    tm = h_ref.shape[0]
    n_halo = halo_ref.shape[0]

    @pl.when(j == 0)
    def _():
        seq_start = (i % tiles_per_seq) == 0
        h_scr[0:n_halo, :] = jnp.where(seq_start, jnp.zeros_like(halo_ref[...]), halo_ref[...])
        h_scr[n_halo:n_halo + tm, :] = h_ref[...]
        acc_scr[...] = jnp.zeros_like(acc_scr)

    h = h_scr[...]

    def conv(half, up, cs):
        st_ref[half, :, cs] = up[n_halo + tm - SUBLANES:n_halo + tm, :]
        out = bc_ref[half:half + 1, cs]
        for c in range(CONV_W):
            lag = CONV_W - 1 - c
            out = out + up[n_halo - lag:n_halo - lag + tm, :] * wc_ref[half, c:c + 1, cs]
        return out

    chunks = [slice(c0, c0 + MXU_COLS) for c0 in range(0, FF_TILE, MXU_COLS)]
    ups = [(_dot(h, wa_ref[:, cs]), _dot(h, wb_ref[:, cs])) for cs in chunks]
    acts = [_ffn_act(conv(0, up_a, cs), conv(1, up_b, cs)) for cs, (up_a, up_b) in zip(chunks, ups)]
    acc_scr[...] += _dot(jnp.concatenate(acts, axis=1), wd_ref[...])

    @pl.when(j == pl.num_programs(1) - 1)
    def _():
        y_ref[...] = _layer_norm(ALPHA * x1_ref[...] + g2_ref[...] * acc_scr[...], l2g_ref[...], l2b_ref[...])


def _ffn_prompt(h2, x1, mod_p, w_up_a, w_up_b, w_down, w_conv, b_conv, ln2_g, ln2_b, *, tm, seq):
    rows = h2.shape[0]
    tiles_per_seq = seq // tm
    row_blk = lambda i, j: (i, 0)
    vec = lambda i, j: (0, 0)
    halo_blk = lambda i, j: (jnp.maximum(i * (tm // BF16_ROWS) - 1, 0), 0)
    return pl.pallas_call(
        functools.partial(_ffn_prompt_kernel, tiles_per_seq),
        grid=(rows // tm, N_FF_TILES),
        in_specs=[pl.BlockSpec((tm, D_MODEL), row_blk), pl.BlockSpec((BF16_ROWS, D_MODEL), halo_blk),
                  pl.BlockSpec((tm, D_MODEL), row_blk),
                  pl.BlockSpec((None, 1, D_MODEL), lambda i, j: (i // tiles_per_seq, 0, 5)),
                  pl.BlockSpec((D_MODEL, FF_TILE), lambda i, j: (0, j)),
                  pl.BlockSpec((D_MODEL, FF_TILE), lambda i, j: (0, j)),
                  pl.BlockSpec((FF_TILE, D_MODEL), lambda i, j: (j, 0)),
                  pl.BlockSpec((2, CONV_W, FF_TILE), lambda i, j: (0, 0, j)),
                  pl.BlockSpec((2, FF_TILE), lambda i, j: (0, j)),
                  pl.BlockSpec((1, D_MODEL), vec), pl.BlockSpec((1, D_MODEL), vec)],
        out_specs=(pl.BlockSpec((tm, D_MODEL), row_blk),
                   pl.BlockSpec((None, 2, SUBLANES, FF_TILE), lambda i, j: (i, 0, 0, j))),
        out_shape=(jax.ShapeDtypeStruct((rows, D_MODEL), F32),
                   jax.ShapeDtypeStruct((rows // tm, 2, SUBLANES, D_FF_PAD), F32)),
        scratch_shapes=[pltpu.VMEM((tm, D_MODEL), F32), pltpu.VMEM((BF16_ROWS + tm, D_MODEL), BF16)],
        compiler_params=_params(2),
        name="ffn_prompt",
    )(h2, h2, x1, mod_p, w_up_a, w_up_b, w_down, w_conv, b_conv, ln2_g, ln2_b)


def _ffn_sample_kernel(n_tok, h_ref, x1_ref, g2_ref, wa_ref, wb_ref, wd_ref, wc_ref, bc_ref, past_ref,
                       l2g_ref, l2b_ref, y_ref, st_ref, acc_scr):
    j = pl.program_id(0)
    n_seq = g2_ref.shape[0]
    h = h_ref[...]

    def conv(half, w_ref):
        up = _dot(h, w_ref[...])
        full = [past_ref[r, half] for r in range(CONV_W - 1)]
        full += [up[t * n_seq:(t + 1) * n_seq, :] for t in range(n_tok)]
        for r in range(CONV_W - 1):
            st_ref[r, half] = full[n_tok + r]
        w = wc_ref[half]
        b = bc_ref[half:half + 1, :]
        return jnp.concatenate(
            [b + sum(full[t + c] * w[c:c + 1, :] for c in range(CONV_W)) for t in range(n_tok)], axis=0)

    part = _dot(_ffn_act(conv(0, wa_ref), conv(1, wb_ref)), wd_ref[...])

    @pl.when(j == 0)
    def _():
        acc_scr[...] = part

    @pl.when(j > 0)
    def _():
        acc_scr[...] += part

    @pl.when(j == pl.num_programs(0) - 1)
    def _():
        g2 = g2_ref[...]
        for t in range(n_tok):
            rs = slice(t * n_seq, (t + 1) * n_seq)
            y_ref[:, t * D_MODEL:(t + 1) * D_MODEL] = _layer_norm(
                ALPHA * x1_ref[rs, :] + g2 * acc_scr[rs, :], l2g_ref[...], l2b_ref[...])


def _ffn_sample(h2, x1, mod_s, w_up_a, w_up_b, w_down, w_conv, b_conv, past, ln2_g, ln2_b, *, n_seq, n_tok):
    rows = n_seq * n_tok
    full = lambda j: (0, 0)
    state_blk = pl.BlockSpec((CONV_W - 1, 2, n_seq, FF_TILE), lambda j: (0, 0, 0, j))
    return pl.pallas_call(
        functools.partial(_ffn_sample_kernel, n_tok),
        grid=(N_FF_TILES,),
        in_specs=[pl.BlockSpec((rows, D_MODEL), full), pl.BlockSpec((rows, D_MODEL), full),
                  pl.BlockSpec((n_seq, D_MODEL), lambda j: (0, 5)),
                  pl.BlockSpec((D_MODEL, FF_TILE), lambda j: (0, j)),
                  pl.BlockSpec((D_MODEL, FF_TILE), lambda j: (0, j)),
                  pl.BlockSpec((FF_TILE, D_MODEL), lambda j: (j, 0)),
                  pl.BlockSpec((2, CONV_W, FF_TILE), lambda j: (0, 0, j)),
                  pl.BlockSpec((2, FF_TILE), lambda j: (0, j)),
                  state_blk,
                  pl.BlockSpec((1, D_MODEL), full), pl.BlockSpec((1, D_MODEL), full)],
        out_specs=(pl.BlockSpec((n_seq, n_tok * D_MODEL), full), state_blk),
        out_shape=(jax.ShapeDtypeStruct((n_seq, n_tok * D_MODEL), F32),
                   jax.ShapeDtypeStruct((CONV_W - 1, 2, n_seq, D_FF_PAD), F32)),
        scratch_shapes=[pltpu.VMEM((rows, D_MODEL), F32)],
        compiler_params=_params(1),
        name="ffn_sample",
    )(h2, x1, mod_s, w_up_a, w_up_b, w_down, w_conv, b_conv, past, ln2_g, ln2_b)


def _t5_bucket_table(n_rel):
    max_exact = NUM_BUCKETS // 2
    ratio = MAX_DISTANCE // max_exact
    log_ratio = ratio.bit_length() - 1
    assert 1 << log_ratio == ratio
    steps = NUM_BUCKETS - max_exact
    out = np.zeros((n_rel,), np.int32)
    for n in range(n_rel):
        if n < max_exact:
            out[n] = n
            continue
        k = 0
        while k + 1 < steps and n ** steps >= (max_exact ** steps) << (log_ratio * (k + 1)):
            k += 1
        out[n] = min(max_exact + k, NUM_BUCKETS - 1)
    return out


def _bias_tables(rel_bias, seq, past, n_tok):
    n_rel = max(seq, past + n_tok)
    onehot = np.zeros((NUM_BUCKETS, n_rel), np.float32)
    onehot[_t5_bucket_table(n_rel), np.arange(n_rel)] = 1.0
    tab = jnp.dot(rel_bias.astype(F32).T, onehot, precision=lax.Precision.HIGHEST)
    n_blocks = seq // MOBA_BLOCK
    tab_pad = jnp.pad(tab, ((0, 0), (MOBA_BLOCK, 1)))
    rev = jnp.stack([tab_pad[:, d * MOBA_BLOCK + 1:(d + 2) * MOBA_BLOCK + 1][:, ::-1]
                     for d in range(n_blocks)], axis=1)
    same_head = np.eye(N_HEADS_B, dtype=bool)
    by_key = jnp.stack([tab[:, t + 1:t + 1 + past][:, ::-1] for t in range(n_tok)], axis=0)
    n_pages = past // PAGE_SIZE
    spread = np.repeat(np.eye(PAGE_SIZE, dtype=np.float32), N_HEADS_B, axis=1)
    by_col = jnp.einsum('rpk,kc->rpc', by_key.reshape(n_tok * N_HEADS_B, n_pages, PAGE_SIZE), spread,
                        precision=lax.Precision.HIGHEST).reshape(n_tok * N_HEADS_B, past * N_HEADS_B)
    row_head = np.tile(np.arange(N_HEADS_B), n_tok)[:, None]
    col_head = np.tile(np.arange(N_HEADS_B), past)[None, :]
    bias_past = jnp.where(row_head == col_head, by_col, NEG_INF)
    new_rows = []
    for t in range(n_tok):
        by_tok = jnp.stack([tab[:, max(t - s, 0)] for s in range(n_tok)], axis=1)
        ok = same_head[:, None, :] & (np.arange(n_tok) <= t)[None, :, None]
        new_rows.append(jnp.where(ok, by_tok[:, :, None], NEG_INF).reshape(N_HEADS_B, n_tok * N_HEADS_B))
    bias_new = jnp.stack(new_rows, axis=0).reshape(n_tok * N_HEADS_B, n_tok * N_HEADS_B)
    return rev, bias_past, bias_new


def _pad_ff(a, axis):
    pad = [(0, 0)] * a.ndim
    pad[axis] = (0, D_FF_PAD - D_FF)
    return jnp.pad(a, pad)


def kernel(x_prompt, x_sample, cache_k, cache_v, state_ffn_conv, page_table, c_prompt, c_sample,
           w_ada, b_ada, w_in, ln_sgu_g, ln_sgu_b, w_s, b_s, rel_bias, norm_a_g, norm_b_g, w_out,
           ln1_g, ln1_b, w_up, w_conv, b_conv, w_down, ln2_g, ln2_b):
    n_p, seq, _ = x_prompt.shape
    n_s, n_tok, _ = x_sample.shape
    n_pages = page_table.shape[1]
    past = n_pages * PAGE_SIZE
    assert w_ada.shape[0] == 1 and seq % MOBA_BLOCK == 0 and past % MOBA_BLOCK == 0 and n_tok <= CHUNK

    w_in_bf = w_in[0].astype(BF16)
    w_out_bf = w_out[0].astype(BF16)
    w_up_a = _cast_pad_cols(w_up[0], 0, D_FF, D_FF_PAD, FF_TILE)
    w_up_b = _cast_pad_cols(w_up[0], D_FF, D_FF, D_FF_PAD, FF_TILE)
    w_down_bf = _cast_pad_rows(w_down[0], D_FF_PAD, FF_TILE)
    w_conv_p = _pad_ff(jnp.transpose(w_conv[0].reshape(CONV_W, 2, D_FF), (1, 0, 2)), 2)
    b_conv_p = _pad_ff(b_conv[0].reshape(2, D_FF), 1)
    row = lambda a: a.reshape(1, -1)
    ln_g, ln_b = row(ln_sgu_g[0]), row(ln_sgu_b[0])
    nag, nbg = row(norm_a_g[0]), row(norm_b_g[0])
    l1g, l1b, l2g, l2b = row(ln1_g[0]), row(ln1_b[0]), row(ln2_g[0]), row(ln2_b[0])
    bs_full = jnp.broadcast_to(b_s[0][:, :, None], (N_GROUPS_A, CHUNK, HEAD_DIM))
    tril_tok = np.tril(np.ones((n_tok, n_tok), np.float32))
    ws_tok = jnp.repeat(jnp.transpose(w_s[0][:, :n_tok, :n_tok] * tril_tok, (1, 2, 0)), HEAD_DIM, axis=2)
    bs_tok = jnp.repeat(b_s[0][:, :n_tok].T, HEAD_DIM, axis=1).reshape(n_tok, 1, W_A)
    rev_tab, bias_past, bias_new = _bias_tables(rel_bias, seq, past, n_tok)

    pad_rows = (-(n_s + n_p)) % SUBLANES
    c_all = jnp.concatenate([c_sample, c_prompt, jnp.zeros((pad_rows, D_MODEL), F32)], axis=0)
    mod = _ada(c_all, w_ada[0], b_ada[0])
    mod_s = mod[:n_s]
    mod_p = mod[n_s:n_s + n_p].reshape(n_p, 1, 6 * D_MODEL)

    xp = x_prompt.reshape(n_p * seq, D_MODEL)
    q_hi, q_lo, k_p, v_p, k_bf, v_bf, k_mean, gm_n = _proj_prompt(
        xp, mod_p, w_in_bf, ln_g, ln_b, w_s[0], bs_full, nag, tm=256, seq=seq)
    attn_p = _attn_prompt(q_hi, q_lo, k_bf, v_bf, k_mean.reshape(n_p, seq // MOBA_BLOCK, W_B), rev_tab,
                          n_seq=n_p, seq=seq)
    x1_p, h2_p = _mix_prompt(xp, gm_n, attn_p, mod_p, w_out_bf, nbg, l1g, l1b, tm=512, seq=seq)
    y_p, st_p = _ffn_prompt(h2_p, x1_p, mod_p, w_up_a, w_up_b, w_down_bf, w_conv_p, b_conv_p, l2g, l2b,
                            tm=512, seq=seq)
    tiles_per_seq = st_p.shape[0] // n_p
    conv_p = st_p[tiles_per_seq - 1::tiles_per_seq, :, SUBLANES - (CONV_W - 1):, :D_FF]
    conv_p = jnp.transpose(conv_p, (0, 2, 1, 3)).reshape(1, n_p, CONV_W - 1, 2 * D_FF)

    xs = x_sample.reshape(n_s, n_tok * D_MODEL)
    q_s, k_s, v_s, u_s, va_s = _proj_sample(xs, mod_s, w_in_bf, ln_g, ln_b, n_seq=n_s, n_tok=n_tok)
    heads = lambda a: a.reshape(n_s, n_tok, N_HEADS_B, HEAD_DIM)
    attn_s = _attn_sample(page_table, heads(q_s), heads(k_s), heads(v_s), bias_past, bias_new, cache_k, cache_v)
    x1_s, h2_s = _mix_sample(xs, u_s, va_s, ws_tok, bs_tok, nag, attn_s.reshape(n_s, n_tok * W_B), mod_s,
                             w_out_bf, nbg, l1g, l1b, n_seq=n_s, n_tok=n_tok)
    past_up = state_ffn_conv[0].reshape(n_s, CONV_W - 1, 2, D_FF)
    past_up = _pad_ff(jnp.transpose(past_up, (1, 2, 0, 3)), 3)
    y_s, st_s = _ffn_sample(h2_s, x1_s, mod_s, w_up_a, w_up_b, w_down_bf, w_conv_p, b_conv_p, past_up,
                            l2g, l2b, n_seq=n_s, n_tok=n_tok)
    conv_s = jnp.transpose(st_s[..., :D_FF], (2, 0, 1, 3)).reshape(1, n_s, CONV_W - 1, 2 * D_FF)

    heads_p = (1, n_p, seq, N_HEADS_B, HEAD_DIM)
    heads_s = (1, n_s, n_tok, N_HEADS_B, HEAD_DIM)
    return (y_p.reshape(n_p, seq, D_MODEL), y_s.reshape(n_s, n_tok, D_MODEL),
            k_p.reshape(heads_p), v_p.reshape(heads_p), conv_p,
            k_s.reshape(heads_s), v_s.reshape(heads_s), va_s.reshape(1, n_s, n_tok, W_A), conv_s)
```

```python
import functools

import numpy as np
import jax
import jax.numpy as jnp
from jax import lax
from jax.experimental import pallas as pl
from jax.experimental.pallas import tpu as pltpu

F32 = jnp.float32
BF16 = jnp.bfloat16

D_MODEL = 2048
HEAD_DIM = 128
N_HEADS_B = 8
N_GROUPS_A = 8
W_B = N_HEADS_B * HEAD_DIM
W_A = N_GROUPS_A * HEAD_DIM
CHUNK = 128
MOBA_BLOCK = 256
MOBA_TOPK = 3
NUM_BUCKETS = 32
MAX_DISTANCE = 1024
CONV_W = 3
D_FF = 5504
PAGE_SIZE = 128
ALPHA = 2.0 ** 0.25
LN_EPS = 1e-5
NEG_INF = -1e30

FF_TILE = 512
D_FF_PAD = ((D_FF + FF_TILE - 1) // FF_TILE) * FF_TILE
N_FF_TILES = D_FF_PAD // FF_TILE
SUBLANES = 8
BF16_ROWS = 16
FFN_ROWS = 256
VMEM_LIMIT_BYTES = 56 * 1024 * 1024

_NT = (((1,), (1,)), ((), ()))


def _params(n_axes, flags=None):
    return pltpu.CompilerParams(dimension_semantics=("arbitrary",) * n_axes,
                                vmem_limit_bytes=VMEM_LIMIT_BYTES, flags=flags)


def _layer_norm(x, g, b):
    mu = jnp.mean(x, axis=-1, keepdims=True)
    xc = x - mu
    var = jnp.mean(xc * xc, axis=-1, keepdims=True)
    return xc * lax.rsqrt(var + LN_EPS) * g + b


def _rms_norm(x, g):
    return x * lax.rsqrt(jnp.mean(x * x, axis=-1, keepdims=True) + LN_EPS) * g


def _split_bf16(x):
    hi = x.astype(BF16)
    lo = (x - hi.astype(F32)).astype(BF16)
    return hi, lo


def _dot(a, b):
    return jnp.dot(a, b, preferred_element_type=F32)


def _dot_nt(a, b):
    return lax.dot_general(a, b, _NT, preferred_element_type=F32)


def _block_scores(q_hi, q_lo, km):
    km_hi, km_lo = _split_bf16(km)
    return _dot_nt(q_hi, km_hi) + _dot_nt(q_hi, km_lo) + _dot_nt(q_lo, km_hi)


def _topk_select(s, n_valid, axis):
    idx = lax.broadcasted_iota(jnp.int32, s.shape, axis)
    rank = jnp.zeros(s.shape, F32)
    for i in range(n_valid):
        si = lax.slice_in_dim(s, i, i + 1, axis=axis)
        beats = (si > s) | ((si == s) & (i < idx))
        rank = rank + beats.astype(F32)
    return ((rank < MOBA_TOPK) & (idx < n_valid)).astype(F32)


def _cast_pad_kernel(n_valid, axis, *refs):
    *in_refs, o_ref = refs
    x = in_refs[0][...] if len(in_refs) == 1 else jnp.concatenate([r[...] for r in in_refs], axis=1)
    pos = pl.program_id(0) * o_ref.shape[axis] + lax.broadcasted_iota(jnp.int32, o_ref.shape, axis)
    o_ref[...] = jnp.where(pos < n_valid, x, 0.0).astype(BF16)


def _cast_pad_cols(w, col0, n_valid, n_out, tile):
    rows, n_cols = w.shape
    lanes = 128
    assert col0 % lanes == 0 and n_cols % lanes == 0 and tile % lanes == 0 and n_out % tile == 0
    last = n_cols // lanes - 1
    pieces = tile // lanes

    def piece(q):
        return pl.BlockSpec((rows, lanes), lambda j: (0, jnp.minimum(col0 // lanes + j * pieces + q, last)))

    return pl.pallas_call(
        functools.partial(_cast_pad_kernel, n_valid, 1),
        grid=(n_out // tile,),
        in_specs=[piece(q) for q in range(pieces)],
        out_specs=pl.BlockSpec((rows, tile), lambda j: (0, j)),
        out_shape=jax.ShapeDtypeStruct((rows, n_out), BF16),
        compiler_params=_params(1),
        name="cast_pad_cols",
    )(*([w] * pieces))


def _cast_pad_rows(w, n_out, tile):
    n_rows, cols = w.shape
    assert n_out % tile == 0 and (n_out - tile) < n_rows
    return pl.pallas_call(
        functools.partial(_cast_pad_kernel, n_rows, 0),
        grid=(n_out // tile,),
        in_specs=[pl.BlockSpec((tile, cols), lambda j: (j, 0))],
        out_specs=pl.BlockSpec((tile, cols), lambda j: (j, 0)),
        out_shape=jax.ShapeDtypeStruct((n_out, cols), BF16),
        compiler_params=_params(1),
        name="cast_pad_rows",
    )(w)


def _ada_kernel(c_ref, w_ref, b_ref, o_ref):
    c = c_ref[...]
    a = jax.nn.silu(c).astype(BF16)
    o_ref[...] = _dot(a, w_ref[...].astype(BF16)) + b_ref[...]


def _ada(c_all, w_ada, b_ada):
    rows = c_all.shape[0]
    n_out = w_ada.shape[1]
    tn = 512
    return pl.pallas_call(
        _ada_kernel,
        grid=(n_out // tn,),
        in_specs=[pl.BlockSpec((rows, D_MODEL), lambda j: (0, 0)),
                  pl.BlockSpec((D_MODEL, tn), lambda j: (0, j)),
                  pl.BlockSpec((1, tn), lambda j: (0, j))],
        out_specs=pl.BlockSpec((rows, tn), lambda j: (0, j)),
        out_shape=jax.ShapeDtypeStruct((rows, n_out), F32),
        compiler_params=_params(1),
        name="ada",
    )(c_all, w_ada, b_ada.reshape(1, n_out))


_COL_Q, _COL_K, _COL_V, _COL_U, _COL_VA = range(5)


def _proj_prompt_kernel(x_ref, sc_ref, sh_ref, w_ref, lng_ref, lnb_ref, ws_ref, bs_ref, nag_ref,
                        qhi_ref, qlo_ref, k_ref, v_ref, kbf_ref, vbf_ref, km_ref, gm_ref):
    tm = x_ref.shape[0]
    h = (x_ref[...] * (1.0 + sc_ref[...]) + sh_ref[...]).astype(BF16)

    def proj(c):
        return _dot(h, w_ref[:, c * W_B:(c + 1) * W_B])

    def gmlp(u, acc_va):
        va = _layer_norm(jax.nn.gelu(acc_va), lng_ref[...], lnb_ref[...]).astype(BF16)
        n_chunks = tm // CHUNK
        row = lax.broadcasted_iota(jnp.int32, (CHUNK, CHUNK), 0)
        col = lax.broadcasted_iota(jnp.int32, (CHUNK, CHUNK), 1)
        groups = []
        for g in range(N_GROUPS_A):
            gs = slice(g * HEAD_DIM, (g + 1) * HEAD_DIM)
            w_g = jnp.where(row >= col, ws_ref[g], 0.0).astype(BF16)
            v_g = jnp.concatenate([va[c * CHUNK:(c + 1) * CHUNK, gs] for c in range(n_chunks)], axis=1)
            m_g = _dot(w_g, v_g)
            mixed = jnp.concatenate([m_g[:, c * HEAD_DIM:(c + 1) * HEAD_DIM] + bs_ref[g]
                                     for c in range(n_chunks)], axis=0)
            groups.append(u[:, gs] * mixed)
        gm_ref[...] = _rms_norm(jnp.concatenate(groups, axis=1), nag_ref[...]).astype(BF16)

    def put_q(acc):
        q_hi, q_lo = _split_bf16(acc * (HEAD_DIM ** -0.5))
        qhi_ref[...] = q_hi
        qlo_ref[...] = q_lo

    def put_k(acc):
        k_ref[...] = acc
        kbf_ref[...] = acc.astype(BF16)
        km_ref[...] = jnp.mean(acc.reshape(tm // MOBA_BLOCK, MOBA_BLOCK, W_B), axis=1)

    def put_v(acc):
        v_ref[...] = acc
        vbf_ref[...] = acc.astype(BF16)

    acc_u = proj(_COL_U)
    acc_va = proj(_COL_VA)
    u = jax.nn.gelu(acc_u)
    acc_q = proj(_COL_Q)
    gmlp(u, acc_va)
    acc_k = proj(_COL_K)
    put_q(acc_q)
    acc_v = proj(_COL_V)
    put_k(acc_k)
    put_v(acc_v)


def _proj_prompt(x, mod_p, w_in, ln_g, ln_b, w_s, bs_full, norm_a_g, *, tm, seq):
    rows = x.shape[0]
    tiles_per_seq = seq // tm
    row_blk = lambda i: (i, 0)
    vec = lambda i: (0, 0)
    whole3 = lambda i: (0, 0, 0)
    out_shape = (
        jax.ShapeDtypeStruct((rows, W_B), BF16),
        jax.ShapeDtypeStruct((rows, W_B), BF16),
        jax.ShapeDtypeStruct((rows, W_B), F32),
        jax.ShapeDtypeStruct((rows, W_B), F32),
        jax.ShapeDtypeStruct((rows, W_B), BF16),
        jax.ShapeDtypeStruct((rows, W_B), BF16),
        jax.ShapeDtypeStruct((rows // tm, tm // MOBA_BLOCK, W_B), F32),
        jax.ShapeDtypeStruct((rows, W_A), BF16),
    )
    out_specs = (
        pl.BlockSpec((tm, W_B), row_blk), pl.BlockSpec((tm, W_B), row_blk),
        pl.BlockSpec((tm, W_B), row_blk), pl.BlockSpec((tm, W_B), row_blk),
        pl.BlockSpec((tm, W_B), row_blk), pl.BlockSpec((tm, W_B), row_blk),
        pl.BlockSpec((None, tm // MOBA_BLOCK, W_B), lambda i: (i, 0, 0)),
        pl.BlockSpec((tm, W_A), row_blk),
    )
    return pl.pallas_call(
        _proj_prompt_kernel,
        grid=(rows // tm,),
        in_specs=[pl.BlockSpec((tm, D_MODEL), row_blk),
                  pl.BlockSpec((None, 1, D_MODEL), lambda i: (i // tiles_per_seq, 0, 1)),
                  pl.BlockSpec((None, 1, D_MODEL), lambda i: (i // tiles_per_seq, 0, 0)),
                  pl.BlockSpec(w_in.shape, vec, pipeline_mode=pl.Buffered(1)),
                  pl.BlockSpec((1, W_A), vec), pl.BlockSpec((1, W_A), vec),
                  pl.BlockSpec((N_GROUPS_A, CHUNK, CHUNK), whole3),
                  pl.BlockSpec((N_GROUPS_A, CHUNK, HEAD_DIM), whole3),
                  pl.BlockSpec((1, W_A), vec)],
        out_specs=out_specs,
        out_shape=out_shape,
        compiler_params=_params(1),
        name="proj_prompt",
    )(x, mod_p, mod_p, w_in, ln_g, ln_b, w_s, bs_full, norm_a_g)


def _proj_sample_kernel(n_tok, x_ref, sc_ref, sh_ref, w_ref, lng_ref, lnb_ref,
                        q_ref, k_ref, v_ref, u_ref, va_ref, h_scr):
    j = pl.program_id(0)
    n_seq = sc_ref.shape[0]

    @pl.when(j == 0)
    def _():
        for t in range(n_tok):
            x_t = x_ref[:, t * D_MODEL:(t + 1) * D_MODEL]
            h_scr[t * n_seq:(t + 1) * n_seq, :] = (x_t * (1.0 + sc_ref[...]) + sh_ref[...]).astype(BF16)

    acc = _dot(h_scr[...], w_ref[...])

    def put(o_ref, val):
        for t in range(n_tok):
            o_ref[:, t * W_B:(t + 1) * W_B] = val[t * n_seq:(t + 1) * n_seq, :]

    @pl.when(j == _COL_Q)
    def _():
        put(q_ref, acc * (HEAD_DIM ** -0.5))

    @pl.when(j == _COL_K)
    def _():
        put(k_ref, acc)

    @pl.when(j == _COL_V)
    def _():
        put(v_ref, acc)

    @pl.when(j == _COL_U)
    def _():
        put(u_ref, jax.nn.gelu(acc))

    @pl.when(j == _COL_VA)
    def _():
        put(va_ref, _layer_norm(jax.nn.gelu(acc), lng_ref[...], lnb_ref[...]))


def _proj_sample(x2d, mod_s, w_in, ln_g, ln_b, *, n_seq, n_tok):
    n_col = w_in.shape[1] // W_B
    whole = lambda j: (0, 0)
    out = jax.ShapeDtypeStruct((n_seq, n_tok * W_B), F32)
    return pl.pallas_call(
        functools.partial(_proj_sample_kernel, n_tok),
        grid=(n_col,),
        in_specs=[pl.BlockSpec((n_seq, n_tok * D_MODEL), whole),
                  pl.BlockSpec((n_seq, D_MODEL), lambda j: (0, 1)),
                  pl.BlockSpec((n_seq, D_MODEL), lambda j: (0, 0)),
                  pl.BlockSpec((D_MODEL, W_B), lambda j: (0, j)),
                  pl.BlockSpec((1, W_A), whole), pl.BlockSpec((1, W_A), whole)],
        out_specs=tuple(pl.BlockSpec((n_seq, n_tok * W_B), whole) for _ in range(5)),
        out_shape=(out,) * 5,
        scratch_shapes=[pltpu.VMEM((n_tok * n_seq, D_MODEL), BF16)],
        compiler_params=_params(1),
        name="proj_sample",
    )(x2d, mod_s, mod_s, w_in, ln_g, ln_b)


def _attn_prompt_kernel(qhi_ref, qlo_ref, k_ref, v_ref, km_ref, rev_ref, o_ref, bias_scr):
    n_blocks = k_ref.shape[0] // MOBA_BLOCK

    @pl.when(pl.program_id(1) == 0)
    def _():
        for d in range(n_blocks):
            r = jnp.broadcast_to(rev_ref[d:d + 1, :], (MOBA_BLOCK, 2 * MOBA_BLOCK))
            rolled = pltpu.roll(r, 0, 1, stride=1, stride_axis=0)
            bias_scr[d] = rolled[:, MOBA_BLOCK:]

    km_hi, km_lo = _split_bf16(km_ref[...])
    row = lax.broadcasted_iota(jnp.int32, (MOBA_BLOCK, MOBA_BLOCK), 0)
    col = lax.broadcasted_iota(jnp.int32, (MOBA_BLOCK, MOBA_BLOCK), 1)

    def rows_of(qb):
        return slice(qb * MOBA_BLOCK, (qb + 1) * MOBA_BLOCK)

    def qk(qb):
        return _dot_nt(qhi_ref[rows_of(qb), :], k_ref[0:(qb + 1) * MOBA_BLOCK, :])

    def softmax(qb, logits):
        sel = None
        if qb > MOBA_TOPK:
            q_hi, q_lo = qhi_ref[rows_of(qb), :], qlo_ref[rows_of(qb), :]
            s_t = _dot_nt(km_hi, q_hi) + _dot_nt(km_lo, q_hi) + _dot_nt(km_hi, q_lo)
            sel = _topk_select(s_t, qb, axis=0).T
        pieces = []
        for jb in range(qb + 1):
            lj = logits[:, jb * MOBA_BLOCK:(jb + 1) * MOBA_BLOCK] + bias_scr[qb - jb]
            if jb == qb:
                lj = jnp.where(row >= col, lj, NEG_INF)
            elif sel is not None:
                lj = jnp.where(sel[:, jb:jb + 1] > 0.0, lj, NEG_INF)
            pieces.append(lj)
        lg = jnp.concatenate(pieces, axis=1) if len(pieces) > 1 else pieces[0]
        m = jnp.max(lg, axis=-1, keepdims=True)
        p = jnp.exp(lg - m)
        return p.astype(BF16), jnp.sum(p, axis=-1, keepdims=True)

    def pv(qb, p, z):
        o_ref[rows_of(qb), :] = _dot(p, v_ref[0:(qb + 1) * MOBA_BLOCK, :]) / z

    logits = qk(0)
    prev = None
    for qb in range(n_blocks):
        nxt = qk(qb + 1) if qb + 1 < n_blocks else None
        p, z = softmax(qb, logits)
        if prev is not None:
            pv(*prev)
        prev = (qb, p, z)
        logits = nxt
    pv(*prev)


def _attn_prompt(q_hi, q_lo, k_bf, v_bf, k_mean, rev_tab, *, n_seq, seq):
    n_blocks = seq // MOBA_BLOCK
    qkv = pl.BlockSpec((seq, HEAD_DIM), lambda h, n: (n, h))
    return pl.pallas_call(
        _attn_prompt_kernel,
        grid=(N_HEADS_B, n_seq),
        in_specs=[qkv, qkv, qkv, qkv,
                  pl.BlockSpec((None, n_blocks, HEAD_DIM), lambda h, n: (n, 0, h)),
                  pl.BlockSpec((None, n_blocks, 2 * MOBA_BLOCK), lambda h, n: (h, 0, 0))],
        out_specs=pl.BlockSpec((seq, HEAD_DIM), lambda h, n: (n, h)),
        out_shape=jax.ShapeDtypeStruct((n_seq * seq, W_B), F32),
        scratch_shapes=[pltpu.VMEM((n_blocks, MOBA_BLOCK, MOBA_BLOCK), F32)],
        compiler_params=_params(2),
        name="attn_prompt",
    )(q_hi, q_lo, k_bf, v_bf, k_mean, rev_tab)


def _attn_sample_kernel(n_pages, pt_ref, q_ref, kn_ref, vn_ref, bias_ref, biasn_ref, *refs):
    k_pages = refs[:n_pages]
    v_pages = refs[n_pages:2 * n_pages]
    o_ref = refs[2 * n_pages]
    n_tok = q_ref.shape[0]
    rows = n_tok * N_HEADS_B
    cols = PAGE_SIZE * N_HEADS_B
    pages_per_block = MOBA_BLOCK // PAGE_SIZE
    n_blocks = n_pages // pages_per_block

    q = q_ref[...].reshape(rows, HEAD_DIM)
    q_bf = q.astype(BF16)

    logit_pages = []
    k_sums = []
    for p in range(n_pages):
        kp = k_pages[p][...]
        k_sums.append(jnp.sum(kp, axis=0))
        logit_pages.append(_dot_nt(q_bf, kp.reshape(cols, HEAD_DIM).astype(BF16)))

    blk_lane = lax.broadcasted_iota(jnp.int32, (rows, n_blocks), 1)
    scores = jnp.zeros((rows, n_blocks), F32)
    for b in range(n_blocks):
        km = sum(k_sums[b * pages_per_block:(b + 1) * pages_per_block]) * (1.0 / MOBA_BLOCK)
        s_b = jnp.sum(q * jnp.concatenate([km] * n_tok, axis=0), axis=1, keepdims=True)
        scores = jnp.where(blk_lane == b, s_b, scores)
    sel = _topk_select(scores, n_blocks, axis=1)

    masked = []
    for p in range(n_pages):
        b = p // pages_per_block
        lp = logit_pages[p] + bias_ref[:, p * cols:(p + 1) * cols]
        masked.append(jnp.where(sel[:, b:b + 1] > 0.0, lp, NEG_INF))
    ln = _dot_nt(q_bf, kn_ref[...].reshape(rows, HEAD_DIM).astype(BF16)) + biasn_ref[...]

    m = jnp.maximum(jnp.max(functools.reduce(jnp.maximum, masked), axis=-1, keepdims=True),
                    jnp.max(ln, axis=-1, keepdims=True))
    p_new = jnp.exp(ln - m)
    acc = _dot(p_new.astype(BF16), vn_ref[...].reshape(rows, HEAD_DIM).astype(BF16))
    p_sum = None
    for p in range(n_pages):
        pp = jnp.exp(masked[p] - m)
        p_sum = pp if p_sum is None else p_sum + pp
        acc = acc + _dot(pp.astype(BF16), v_pages[p][...].reshape(cols, HEAD_DIM).astype(BF16))
    z = jnp.sum(p_sum, axis=-1, keepdims=True) + jnp.sum(p_new, axis=-1, keepdims=True)
    o_ref[...] = (acc / z).reshape(n_tok, N_HEADS_B, HEAD_DIM)


def _attn_sample(page_table, q, k_new, v_new, bias_past, bias_new, cache_k, cache_v):
    n_seq, n_pages = page_table.shape
    n_tok = q.shape[1]
    tok = pl.BlockSpec((None, n_tok, N_HEADS_B, HEAD_DIM), lambda n, pt: (n, 0, 0, 0))

    def page_spec(p):
        return pl.BlockSpec((None, None, PAGE_SIZE, N_HEADS_B, HEAD_DIM),
                            lambda n, pt: (0, pt[n * n_pages + p], 0, 0, 0))

    grid_spec = pltpu.PrefetchScalarGridSpec(
        num_scalar_prefetch=1,
        grid=(n_seq,),
        in_specs=[tok, tok, tok,
                  pl.BlockSpec(bias_past.shape, lambda n, pt: (0, 0)),
                  pl.BlockSpec(bias_new.shape, lambda n, pt: (0, 0))]
                 + [page_spec(p) for p in range(n_pages)] * 2,
        out_specs=tok,
    )
    return pl.pallas_call(
        functools.partial(_attn_sample_kernel, n_pages),
        grid_spec=grid_spec,
        out_shape=jax.ShapeDtypeStruct((n_seq, n_tok, N_HEADS_B, HEAD_DIM), F32),
        compiler_params=_params(1),
        name="attn_sample",
    )(page_table.reshape(-1), q, k_new, v_new, bias_past, bias_new,
      *([cache_k] * n_pages), *([cache_v] * n_pages))


def _mix_tail(x, gm_n, attn, g1, sc2, sh2, wo_ref, nbg_ref, l1g_ref, l1b_ref, x1_ref, h2_ref):
    a_n = _rms_norm(attn, nbg_ref[...]).astype(BF16)
    o = _dot(gm_n, wo_ref[0:W_A, :]) + _dot(a_n, wo_ref[W_A:W_A + W_B, :])
    x1 = _layer_norm(ALPHA * x + g1 * o, l1g_ref[...], l1b_ref[...])
    x1_ref[...] = x1
    h2_ref[...] = (x1 * (1.0 + sc2) + sh2).astype(BF16)


def _mix_prompt_kernel(x_ref, gm_ref, at_ref, g1_ref, sc2_ref, sh2_ref, wo_ref, nbg_ref, l1g_ref, l1b_ref,
                       x1_ref, h2_ref):
    _mix_tail(x_ref[...], gm_ref[...], at_ref[...], g1_ref[...], sc2_ref[...], sh2_ref[...],
              wo_ref, nbg_ref, l1g_ref, l1b_ref, x1_ref, h2_ref)


def _mix_prompt(x, gm_n, attn, mod_p, w_out, norm_b_g, ln1_g, ln1_b, *, tm, seq):
    rows = x.shape[0]
    tiles_per_seq = seq // tm
    row_blk = lambda i: (i, 0)
    vec = lambda i: (0, 0)
    mod = lambda c: pl.BlockSpec((None, 1, D_MODEL), lambda i: (i // tiles_per_seq, 0, c))
    return pl.pallas_call(
        _mix_prompt_kernel,
        grid=(rows // tm,),
        in_specs=[pl.BlockSpec((tm, D_MODEL), row_blk), pl.BlockSpec((tm, W_A), row_blk),
                  pl.BlockSpec((tm, W_B), row_blk), mod(2), mod(4), mod(3),
                  pl.BlockSpec((D_MODEL, D_MODEL), vec), pl.BlockSpec((1, W_B), vec),
                  pl.BlockSpec((1, D_MODEL), vec), pl.BlockSpec((1, D_MODEL), vec)],
        out_specs=(pl.BlockSpec((tm, D_MODEL), row_blk), pl.BlockSpec((tm, D_MODEL), row_blk)),
        out_shape=(jax.ShapeDtypeStruct((rows, D_MODEL), F32), jax.ShapeDtypeStruct((rows, D_MODEL), BF16)),
        compiler_params=_params(1),
        name="mix_prompt",
    )(x, gm_n, attn, mod_p, mod_p, mod_p, w_out, norm_b_g, ln1_g, ln1_b)


def _mix_sample_kernel(x_ref, u_ref, va_ref, ws_ref, bs_ref, nag_ref, at_ref, g1_ref, sc2_ref, sh2_ref,
                       wo_ref, nbg_ref, l1g_ref, l1b_ref, x1_ref, h2_ref):
    n_tok = ws_ref.shape[0]
    mixed = bs_ref[...]
    for s in range(n_tok):
        mixed = mixed + ws_ref[s:s + 1, :] * va_ref[:, s * W_A:(s + 1) * W_A]
    gm_n = _rms_norm(u_ref[...] * mixed, nag_ref[...]).astype(BF16)
    _mix_tail(x_ref[...], gm_n, at_ref[...], g1_ref[...], sc2_ref[...], sh2_ref[...],
              wo_ref, nbg_ref, l1g_ref, l1b_ref, x1_ref, h2_ref)


def _mix_sample(x2d, u, va, ws_tok, bs_tok, norm_a_g, attn2d, mod_s, w_out, norm_b_g, ln1_g, ln1_b,
                *, n_seq, n_tok):
    tok_blk = lambda t: (0, t)
    vec = lambda t: (0, 0)
    mod = lambda c: pl.BlockSpec((n_seq, D_MODEL), lambda t: (0, c))
    out_blk = pl.BlockSpec((n_seq, D_MODEL), lambda t: (t, 0))
    return pl.pallas_call(
        _mix_sample_kernel,
        grid=(n_tok,),
        in_specs=[pl.BlockSpec((n_seq, D_MODEL), tok_blk), pl.BlockSpec((n_seq, W_A), tok_blk),
                  pl.BlockSpec((n_seq, n_tok * W_A), vec),
                  pl.BlockSpec((None, n_tok, W_A), lambda t: (t, 0, 0)),
                  pl.BlockSpec((None, 1, W_A), lambda t: (t, 0, 0)),
                  pl.BlockSpec((1, W_A), vec),
                  pl.BlockSpec((n_seq, W_B), tok_blk), mod(2), mod(4), mod(3),
                  pl.BlockSpec((D_MODEL, D_MODEL), vec), pl.BlockSpec((1, W_B), vec),
                  pl.BlockSpec((1, D_MODEL), vec), pl.BlockSpec((1, D_MODEL), vec)],
        out_specs=(out_blk, out_blk),
        out_shape=(jax.ShapeDtypeStruct((n_tok * n_seq, D_MODEL), F32),
                   jax.ShapeDtypeStruct((n_tok * n_seq, D_MODEL), BF16)),
        compiler_params=_params(1),
        name="mix_sample",
    )(x2d, u, va, ws_tok, bs_tok, norm_a_g, attn2d, mod_s, mod_s, mod_s, w_out, norm_b_g, ln1_g, ln1_b)


def _ffn_act(conv_a, conv_b):
    return (jax.nn.silu(conv_a) * conv_b).astype(BF16)


def _ffn_prompt_kernel(tiles_per_seq, h_ref, halo_ref, x1_ref, g2_ref, wa_ref, wb_ref, wd_ref, wc_ref, bc_ref,
                       l2g_ref, l2b_ref, y_ref, st_ref, acc_scr, h_scr):
    i = pl.program_id(0)
    j = pl.program_id(1)
    tm = h_ref.shape[0]
    n_halo = halo_ref.shape[0]

    @pl.when(j == 0)
    def _():
        seq_start = (i % tiles_per_seq) == 0
        h_scr[0:n_halo, :] = jnp.where(seq_start, jnp.zeros_like(halo_ref[...]), halo_ref[...])
        h_scr[n_halo:n_halo + tm, :] = h_ref[...]
        acc_scr[...] = jnp.zeros_like(acc_scr)

    n_parts = tm // FFN_ROWS
    assert n_parts * FFN_ROWS == tm

    def up_proj(r, prev):
        if prev is None:
            h = h_scr[0:n_halo + FFN_ROWS, :]
            return _dot(h, wa_ref[...]), _dot(h, wb_ref[...])
        h = h_scr[n_halo + r * FFN_ROWS:n_halo + (r + 1) * FFN_ROWS, :]
        return tuple(jnp.concatenate([p[FFN_ROWS:, :], _dot(h, w_ref[...])], axis=0)
                     for p, w_ref in zip(prev, (wa_ref, wb_ref)))

    def conv(half, up, last):
        if last:
            st_ref[half] = up[n_halo + FFN_ROWS - SUBLANES:n_halo + FFN_ROWS, :]
        out = bc_ref[half:half + 1, :]
        for c in range(CONV_W):
            lag = CONV_W - 1 - c
            out = out + up[n_halo - lag:n_halo - lag + FFN_ROWS, :] * wc_ref[half, c:c + 1, :]
        return out

    def down(r, act):
        acc_scr[r * FFN_ROWS:(r + 1) * FFN_ROWS, :] += _dot(act, wd_ref[...])

    ups = up_proj(0, None)
    for r in range(n_parts):
        nxt = up_proj(r + 1, ups) if r + 1 < n_parts else None
        act = _ffn_act(conv(0, ups[0], r == n_parts - 1), conv(1, ups[1], r == n_parts - 1))
        down(r, act)
        ups = nxt

    @pl.when(j == pl.num_programs(1) - 1)
    def _():
        y_ref[...] = _layer_norm(ALPHA * x1_ref[...] + g2_ref[...] * acc_scr[...], l2g_ref[...], l2b_ref[...])


def _ffn_prompt(h2, x1, mod_p, w_up_a, w_up_b, w_down, w_conv, b_conv, ln2_g, ln2_b, *, tm, seq):
    rows = h2.shape[0]
    tiles_per_seq = seq // tm
    row_blk = lambda i, j: (i, 0)
    vec = lambda i, j: (0, 0)
    halo_blk = lambda i, j: (jnp.maximum(i * (tm // BF16_ROWS) - 1, 0), 0)
    return pl.pallas_call(
        functools.partial(_ffn_prompt_kernel, tiles_per_seq),
        grid=(rows // tm, N_FF_TILES),
        in_specs=[pl.BlockSpec((tm, D_MODEL), row_blk), pl.BlockSpec((BF16_ROWS, D_MODEL), halo_blk),
                  pl.BlockSpec((tm, D_MODEL), row_blk),
                  pl.BlockSpec((None, 1, D_MODEL), lambda i, j: (i // tiles_per_seq, 0, 5)),
                  pl.BlockSpec((D_MODEL, FF_TILE), lambda i, j: (0, j)),
                  pl.BlockSpec((D_MODEL, FF_TILE), lambda i, j: (0, j)),
                  pl.BlockSpec((FF_TILE, D_MODEL), lambda i, j: (j, 0)),
                  pl.BlockSpec((2, CONV_W, FF_TILE), lambda i, j: (0, 0, j)),
                  pl.BlockSpec((2, FF_TILE), lambda i, j: (0, j)),
                  pl.BlockSpec((1, D_MODEL), vec), pl.BlockSpec((1, D_MODEL), vec)],
        out_specs=(pl.BlockSpec((tm, D_MODEL), row_blk),
                   pl.BlockSpec((None, 2, SUBLANES, FF_TILE), lambda i, j: (i, 0, 0, j))),
        out_shape=(jax.ShapeDtypeStruct((rows, D_MODEL), F32),
                   jax.ShapeDtypeStruct((rows // tm, 2, SUBLANES, D_FF_PAD), F32)),
        scratch_shapes=[pltpu.VMEM((tm, D_MODEL), F32), pltpu.VMEM((BF16_ROWS + tm, D_MODEL), BF16)],
        compiler_params=_params(2),
        name="ffn_prompt",
    )(h2, h2, x1, mod_p, w_up_a, w_up_b, w_down, w_conv, b_conv, ln2_g, ln2_b)


def _ffn_sample_kernel(n_tok, h_ref, x1_ref, g2_ref, wa_ref, wb_ref, wd_ref, wc_ref, bc_ref, past_ref,
                       l2g_ref, l2b_ref, y_ref, st_ref, acc_scr):
    j = pl.program_id(0)
    n_seq = g2_ref.shape[0]
    h = h_ref[...]

    def conv(half, w_ref):
        up = _dot(h, w_ref[...])
        full = [past_ref[r, half] for r in range(CONV_W - 1)]
        full += [up[t * n_seq:(t + 1) * n_seq, :] for t in range(n_tok)]
        for r in range(CONV_W - 1):
            st_ref[r, half] = full[n_tok + r]
        w = wc_ref[half]
        b = bc_ref[half:half + 1, :]
        return jnp.concatenate(
            [b + sum(full[t + c] * w[c:c + 1, :] for c in range(CONV_W)) for t in range(n_tok)], axis=0)

    part = _dot(_ffn_act(conv(0, wa_ref), conv(1, wb_ref)), wd_ref[...])

    @pl.when(j == 0)
    def _():
        acc_scr[...] = part

    @pl.when(j > 0)
    def _():
        acc_scr[...] += part

    @pl.when(j == pl.num_programs(0) - 1)
    def _():
        g2 = g2_ref[...]
        for t in range(n_tok):
            rs = slice(t * n_seq, (t + 1) * n_seq)
            y_ref[:, t * D_MODEL:(t + 1) * D_MODEL] = _layer_norm(
                ALPHA * x1_ref[rs, :] + g2 * acc_scr[rs, :], l2g_ref[...], l2b_ref[...])


def _ffn_sample(h2, x1, mod_s, w_up_a, w_up_b, w_down, w_conv, b_conv, past, ln2_g, ln2_b, *, n_seq, n_tok):
    rows = n_seq * n_tok
    full = lambda j: (0, 0)
    state_blk = pl.BlockSpec((CONV_W - 1, 2, n_seq, FF_TILE), lambda j: (0, 0, 0, j))
    return pl.pallas_call(
        functools.partial(_ffn_sample_kernel, n_tok),
        grid=(N_FF_TILES,),
        in_specs=[pl.BlockSpec((rows, D_MODEL), full), pl.BlockSpec((rows, D_MODEL), full),
                  pl.BlockSpec((n_seq, D_MODEL), lambda j: (0, 5)),
                  pl.BlockSpec((D_MODEL, FF_TILE), lambda j: (0, j)),
                  pl.BlockSpec((D_MODEL, FF_TILE), lambda j: (0, j)),
                  pl.BlockSpec((FF_TILE, D_MODEL), lambda j: (j, 0)),
                  pl.BlockSpec((2, CONV_W, FF_TILE), lambda j: (0, 0, j)),
                  pl.BlockSpec((2, FF_TILE), lambda j: (0, j)),
                  state_blk,
                  pl.BlockSpec((1, D_MODEL), full), pl.BlockSpec((1, D_MODEL), full)],
        out_specs=(pl.BlockSpec((n_seq, n_tok * D_MODEL), full), state_blk),
        out_shape=(jax.ShapeDtypeStruct((n_seq, n_tok * D_MODEL), F32),
                   jax.ShapeDtypeStruct((CONV_W - 1, 2, n_seq, D_FF_PAD), F32)),
        scratch_shapes=[pltpu.VMEM((rows, D_MODEL), F32)],
        compiler_params=_params(1),
        name="ffn_sample",
    )(h2, x1, mod_s, w_up_a, w_up_b, w_down, w_conv, b_conv, past, ln2_g, ln2_b)


def _t5_bucket_table(n_rel):
    max_exact = NUM_BUCKETS // 2
    ratio = MAX_DISTANCE // max_exact
    log_ratio = ratio.bit_length() - 1
    assert 1 << log_ratio == ratio
    steps = NUM_BUCKETS - max_exact
    out = np.zeros((n_rel,), np.int32)
    for n in range(n_rel):
        if n < max_exact:
            out[n] = n
            continue
        k = 0
        while k + 1 < steps and n ** steps >= (max_exact ** steps) << (log_ratio * (k + 1)):
            k += 1
        out[n] = min(max_exact + k, NUM_BUCKETS - 1)
    return out


def _bias_tables(rel_bias, seq, past, n_tok):
    n_rel = max(seq, past + n_tok)
    onehot = np.zeros((NUM_BUCKETS, n_rel), np.float32)
    onehot[_t5_bucket_table(n_rel), np.arange(n_rel)] = 1.0
    tab = jnp.dot(rel_bias.astype(F32).T, onehot, precision=lax.Precision.HIGHEST)
    n_blocks = seq // MOBA_BLOCK
    tab_pad = jnp.pad(tab, ((0, 0), (MOBA_BLOCK, 1)))
    rev = jnp.stack([tab_pad[:, d * MOBA_BLOCK + 1:(d + 2) * MOBA_BLOCK + 1][:, ::-1]
                     for d in range(n_blocks)], axis=1)
    same_head = np.eye(N_HEADS_B, dtype=bool)
    by_key = jnp.stack([tab[:, t + 1:t + 1 + past][:, ::-1] for t in range(n_tok)], axis=0)
    n_pages = past // PAGE_SIZE
    spread = np.repeat(np.eye(PAGE_SIZE, dtype=np.float32), N_HEADS_B, axis=1)
    by_col = jnp.einsum('rpk,kc->rpc', by_key.reshape(n_tok * N_HEADS_B, n_pages, PAGE_SIZE), spread,
                        precision=lax.Precision.HIGHEST).reshape(n_tok * N_HEADS_B, past * N_HEADS_B)
    row_head = np.tile(np.arange(N_HEADS_B), n_tok)[:, None]
    col_head = np.tile(np.arange(N_HEADS_B), past)[None, :]
    bias_past = jnp.where(row_head == col_head, by_col, NEG_INF)
    new_rows = []
    for t in range(n_tok):
        by_tok = jnp.stack([tab[:, max(t - s, 0)] for s in range(n_tok)], axis=1)
        ok = same_head[:, None, :] & (np.arange(n_tok) <= t)[None, :, None]
        new_rows.append(jnp.where(ok, by_tok[:, :, None], NEG_INF).reshape(N_HEADS_B, n_tok * N_HEADS_B))
    bias_new = jnp.stack(new_rows, axis=0).reshape(n_tok * N_HEADS_B, n_tok * N_HEADS_B)
    return rev, bias_past, bias_new


def _pad_ff(a, axis):
    pad = [(0, 0)] * a.ndim
    pad[axis] = (0, D_FF_PAD - D_FF)
    return jnp.pad(a, pad)


def kernel(x_prompt, x_sample, cache_k, cache_v, state_ffn_conv, page_table, c_prompt, c_sample,
           w_ada, b_ada, w_in, ln_sgu_g, ln_sgu_b, w_s, b_s, rel_bias, norm_a_g, norm_b_g, w_out,
           ln1_g, ln1_b, w_up, w_conv, b_conv, w_down, ln2_g, ln2_b):
    n_p, seq, _ = x_prompt.shape
    n_s, n_tok, _ = x_sample.shape
    n_pages = page_table.shape[1]
    past = n_pages * PAGE_SIZE
    assert w_ada.shape[0] == 1 and seq % MOBA_BLOCK == 0 and past % MOBA_BLOCK == 0 and n_tok <= CHUNK

    w_in_bf = w_in[0].astype(BF16)
    w_out_bf = w_out[0].astype(BF16)
    w_up_a = _cast_pad_cols(w_up[0], 0, D_FF, D_FF_PAD, FF_TILE)
    w_up_b = _cast_pad_cols(w_up[0], D_FF, D_FF, D_FF_PAD, FF_TILE)
    w_down_bf = _cast_pad_rows(w_down[0], D_FF_PAD, FF_TILE)
    w_conv_p = _pad_ff(jnp.transpose(w_conv[0].reshape(CONV_W, 2, D_FF), (1, 0, 2)), 2)
    b_conv_p = _pad_ff(b_conv[0].reshape(2, D_FF), 1)
    row = lambda a: a.reshape(1, -1)
    ln_g, ln_b = row(ln_sgu_g[0]), row(ln_sgu_b[0])
    nag, nbg = row(norm_a_g[0]), row(norm_b_g[0])
    l1g, l1b, l2g, l2b = row(ln1_g[0]), row(ln1_b[0]), row(ln2_g[0]), row(ln2_b[0])
    bs_full = jnp.broadcast_to(b_s[0][:, :, None], (N_GROUPS_A, CHUNK, HEAD_DIM))
    tril_tok = np.tril(np.ones((n_tok, n_tok), np.float32))
    ws_tok = jnp.repeat(jnp.transpose(w_s[0][:, :n_tok, :n_tok] * tril_tok, (1, 2, 0)), HEAD_DIM, axis=2)
    bs_tok = jnp.repeat(b_s[0][:, :n_tok].T, HEAD_DIM, axis=1).reshape(n_tok, 1, W_A)
    rev_tab, bias_past, bias_new = _bias_tables(rel_bias, seq, past, n_tok)

    pad_rows = (-(n_s + n_p)) % SUBLANES
    c_all = jnp.concatenate([c_sample, c_prompt, jnp.zeros((pad_rows, D_MODEL), F32)], axis=0)
    mod = _ada(c_all, w_ada[0], b_ada[0])
    mod_s = mod[:n_s]
    mod_p = mod[n_s:n_s + n_p].reshape(n_p, 1, 6 * D_MODEL)

    xp = x_prompt.reshape(n_p * seq, D_MODEL)
    q_hi, q_lo, k_p, v_p, k_bf, v_bf, k_mean, gm_n = _proj_prompt(
        xp, mod_p, w_in_bf, ln_g, ln_b, w_s[0], bs_full, nag, tm=256, seq=seq)
    attn_p = _attn_prompt(q_hi, q_lo, k_bf, v_bf, k_mean.reshape(n_p, seq // MOBA_BLOCK, W_B), rev_tab,
                          n_seq=n_p, seq=seq)
    x1_p, h2_p = _mix_prompt(xp, gm_n, attn_p, mod_p, w_out_bf, nbg, l1g, l1b, tm=512, seq=seq)
    y_p, st_p = _ffn_prompt(h2_p, x1_p, mod_p, w_up_a, w_up_b, w_down_bf, w_conv_p, b_conv_p, l2g, l2b,
                            tm=512, seq=seq)
    tiles_per_seq = st_p.shape[0] // n_p
    conv_p = st_p[tiles_per_seq - 1::tiles_per_seq, :, SUBLANES - (CONV_W - 1):, :D_FF]
    conv_p = jnp.transpose(conv_p, (0, 2, 1, 3)).reshape(1, n_p, CONV_W - 1, 2 * D_FF)

    xs = x_sample.reshape(n_s, n_tok * D_MODEL)
    q_s, k_s, v_s, u_s, va_s = _proj_sample(xs, mod_s, w_in_bf, ln_g, ln_b, n_seq=n_s, n_tok=n_tok)
    heads = lambda a: a.reshape(n_s, n_tok, N_HEADS_B, HEAD_DIM)
    attn_s = _attn_sample(page_table, heads(q_s), heads(k_s), heads(v_s), bias_past, bias_new, cache_k, cache_v)
    x1_s, h2_s = _mix_sample(xs, u_s, va_s, ws_tok, bs_tok, nag, attn_s.reshape(n_s, n_tok * W_B), mod_s,
                             w_out_bf, nbg, l1g, l1b, n_seq=n_s, n_tok=n_tok)
    past_up = state_ffn_conv[0].reshape(n_s, CONV_W - 1, 2, D_FF)
    past_up = _pad_ff(jnp.transpose(past_up, (1, 2, 0, 3)), 3)
    y_s, st_s = _ffn_sample(h2_s, x1_s, mod_s, w_up_a, w_up_b, w_down_bf, w_conv_p, b_conv_p, past_up,
                            l2g, l2b, n_seq=n_s, n_tok=n_tok)
    conv_s = jnp.transpose(st_s[..., :D_FF], (2, 0, 1, 3)).reshape(1, n_s, CONV_W - 1, 2 * D_FF)

    heads_p = (1, n_p, seq, N_HEADS_B, HEAD_DIM)
    heads_s = (1, n_s, n_tok, N_HEADS_B, HEAD_DIM)
    return (y_p.reshape(n_p, seq, D_MODEL), y_s.reshape(n_s, n_tok, D_MODEL),
            k_p.reshape(heads_p), v_p.reshape(heads_p), conv_p,
            k_s.reshape(heads_s), v_s.reshape(heads_s), va_s.reshape(1, n_s, n_tok, W_A), conv_s)
```

```python
import functools

import numpy as np
import jax
import jax.numpy as jnp
from jax import lax
from jax.experimental import pallas as pl
from jax.experimental.pallas import tpu as pltpu

F32 = jnp.float32
BF16 = jnp.bfloat16

D_MODEL = 2048
HEAD_DIM = 128
N_HEADS_B = 8
N_GROUPS_A = 8
W_B = N_HEADS_B * HEAD_DIM
W_A = N_GROUPS_A * HEAD_DIM
CHUNK = 128
MOBA_BLOCK = 256
MOBA_TOPK = 3
NUM_BUCKETS = 32
MAX_DISTANCE = 1024
CONV_W = 3
D_FF = 5504
PAGE_SIZE = 128
ALPHA = 2.0 ** 0.25
LN_EPS = 1e-5
NEG_INF = -1e30
LOG2_E = 1.4426950408889634

FF_TILE = 512
D_FF_PAD = ((D_FF + FF_TILE - 1) // FF_TILE) * FF_TILE
N_FF_TILES = D_FF_PAD // FF_TILE
SUBLANES = 8
BF16_ROWS = 16
FFN_ROWS = 256
MIX_ROWS = 256
VMEM_LIMIT_BYTES = 56 * 1024 * 1024

_NT = (((1,), (1,)), ((), ()))


def _params(n_axes, flags=None):
    return pltpu.CompilerParams(dimension_semantics=("arbitrary",) * n_axes,
                                vmem_limit_bytes=VMEM_LIMIT_BYTES, flags=flags)


def _layer_norm(x, g, b):
    mu = jnp.mean(x, axis=-1, keepdims=True)
    xc = x - mu
    var = jnp.mean(xc * xc, axis=-1, keepdims=True)
    return xc * lax.rsqrt(var + LN_EPS) * g + b


def _rms_norm(x, g):
    return x * lax.rsqrt(jnp.mean(x * x, axis=-1, keepdims=True) + LN_EPS) * g


def _split_bf16(x):
    hi = x.astype(BF16)
    lo = (x - hi.astype(F32)).astype(BF16)
    return hi, lo


def _dot(a, b):
    return jnp.dot(a, b, preferred_element_type=F32)


def _dot_nt(a, b):
    return lax.dot_general(a, b, _NT, preferred_element_type=F32)


def _topk_select(s, n_valid, axis):
    idx = lax.broadcasted_iota(jnp.int32, s.shape, axis)
    rank = jnp.zeros(s.shape, F32)
    for i in range(n_valid):
        si = lax.slice_in_dim(s, i, i + 1, axis=axis)
        beats = (si > s) | ((si == s) & (i < idx))
        rank = rank + beats.astype(F32)
    return ((rank < MOBA_TOPK) & (idx < n_valid)).astype(F32)


def _cast_pad_kernel(n_valid, axis, *refs):
    *in_refs, o_ref = refs
    x = in_refs[0][...] if len(in_refs) == 1 else jnp.concatenate([r[...] for r in in_refs], axis=1)
    pos = pl.program_id(0) * o_ref.shape[axis] + lax.broadcasted_iota(jnp.int32, o_ref.shape, axis)
    o_ref[...] = jnp.where(pos < n_valid, x, 0.0).astype(BF16)


def _cast_pad_cols(w, col0, n_valid, n_out, tile):
    rows, n_cols = w.shape
    lanes = 128
    assert col0 % lanes == 0 and n_cols % lanes == 0 and tile % lanes == 0 and n_out % tile == 0
    last = n_cols // lanes - 1
    pieces = tile // lanes

    def piece(q):
        return pl.BlockSpec((rows, lanes), lambda j: (0, jnp.minimum(col0 // lanes + j * pieces + q, last)))

    return pl.pallas_call(
        functools.partial(_cast_pad_kernel, n_valid, 1),
        grid=(n_out // tile,),
        in_specs=[piece(q) for q in range(pieces)],
        out_specs=pl.BlockSpec((rows, tile), lambda j: (0, j)),
        out_shape=jax.ShapeDtypeStruct((rows, n_out), BF16),
        compiler_params=_params(1),
        name="cast_pad_cols",
    )(*([w] * pieces))


def _cast_pad_rows(w, n_out, tile):
    n_rows, cols = w.shape
    assert n_out % tile == 0 and (n_out - tile) < n_rows
    return pl.pallas_call(
        functools.partial(_cast_pad_kernel, n_rows, 0),
        grid=(n_out // tile,),
        in_specs=[pl.BlockSpec((tile, cols), lambda j: (j, 0))],
        out_specs=pl.BlockSpec((tile, cols), lambda j: (j, 0)),
        out_shape=jax.ShapeDtypeStruct((n_out, cols), BF16),
        compiler_params=_params(1),
        name="cast_pad_rows",
    )(w)


def _ada_kernel(c_ref, w_ref, b_ref, o_ref):
    c = c_ref[...]
    a = jax.nn.silu(c).astype(BF16)
    o_ref[...] = _dot(a, w_ref[...].astype(BF16)) + b_ref[...]


def _ada(c_all, w_ada, b_ada):
    rows = c_all.shape[0]
    n_out = w_ada.shape[1]
    tn = 512
    return pl.pallas_call(
        _ada_kernel,
        grid=(n_out // tn,),
        in_specs=[pl.BlockSpec((rows, D_MODEL), lambda j: (0, 0)),
                  pl.BlockSpec((D_MODEL, tn), lambda j: (0, j)),
                  pl.BlockSpec((1, tn), lambda j: (0, j))],
        out_specs=pl.BlockSpec((rows, tn), lambda j: (0, j)),
        out_shape=jax.ShapeDtypeStruct((rows, n_out), F32),
        compiler_params=_params(1),
        name="ada",
    )(c_all, w_ada, b_ada.reshape(1, n_out))


_COL_Q, _COL_K, _COL_V, _COL_U, _COL_VA = range(5)


def _proj_prompt_kernel(x_ref, sc_ref, sh_ref, w_ref, lng_ref, lnb_ref, ws_ref, bs_ref, nag_ref,
                        qhi_ref, qlo_ref, k_ref, v_ref, kbf_ref, vbf_ref, km_ref, gm_ref):
    tm = x_ref.shape[0]
    h = (x_ref[...] * (1.0 + sc_ref[...]) + sh_ref[...]).astype(BF16)

    def proj(c):
        return _dot(h, w_ref[:, c * W_B:(c + 1) * W_B])

    def gmlp(u, acc_va):
        va = _layer_norm(jax.nn.gelu(acc_va), lng_ref[...], lnb_ref[...]).astype(BF16)
        n_chunks = tm // CHUNK
        row = lax.broadcasted_iota(jnp.int32, (CHUNK, CHUNK), 0)
        col = lax.broadcasted_iota(jnp.int32, (CHUNK, CHUNK), 1)
        groups = []
        for g in range(N_GROUPS_A):
            gs = slice(g * HEAD_DIM, (g + 1) * HEAD_DIM)
            w_g = jnp.where(row >= col, ws_ref[g], 0.0).astype(BF16)
            v_g = jnp.concatenate([va[c * CHUNK:(c + 1) * CHUNK, gs] for c in range(n_chunks)], axis=1)
            m_g = _dot(w_g, v_g)
            mixed = jnp.concatenate([m_g[:, c * HEAD_DIM:(c + 1) * HEAD_DIM] + bs_ref[g]
                                     for c in range(n_chunks)], axis=0)
            groups.append(u[:, gs] * mixed)
        gm_ref[...] = _rms_norm(jnp.concatenate(groups, axis=1), nag_ref[...]).astype(BF16)

    def put_q(acc):
        q_hi, q_lo = _split_bf16(acc * (LOG2_E * HEAD_DIM ** -0.5))
        qhi_ref[...] = q_hi
        qlo_ref[...] = q_lo

    def put_k(acc):
        k_ref[...] = acc
        kbf_ref[...] = acc.astype(BF16)
        km_ref[...] = jnp.mean(acc.reshape(tm // MOBA_BLOCK, MOBA_BLOCK, W_B), axis=1)

    def put_v(acc):
        v_ref[...] = acc
        vbf_ref[...] = acc.astype(BF16)

    acc_u = proj(_COL_U)
    acc_va = proj(_COL_VA)
    u = jax.nn.gelu(acc_u)
    acc_q = proj(_COL_Q)
    gmlp(u, acc_va)
    acc_k = proj(_COL_K)
    put_q(acc_q)
    acc_v = proj(_COL_V)
    put_k(acc_k)
    put_v(acc_v)


def _proj_prompt(x, mod_p, w_in, ln_g, ln_b, w_s, bs_full, norm_a_g, *, tm, seq):
    rows = x.shape[0]
    tiles_per_seq = seq // tm
    row_blk = lambda i: (i, 0)
    vec = lambda i: (0, 0)
    whole3 = lambda i: (0, 0, 0)
    out_shape = (
        jax.ShapeDtypeStruct((rows, W_B), BF16),
        jax.ShapeDtypeStruct((rows, W_B), BF16),
        jax.ShapeDtypeStruct((rows, W_B), F32),
        jax.ShapeDtypeStruct((rows, W_B), F32),
        jax.ShapeDtypeStruct((rows, W_B), BF16),
        jax.ShapeDtypeStruct((rows, W_B), BF16),
        jax.ShapeDtypeStruct((rows // tm, tm // MOBA_BLOCK, W_B), F32),
        jax.ShapeDtypeStruct((rows, W_A), BF16),
    )
    out_specs = (
        pl.BlockSpec((tm, W_B), row_blk), pl.BlockSpec((tm, W_B), row_blk),
        pl.BlockSpec((tm, W_B), row_blk), pl.BlockSpec((tm, W_B), row_blk),
        pl.BlockSpec((tm, W_B), row_blk), pl.BlockSpec((tm, W_B), row_blk),
        pl.BlockSpec((None, tm // MOBA_BLOCK, W_B), lambda i: (i, 0, 0)),
        pl.BlockSpec((tm, W_A), row_blk),
    )
    return pl.pallas_call(
        _proj_prompt_kernel,
        grid=(rows // tm,),
        in_specs=[pl.BlockSpec((tm, D_MODEL), row_blk),
                  pl.BlockSpec((None, 1, D_MODEL), lambda i: (i // tiles_per_seq, 0, 1)),
                  pl.BlockSpec((None, 1, D_MODEL), lambda i: (i // tiles_per_seq, 0, 0)),
                  pl.BlockSpec(w_in.shape, vec, pipeline_mode=pl.Buffered(1)),
                  pl.BlockSpec((1, W_A), vec), pl.BlockSpec((1, W_A), vec),
                  pl.BlockSpec((N_GROUPS_A, CHUNK, CHUNK), whole3),
                  pl.BlockSpec((N_GROUPS_A, CHUNK, HEAD_DIM), whole3),
                  pl.BlockSpec((1, W_A), vec)],
        out_specs=out_specs,
        out_shape=out_shape,
        compiler_params=_params(1),
        name="proj_prompt",
    )(x, mod_p, mod_p, w_in, ln_g, ln_b, w_s, bs_full, norm_a_g)


def _proj_sample_kernel(n_tok, x_ref, sc_ref, sh_ref, w_ref, lng_ref, lnb_ref,
                        q_ref, k_ref, v_ref, u_ref, va_ref, h_scr):
    j = pl.program_id(0)
    n_seq = sc_ref.shape[0]

    @pl.when(j == 0)
    def _():
        for t in range(n_tok):
            x_t = x_ref[:, t * D_MODEL:(t + 1) * D_MODEL]
            h_scr[t * n_seq:(t + 1) * n_seq, :] = (x_t * (1.0 + sc_ref[...]) + sh_ref[...]).astype(BF16)

    acc = _dot(h_scr[...], w_ref[...])

    def put(o_ref, val):
        for t in range(n_tok):
            o_ref[:, t * W_B:(t + 1) * W_B] = val[t * n_seq:(t + 1) * n_seq, :]

    @pl.when(j == _COL_Q)
    def _():
        put(q_ref, acc * (HEAD_DIM ** -0.5))

    @pl.when(j == _COL_K)
    def _():
        put(k_ref, acc)

    @pl.when(j == _COL_V)
    def _():
        put(v_ref, acc)

    @pl.when(j == _COL_U)
    def _():
        put(u_ref, jax.nn.gelu(acc))

    @pl.when(j == _COL_VA)
    def _():
        put(va_ref, _layer_norm(jax.nn.gelu(acc), lng_ref[...], lnb_ref[...]))


def _proj_sample(x2d, mod_s, w_in, ln_g, ln_b, *, n_seq, n_tok):
    n_col = w_in.shape[1] // W_B
    whole = lambda j: (0, 0)
    out = jax.ShapeDtypeStruct((n_seq, n_tok * W_B), F32)
    return pl.pallas_call(
        functools.partial(_proj_sample_kernel, n_tok),
        grid=(n_col,),
        in_specs=[pl.BlockSpec((n_seq, n_tok * D_MODEL), whole),
                  pl.BlockSpec((n_seq, D_MODEL), lambda j: (0, 1)),
                  pl.BlockSpec((n_seq, D_MODEL), lambda j: (0, 0)),
                  pl.BlockSpec((D_MODEL, W_B), lambda j: (0, j)),
                  pl.BlockSpec((1, W_A), whole), pl.BlockSpec((1, W_A), whole)],
        out_specs=tuple(pl.BlockSpec((n_seq, n_tok * W_B), whole) for _ in range(5)),
        out_shape=(out,) * 5,
        scratch_shapes=[pltpu.VMEM((n_tok * n_seq, D_MODEL), BF16)],
        compiler_params=_params(1),
        name="proj_sample",
    )(x2d, mod_s, mod_s, w_in, ln_g, ln_b)


def _attn_prompt_kernel(qhi_ref, qlo_ref, k_ref, v_ref, km_ref, rev_ref, o_ref, bias_scr):
    n_blocks = k_ref.shape[0] // MOBA_BLOCK

    @pl.when(pl.program_id(1) == 0)
    def _():
        for d in range(n_blocks):
            r = jnp.broadcast_to(rev_ref[d:d + 1, :], (MOBA_BLOCK, 2 * MOBA_BLOCK))
            rolled = pltpu.roll(r, 0, 1, stride=1, stride_axis=0)
            bias_scr[d] = rolled[:, MOBA_BLOCK:]

    km_hi, km_lo = _split_bf16(km_ref[...])
    row = lax.broadcasted_iota(jnp.int32, (MOBA_BLOCK, MOBA_BLOCK), 0)
    col = lax.broadcasted_iota(jnp.int32, (MOBA_BLOCK, MOBA_BLOCK), 1)

    def rows_of(qb):
        return slice(qb * MOBA_BLOCK, (qb + 1) * MOBA_BLOCK)

    def qk(qb):
        return _dot_nt(qhi_ref[rows_of(qb), :], k_ref[0:(qb + 1) * MOBA_BLOCK, :])

    def softmax(qb, logits):
        sel = None
        if qb > MOBA_TOPK:
            q_hi, q_lo = qhi_ref[rows_of(qb), :], qlo_ref[rows_of(qb), :]
            s_t = _dot_nt(km_hi, q_hi) + _dot_nt(km_lo, q_hi) + _dot_nt(km_hi, q_lo)
            sel = _topk_select(s_t, qb, axis=0).T
        pieces = []
        for jb in range(qb + 1):
            lj = logits[:, jb * MOBA_BLOCK:(jb + 1) * MOBA_BLOCK] + bias_scr[qb - jb]
            if jb == qb:
                lj = jnp.where(row >= col, lj, NEG_INF)
            elif sel is not None:
                lj = jnp.where(sel[:, jb:jb + 1] > 0.0, lj, NEG_INF)
            pieces.append(lj)
        lg = jnp.concatenate(pieces, axis=1) if len(pieces) > 1 else pieces[0]
        m = jnp.max(lg, axis=-1, keepdims=True)
        p = jnp.exp2(lg - m)
        return p.astype(BF16), jnp.sum(p, axis=-1, keepdims=True)

    def pv(qb, p, z):
        o_ref[rows_of(qb), :] = _dot(p, v_ref[0:(qb + 1) * MOBA_BLOCK, :]) / z

    logits = qk(0)
    prev = None
    for qb in range(n_blocks):
        nxt = qk(qb + 1) if qb + 1 < n_blocks else None
        p, z = softmax(qb, logits)
        if prev is not None:
            pv(*prev)
        prev = (qb, p, z)
        logits = nxt
    pv(*prev)


def _attn_prompt(q_hi, q_lo, k_bf, v_bf, k_mean, rev_tab, *, n_seq, seq):
    n_blocks = seq // MOBA_BLOCK
    qkv = pl.BlockSpec((seq, HEAD_DIM), lambda h, n: (n, h))
    return pl.pallas_call(
        _attn_prompt_kernel,
        grid=(N_HEADS_B, n_seq),
        in_specs=[qkv, qkv, qkv, qkv,
                  pl.BlockSpec((None, n_blocks, HEAD_DIM), lambda h, n: (n, 0, h)),
                  pl.BlockSpec((None, n_blocks, 2 * MOBA_BLOCK), lambda h, n: (h, 0, 0))],
        out_specs=pl.BlockSpec((seq, HEAD_DIM), lambda h, n: (n, h)),
        out_shape=jax.ShapeDtypeStruct((n_seq * seq, W_B), F32),
        scratch_shapes=[pltpu.VMEM((n_blocks, MOBA_BLOCK, MOBA_BLOCK), F32)],
        compiler_params=_params(2),
        name="attn_prompt",
    )(q_hi, q_lo, k_bf, v_bf, k_mean, rev_tab)


def _attn_sample_kernel(n_pages, pt_ref, q_ref, kn_ref, vn_ref, bias_ref, biasn_ref, *refs):
    k_pages = refs[:n_pages]
    v_pages = refs[n_pages:2 * n_pages]
    o_ref = refs[2 * n_pages]
    n_tok = q_ref.shape[0]
    rows = n_tok * N_HEADS_B
    cols = PAGE_SIZE * N_HEADS_B
    pages_per_block = MOBA_BLOCK // PAGE_SIZE
    n_blocks = n_pages // pages_per_block

    q = q_ref[...].reshape(rows, HEAD_DIM)
    q_bf = q.astype(BF16)

    logit_pages = []
    k_sums = []
    for p in range(n_pages):
        kp = k_pages[p][...]
        k_sums.append(jnp.sum(kp, axis=0))
        logit_pages.append(_dot_nt(q_bf, kp.reshape(cols, HEAD_DIM).astype(BF16)))

    blk_lane = lax.broadcasted_iota(jnp.int32, (rows, n_blocks), 1)
    scores = jnp.zeros((rows, n_blocks), F32)
    for b in range(n_blocks):
        km = sum(k_sums[b * pages_per_block:(b + 1) * pages_per_block]) * (1.0 / MOBA_BLOCK)
        s_b = jnp.sum(q * jnp.concatenate([km] * n_tok, axis=0), axis=1, keepdims=True)
        scores = jnp.where(blk_lane == b, s_b, scores)
    sel = _topk_select(scores, n_blocks, axis=1)

    masked = []
    for p in range(n_pages):
        b = p // pages_per_block
        lp = logit_pages[p] + bias_ref[:, p * cols:(p + 1) * cols]
        masked.append(jnp.where(sel[:, b:b + 1] > 0.0, lp, NEG_INF))
    ln = _dot_nt(q_bf, kn_ref[...].reshape(rows, HEAD_DIM).astype(BF16)) + biasn_ref[...]

    m = jnp.maximum(jnp.max(functools.reduce(jnp.maximum, masked), axis=-1, keepdims=True),
                    jnp.max(ln, axis=-1, keepdims=True))
    p_new = jnp.exp(ln - m)
    acc = _dot(p_new.astype(BF16), vn_ref[...].reshape(rows, HEAD_DIM).astype(BF16))
    p_sum = None
    for p in range(n_pages):
        pp = jnp.exp(masked[p] - m)
        p_sum = pp if p_sum is None else p_sum + pp
        acc = acc + _dot(pp.astype(BF16), v_pages[p][...].reshape(cols, HEAD_DIM).astype(BF16))
    z = jnp.sum(p_sum, axis=-1, keepdims=True) + jnp.sum(p_new, axis=-1, keepdims=True)
    o_ref[...] = (acc / z).reshape(n_tok, N_HEADS_B, HEAD_DIM)


def _attn_sample(page_table, q, k_new, v_new, bias_past, bias_new, cache_k, cache_v):
    n_seq, n_pages = page_table.shape
    n_tok = q.shape[1]
    tok = pl.BlockSpec((None, n_tok, N_HEADS_B, HEAD_DIM), lambda n, pt: (n, 0, 0, 0))

    def page_spec(p):
        return pl.BlockSpec((None, None, PAGE_SIZE, N_HEADS_B, HEAD_DIM),
                            lambda n, pt: (0, pt[n * n_pages + p], 0, 0, 0))

    grid_spec = pltpu.PrefetchScalarGridSpec(
        num_scalar_prefetch=1,
        grid=(n_seq,),
        in_specs=[tok, tok, tok,
                  pl.BlockSpec(bias_past.shape, lambda n, pt: (0, 0)),
                  pl.BlockSpec(bias_new.shape, lambda n, pt: (0, 0))]
                 + [page_spec(p) for p in range(n_pages)] * 2,
        out_specs=tok,
    )
    return pl.pallas_call(
        functools.partial(_attn_sample_kernel, n_pages),
        grid_spec=grid_spec,
        out_shape=jax.ShapeDtypeStruct((n_seq, n_tok, N_HEADS_B, HEAD_DIM), F32),
        compiler_params=_params(1),
        name="attn_sample",
    )(page_table.reshape(-1), q, k_new, v_new, bias_past, bias_new,
      *([cache_k] * n_pages), *([cache_v] * n_pages))


def _mix_project(gm_n, attn, wo_ref, nbg_ref):
    a_n = _rms_norm(attn, nbg_ref[...]).astype(BF16)
    return _dot(gm_n, wo_ref[0:W_A, :]) + _dot(a_n, wo_ref[W_A:W_A + W_B, :])


def _mix_finish(x, o, g1, sc2, sh2, l1g_ref, l1b_ref):
    x1 = _layer_norm(ALPHA * x + g1 * o, l1g_ref[...], l1b_ref[...])
    return x1, (x1 * (1.0 + sc2) + sh2).astype(BF16)


def _mix_prompt_kernel(x_ref, gm_ref, at_ref, g1_ref, sc2_ref, sh2_ref, wo_ref, nbg_ref, l1g_ref, l1b_ref,
                       x1_ref, h2_ref):
    tm = x_ref.shape[0]
    parts = [slice(r, r + MIX_ROWS) for r in range(0, tm, MIX_ROWS)]
    outs = [_mix_project(gm_ref[rs, :], at_ref[rs, :], wo_ref, nbg_ref) for rs in parts]
    for rs, o in zip(parts, outs):
        x1_ref[rs, :], h2_ref[rs, :] = _mix_finish(x_ref[rs, :], o, g1_ref[...], sc2_ref[...], sh2_ref[...],
                                                   l1g_ref, l1b_ref)


def _mix_prompt(x, gm_n, attn, mod_p, w_out, norm_b_g, ln1_g, ln1_b, *, tm, seq):
    rows = x.shape[0]
    tiles_per_seq = seq // tm
    row_blk = lambda i: (i, 0)
    vec = lambda i: (0, 0)
    mod = lambda c: pl.BlockSpec((None, 1, D_MODEL), lambda i: (i // tiles_per_seq, 0, c))
    return pl.pallas_call(
        _mix_prompt_kernel,
        grid=(rows // tm,),
        in_specs=[pl.BlockSpec((tm, D_MODEL), row_blk), pl.BlockSpec((tm, W_A), row_blk),
                  pl.BlockSpec((tm, W_B), row_blk), mod(2), mod(4), mod(3),
                  pl.BlockSpec((D_MODEL, D_MODEL), vec), pl.BlockSpec((1, W_B), vec),
                  pl.BlockSpec((1, D_MODEL), vec), pl.BlockSpec((1, D_MODEL), vec)],
        out_specs=(pl.BlockSpec((tm, D_MODEL), row_blk), pl.BlockSpec((tm, D_MODEL), row_blk)),
        out_shape=(jax.ShapeDtypeStruct((rows, D_MODEL), F32), jax.ShapeDtypeStruct((rows, D_MODEL), BF16)),
        compiler_params=_params(1),
        name="mix_prompt",
    )(x, gm_n, attn, mod_p, mod_p, mod_p, w_out, norm_b_g, ln1_g, ln1_b)


def _mix_sample_kernel(x_ref, u_ref, va_ref, ws_ref, bs_ref, nag_ref, at_ref, g1_ref, sc2_ref, sh2_ref,
                       wo_ref, nbg_ref, l1g_ref, l1b_ref, x1_ref, h2_ref):
    n_tok = ws_ref.shape[0]
    mixed = bs_ref[...]
    for s in range(n_tok):
        mixed = mixed + ws_ref[s:s + 1, :] * va_ref[:, s * W_A:(s + 1) * W_A]
    gm_n = _rms_norm(u_ref[...] * mixed, nag_ref[...]).astype(BF16)
    o = _mix_project(gm_n, at_ref[...], wo_ref, nbg_ref)
    x1_ref[...], h2_ref[...] = _mix_finish(x_ref[...], o, g1_ref[...], sc2_ref[...], sh2_ref[...],
                                           l1g_ref, l1b_ref)


def _mix_sample(x2d, u, va, ws_tok, bs_tok, norm_a_g, attn2d, mod_s, w_out, norm_b_g, ln1_g, ln1_b,
                *, n_seq, n_tok):
    tok_blk = lambda t: (0, t)
    vec = lambda t: (0, 0)
    mod = lambda c: pl.BlockSpec((n_seq, D_MODEL), lambda t: (0, c))
    out_blk = pl.BlockSpec((n_seq, D_MODEL), lambda t: (t, 0))
    return pl.pallas_call(
        _mix_sample_kernel,
        grid=(n_tok,),
        in_specs=[pl.BlockSpec((n_seq, D_MODEL), tok_blk), pl.BlockSpec((n_seq, W_A), tok_blk),
                  pl.BlockSpec((n_seq, n_tok * W_A), vec),
                  pl.BlockSpec((None, n_tok, W_A), lambda t: (t, 0, 0)),
                  pl.BlockSpec((None, 1, W_A), lambda t: (t, 0, 0)),
                  pl.BlockSpec((1, W_A), vec),
                  pl.BlockSpec((n_seq, W_B), tok_blk), mod(2), mod(4), mod(3),
                  pl.BlockSpec((D_MODEL, D_MODEL), vec), pl.BlockSpec((1, W_B), vec),
                  pl.BlockSpec((1, D_MODEL), vec), pl.BlockSpec((1, D_MODEL), vec)],
        out_specs=(out_blk, out_blk),
        out_shape=(jax.ShapeDtypeStruct((n_tok * n_seq, D_MODEL), F32),
                   jax.ShapeDtypeStruct((n_tok * n_seq, D_MODEL), BF16)),
        compiler_params=_params(1),
        name="mix_sample",
    )(x2d, u, va, ws_tok, bs_tok, norm_a_g, attn2d, mod_s, mod_s, mod_s, w_out, norm_b_g, ln1_g, ln1_b)


def _ffn_act(conv_a, conv_b):
    return (jax.nn.silu(conv_a) * conv_b).astype(BF16)


def _ffn_prompt_kernel(tiles_per_seq, h_ref, halo_ref, x1_ref, g2_ref, wa_ref, wb_ref, wd_ref, wc_ref, bc_ref,
                       l2g_ref, l2b_ref, y_ref, st_ref, acc_scr, h_scr):
    i = pl.program_id(0)
    j = pl.program_id(1)
    tm = h_ref.shape[0]
    n_halo = halo_ref.shape[0]

    @pl.when(j == 0)
    def _():
        seq_start = (i % tiles_per_seq) == 0
        h_scr[0:n_halo, :] = jnp.where(seq_start, jnp.zeros_like(halo_ref[...]), halo_ref[...])
        h_scr[n_halo:n_halo + tm, :] = h_ref[...]
        acc_scr[...] = jnp.zeros_like(acc_scr)

    n_parts = tm // FFN_ROWS
    assert n_parts * FFN_ROWS == tm

    def up_proj(r, prev):
        if prev is None:
            h = h_scr[0:n_halo + FFN_ROWS, :]
            return _dot(h, wa_ref[...]), _dot(h, wb_ref[...])
        h = h_scr[n_halo + r * FFN_ROWS:n_halo + (r + 1) * FFN_ROWS, :]
        return tuple(jnp.concatenate([p[FFN_ROWS:, :], _dot(h, w_ref[...])], axis=0)
                     for p, w_ref in zip(prev, (wa_ref, wb_ref)))

    def conv(half, up, last):
        if last:
            st_ref[half] = up[n_halo + FFN_ROWS - SUBLANES:n_halo + FFN_ROWS, :]
        out = bc_ref[half:half + 1, :]
        for c in range(CONV_W):
            lag = CONV_W - 1 - c
            out = out + up[n_halo - lag:n_halo - lag + FFN_ROWS, :] * wc_ref[half, c:c + 1, :]
        return out

    def down(r, act):
        acc_scr[r * FFN_ROWS:(r + 1) * FFN_ROWS, :] += _dot(act, wd_ref[...])

    ups = up_proj(0, None)
    for r in range(n_parts):
        nxt = up_proj(r + 1, ups) if r + 1 < n_parts else None
        act = _ffn_act(conv(0, ups[0], r == n_parts - 1), conv(1, ups[1], r == n_parts - 1))
        down(r, act)
        ups = nxt

    @pl.when(j == pl.num_programs(1) - 1)
    def _():
        y_ref[...] = _layer_norm(ALPHA * x1_ref[...] + g2_ref[...] * acc_scr[...], l2g_ref[...], l2b_ref[...])


def _ffn_prompt(h2, x1, mod_p, w_up_a, w_up_b, w_down, w_conv, b_conv, ln2_g, ln2_b, *, tm, seq):
    rows = h2.shape[0]
    tiles_per_seq = seq // tm
    row_blk = lambda i, j: (i, 0)
    vec = lambda i, j: (0, 0)
    halo_blk = lambda i, j: (jnp.maximum(i * (tm // BF16_ROWS) - 1, 0), 0)
    return pl.pallas_call(
        functools.partial(_ffn_prompt_kernel, tiles_per_seq),
        grid=(rows // tm, N_FF_TILES),
        in_specs=[pl.BlockSpec((tm, D_MODEL), row_blk), pl.BlockSpec((BF16_ROWS, D_MODEL), halo_blk),
                  pl.BlockSpec((tm, D_MODEL), row_blk),
                  pl.BlockSpec((None, 1, D_MODEL), lambda i, j: (i // tiles_per_seq, 0, 5)),
                  pl.BlockSpec((D_MODEL, FF_TILE), lambda i, j: (0, j)),
                  pl.BlockSpec((D_MODEL, FF_TILE), lambda i, j: (0, j)),
                  pl.BlockSpec((FF_TILE, D_MODEL), lambda i, j: (j, 0)),
                  pl.BlockSpec((2, CONV_W, FF_TILE), lambda i, j: (0, 0, j)),
                  pl.BlockSpec((2, FF_TILE), lambda i, j: (0, j)),
                  pl.BlockSpec((1, D_MODEL), vec), pl.BlockSpec((1, D_MODEL), vec)],
        out_specs=(pl.BlockSpec((tm, D_MODEL), row_blk),
                   pl.BlockSpec((None, 2, SUBLANES, FF_TILE), lambda i, j: (i, 0, 0, j))),
        out_shape=(jax.ShapeDtypeStruct((rows, D_MODEL), F32),
                   jax.ShapeDtypeStruct((rows // tm, 2, SUBLANES, D_FF_PAD), F32)),
        scratch_shapes=[pltpu.VMEM((tm, D_MODEL), F32), pltpu.VMEM((BF16_ROWS + tm, D_MODEL), BF16)],
        compiler_params=_params(2),
        name="ffn_prompt",
    )(h2, h2, x1, mod_p, w_up_a, w_up_b, w_down, w_conv, b_conv, ln2_g, ln2_b)


def _ffn_sample_kernel(n_tok, h_ref, x1_ref, g2_ref, wa_ref, wb_ref, wd_ref, wc_ref, bc_ref, past_ref,
                       l2g_ref, l2b_ref, y_ref, st_ref, acc_scr):
    j = pl.program_id(0)
    n_seq = g2_ref.shape[0]
    h = h_ref[...]

    def conv(half, w_ref):
        up = _dot(h, w_ref[...])
        full = [past_ref[r, half] for r in range(CONV_W - 1)]
        full += [up[t * n_seq:(t + 1) * n_seq, :] for t in range(n_tok)]
        for r in range(CONV_W - 1):
            st_ref[r, half] = full[n_tok + r]
        w = wc_ref[half]
        b = bc_ref[half:half + 1, :]
        return jnp.concatenate(
            [b + sum(full[t + c] * w[c:c + 1, :] for c in range(CONV_W)) for t in range(n_tok)], axis=0)

    @pl.when(j == 0)
    def _():
        acc_scr[...] = jnp.zeros_like(acc_scr)

    acc_scr[...] += _dot(_ffn_act(conv(0, wa_ref), conv(1, wb_ref)), wd_ref[...])

    @pl.when(j == pl.num_programs(0) - 1)
    def _():
        g2 = g2_ref[...]
        for t in range(n_tok):
            rs = slice(t * n_seq, (t + 1) * n_seq)
            y_ref[:, t * D_MODEL:(t + 1) * D_MODEL] = _layer_norm(
                ALPHA * x1_ref[rs, :] + g2 * acc_scr[rs, :], l2g_ref[...], l2b_ref[...])


def _ffn_sample(h2, x1, mod_s, w_up_a, w_up_b, w_down, w_conv, b_conv, past, ln2_g, ln2_b, *, n_seq, n_tok):
    rows = n_seq * n_tok
    full = lambda j: (0, 0)
    state_blk = pl.BlockSpec((CONV_W - 1, 2, n_seq, FF_TILE), lambda j: (0, 0, 0, j))
    return pl.pallas_call(
        functools.partial(_ffn_sample_kernel, n_tok),
        grid=(N_FF_TILES,),
        in_specs=[pl.BlockSpec((rows, D_MODEL), full), pl.BlockSpec((rows, D_MODEL), full),
                  pl.BlockSpec((n_seq, D_MODEL), lambda j: (0, 5)),
                  pl.BlockSpec((D_MODEL, FF_TILE), lambda j: (0, j)),
                  pl.BlockSpec((D_MODEL, FF_TILE), lambda j: (0, j)),
                  pl.BlockSpec((FF_TILE, D_MODEL), lambda j: (j, 0)),
                  pl.BlockSpec((2, CONV_W, FF_TILE), lambda j: (0, 0, j)),
                  pl.BlockSpec((2, FF_TILE), lambda j: (0, j)),
                  state_blk,
                  pl.BlockSpec((1, D_MODEL), full), pl.BlockSpec((1, D_MODEL), full)],
        out_specs=(pl.BlockSpec((n_seq, n_tok * D_MODEL), full), state_blk),
        out_shape=(jax.ShapeDtypeStruct((n_seq, n_tok * D_MODEL), F32),
                   jax.ShapeDtypeStruct((CONV_W - 1, 2, n_seq, D_FF_PAD), F32)),
        scratch_shapes=[pltpu.VMEM((rows, D_MODEL), F32)],
        compiler_params=_params(1),
        name="ffn_sample",
    )(h2, x1, mod_s, w_up_a, w_up_b, w_down, w_conv, b_conv, past, ln2_g, ln2_b)


def _t5_bucket_table(n_rel):
    max_exact = NUM_BUCKETS // 2
    ratio = MAX_DISTANCE // max_exact
    log_ratio = ratio.bit_length() - 1
    assert 1 << log_ratio == ratio
    steps = NUM_BUCKETS - max_exact
    out = np.zeros((n_rel,), np.int32)
    for n in range(n_rel):
        if n < max_exact:
            out[n] = n
            continue
        k = 0
        while k + 1 < steps and n ** steps >= (max_exact ** steps) << (log_ratio * (k + 1)):
            k += 1
        out[n] = min(max_exact + k, NUM_BUCKETS - 1)
    return out


def _bias_tables(rel_bias, seq, past, n_tok):
    n_rel = max(seq, past + n_tok)
    onehot = np.zeros((NUM_BUCKETS, n_rel), np.float32)
    onehot[_t5_bucket_table(n_rel), np.arange(n_rel)] = 1.0
    tab = jnp.dot(rel_bias.astype(F32).T, onehot, precision=lax.Precision.HIGHEST)
    n_blocks = seq // MOBA_BLOCK
    tab_pad = jnp.pad(tab, ((0, 0), (MOBA_BLOCK, 1)))
    rev = jnp.stack([tab_pad[:, d * MOBA_BLOCK + 1:(d + 2) * MOBA_BLOCK + 1][:, ::-1]
                     for d in range(n_blocks)], axis=1)
    same_head = np.eye(N_HEADS_B, dtype=bool)
    by_key = jnp.stack([tab[:, t + 1:t + 1 + past][:, ::-1] for t in range(n_tok)], axis=0)
    n_pages = past // PAGE_SIZE
    spread = np.repeat(np.eye(PAGE_SIZE, dtype=np.float32), N_HEADS_B, axis=1)
    by_col = jnp.einsum('rpk,kc->rpc', by_key.reshape(n_tok * N_HEADS_B, n_pages, PAGE_SIZE), spread,
                        precision=lax.Precision.HIGHEST).reshape(n_tok * N_HEADS_B, past * N_HEADS_B)
    row_head = np.tile(np.arange(N_HEADS_B), n_tok)[:, None]
    col_head = np.tile(np.arange(N_HEADS_B), past)[None, :]
    bias_past = jnp.where(row_head == col_head, by_col, NEG_INF)
    new_rows = []
    for t in range(n_tok):
        by_tok = jnp.stack([tab[:, max(t - s, 0)] for s in range(n_tok)], axis=1)
        ok = same_head[:, None, :] & (np.arange(n_tok) <= t)[None, :, None]
        new_rows.append(jnp.where(ok, by_tok[:, :, None], NEG_INF).reshape(N_HEADS_B, n_tok * N_HEADS_B))
    bias_new = jnp.stack(new_rows, axis=0).reshape(n_tok * N_HEADS_B, n_tok * N_HEADS_B)
    return rev * LOG2_E, bias_past, bias_new


def _pad_ff(a, axis):
    pad = [(0, 0)] * a.ndim
    pad[axis] = (0, D_FF_PAD - D_FF)
    return jnp.pad(a, pad)


def kernel(x_prompt, x_sample, cache_k, cache_v, state_ffn_conv, page_table, c_prompt, c_sample,
           w_ada, b_ada, w_in, ln_sgu_g, ln_sgu_b, w_s, b_s, rel_bias, norm_a_g, norm_b_g, w_out,
           ln1_g, ln1_b, w_up, w_conv, b_conv, w_down, ln2_g, ln2_b):
    n_p, seq, _ = x_prompt.shape
    n_s, n_tok, _ = x_sample.shape
    n_pages = page_table.shape[1]
    past = n_pages * PAGE_SIZE
    assert w_ada.shape[0] == 1 and seq % MOBA_BLOCK == 0 and past % MOBA_BLOCK == 0 and n_tok <= CHUNK

    w_in_bf = w_in[0].astype(BF16)
    w_out_bf = w_out[0].astype(BF16)
    w_up_a = _cast_pad_cols(w_up[0], 0, D_FF, D_FF_PAD, FF_TILE)
    w_up_b = _cast_pad_cols(w_up[0], D_FF, D_FF, D_FF_PAD, FF_TILE)
    w_down_bf = _cast_pad_rows(w_down[0], D_FF_PAD, FF_TILE)
    w_conv_p = _pad_ff(jnp.transpose(w_conv[0].reshape(CONV_W, 2, D_FF), (1, 0, 2)), 2)
    b_conv_p = _pad_ff(b_conv[0].reshape(2, D_FF), 1)
    row = lambda a: a.reshape(1, -1)
    ln_g, ln_b = row(ln_sgu_g[0]), row(ln_sgu_b[0])
    nag, nbg = row(norm_a_g[0]), row(norm_b_g[0])
    l1g, l1b, l2g, l2b = row(ln1_g[0]), row(ln1_b[0]), row(ln2_g[0]), row(ln2_b[0])
    bs_full = jnp.broadcast_to(b_s[0][:, :, None], (N_GROUPS_A, CHUNK, HEAD_DIM))
    tril_tok = np.tril(np.ones((n_tok, n_tok), np.float32))
    ws_tok = jnp.repeat(jnp.transpose(w_s[0][:, :n_tok, :n_tok] * tril_tok, (1, 2, 0)), HEAD_DIM, axis=2)
    bs_tok = jnp.repeat(b_s[0][:, :n_tok].T, HEAD_DIM, axis=1).reshape(n_tok, 1, W_A)
    rev_tab, bias_past, bias_new = _bias_tables(rel_bias, seq, past, n_tok)

    pad_rows = (-(n_s + n_p)) % SUBLANES
    c_all = jnp.concatenate([c_sample, c_prompt, jnp.zeros((pad_rows, D_MODEL), F32)], axis=0)
    mod = _ada(c_all, w_ada[0], b_ada[0])
    mod_s = mod[:n_s]
    mod_p = mod[n_s:n_s + n_p].reshape(n_p, 1, 6 * D_MODEL)

    xp = x_prompt.reshape(n_p * seq, D_MODEL)
    q_hi, q_lo, k_p, v_p, k_bf, v_bf, k_mean, gm_n = _proj_prompt(
        xp, mod_p, w_in_bf, ln_g, ln_b, w_s[0], bs_full, nag, tm=256, seq=seq)
    attn_p = _attn_prompt(q_hi, q_lo, k_bf, v_bf, k_mean.reshape(n_p, seq // MOBA_BLOCK, W_B), rev_tab,
                          n_seq=n_p, seq=seq)
    x1_p, h2_p = _mix_prompt(xp, gm_n, attn_p, mod_p, w_out_bf, nbg, l1g, l1b, tm=512, seq=seq)
    y_p, st_p = _ffn_prompt(h2_p, x1_p, mod_p, w_up_a, w_up_b, w_down_bf, w_conv_p, b_conv_p, l2g, l2b,
                            tm=512, seq=seq)
    tiles_per_seq = st_p.shape[0] // n_p
    conv_p = st_p[tiles_per_seq - 1::tiles_per_seq, :, SUBLANES - (CONV_W - 1):, :D_FF]
    conv_p = jnp.transpose(conv_p, (0, 2, 1, 3)).reshape(1, n_p, CONV_W - 1, 2 * D_FF)

    xs = x_sample.reshape(n_s, n_tok * D_MODEL)
    q_s, k_s, v_s, u_s, va_s = _proj_sample(xs, mod_s, w_in_bf, ln_g, ln_b, n_seq=n_s, n_tok=n_tok)
    heads = lambda a: a.reshape(n_s, n_tok, N_HEADS_B, HEAD_DIM)
    attn_s = _attn_sample(page_table, heads(q_s), heads(k_s), heads(v_s), bias_past, bias_new, cache_k, cache_v)
    x1_s, h2_s = _mix_sample(xs, u_s, va_s, ws_tok, bs_tok, nag, attn_s.reshape(n_s, n_tok * W_B), mod_s,
                             w_out_bf, nbg, l1g, l1b, n_seq=n_s, n_tok=n_tok)
    past_up = state_ffn_conv[0].reshape(n_s, CONV_W - 1, 2, D_FF)
    past_up = _pad_ff(jnp.transpose(past_up, (1, 2, 0, 3)), 3)
    y_s, st_s = _ffn_sample(h2_s, x1_s, mod_s, w_up_a, w_up_b, w_down_bf, w_conv_p, b_conv_p, past_up,
                            l2g, l2b, n_seq=n_s, n_tok=n_tok)
    conv_s = jnp.transpose(st_s[..., :D_FF], (2, 0, 1, 3)).reshape(1, n_s, CONV_W - 1, 2 * D_FF)

    heads_p = (1, n_p, seq, N_HEADS_B, HEAD_DIM)
    heads_s = (1, n_s, n_tok, N_HEADS_B, HEAD_DIM)
    return (y_p.reshape(n_p, seq, D_MODEL), y_s.reshape(n_s, n_tok, D_MODEL),
            k_p.reshape(heads_p), v_p.reshape(heads_p), conv_p,
            k_s.reshape(heads_s), v_s.reshape(heads_s), va_s.reshape(1, n_s, n_tok, W_A), conv_s)
```

```python
import functools

import numpy as np
import jax
import jax.numpy as jnp
from jax import lax
from jax.experimental import pallas as pl
from jax.experimental.pallas import tpu as pltpu

F32 = jnp.float32
BF16 = jnp.bfloat16

D_MODEL = 2048
HEAD_DIM = 128
N_HEADS_B = 8
N_GROUPS_A = 8
W_B = N_HEADS_B * HEAD_DIM
W_A = N_GROUPS_A * HEAD_DIM
CHUNK = 128
MOBA_BLOCK = 256
MOBA_TOPK = 3
NUM_BUCKETS = 32
MAX_DISTANCE = 1024
CONV_W = 3
D_FF = 5504
PAGE_SIZE = 128
ALPHA = 2.0 ** 0.25
LN_EPS = 1e-5
NEG_INF = -1e30
LOG2_E = 1.4426950408889634

FF_TILE = 512
D_FF_PAD = ((D_FF + FF_TILE - 1) // FF_TILE) * FF_TILE
N_FF_TILES = D_FF_PAD // FF_TILE
SUBLANES = 8
BF16_ROWS = 16
FFN_ROWS = 256
MIX_ROWS = 256
VMEM_LIMIT_BYTES = 56 * 1024 * 1024

_NT = (((1,), (1,)), ((), ()))


def _params(n_axes, flags=None):
    return pltpu.CompilerParams(dimension_semantics=("arbitrary",) * n_axes,
                                vmem_limit_bytes=VMEM_LIMIT_BYTES, flags=flags)


def _layer_norm(x, g, b):
    mu = jnp.mean(x, axis=-1, keepdims=True)
    xc = x - mu
    var = jnp.mean(xc * xc, axis=-1, keepdims=True)
    return xc * lax.rsqrt(var + LN_EPS) * g + b


def _rms_norm(x, g):
    return x * lax.rsqrt(jnp.mean(x * x, axis=-1, keepdims=True) + LN_EPS) * g


def _split_bf16(x):
    hi = x.astype(BF16)
    lo = (x - hi.astype(F32)).astype(BF16)
    return hi, lo


def _dot(a, b):
    return jnp.dot(a, b, preferred_element_type=F32)


def _dot_nt(a, b):
    return lax.dot_general(a, b, _NT, preferred_element_type=F32)


def _topk_select(s, n_valid, axis):
    idx = lax.broadcasted_iota(jnp.int32, s.shape, axis)
    rank = jnp.zeros(s.shape, F32)
    for i in range(n_valid):
        si = lax.slice_in_dim(s, i, i + 1, axis=axis)
        beats = (si > s) | ((si == s) & (i < idx))
        rank = rank + beats.astype(F32)
    return ((rank < MOBA_TOPK) & (idx < n_valid)).astype(F32)


def _ada_kernel(c_ref, w_ref, b_ref, o_ref):
    c = c_ref[...]
    a = jax.nn.silu(c).astype(BF16)
    o_ref[...] = _dot(a, w_ref[...].astype(BF16)) + b_ref[...]


def _ada(c_all, w_ada, b_ada):
    rows = c_all.shape[0]
    n_out = w_ada.shape[1]
    tn = 512
    return pl.pallas_call(
        _ada_kernel,
        grid=(n_out // tn,),
        in_specs=[pl.BlockSpec((rows, D_MODEL), lambda j: (0, 0)),
                  pl.BlockSpec((D_MODEL, tn), lambda j: (0, j)),
                  pl.BlockSpec((1, tn), lambda j: (0, j))],
        out_specs=pl.BlockSpec((rows, tn), lambda j: (0, j)),
        out_shape=jax.ShapeDtypeStruct((rows, n_out), F32),
        compiler_params=_params(1),
        name="ada",
    )(c_all, w_ada, b_ada.reshape(1, n_out))


_COL_Q, _COL_K, _COL_V, _COL_U, _COL_VA = range(5)


def _proj_prompt_kernel(x_ref, sc_ref, sh_ref, w_ref, lng_ref, lnb_ref, ws_ref, bs_ref, nag_ref,
                        wup_ref, wdn_ref, wout_ref,
                        qhi_ref, qlo_ref, k_ref, v_ref, kbf_ref, vbf_ref, km_ref, gm_ref,
                        wa_ref, wb_ref, wd_ref, wo_ref):
    tm = x_ref.shape[0]
    w_up = wup_ref[...]
    ff_pad = jnp.zeros((w_up.shape[0], D_FF_PAD - D_FF), BF16)
    wa_ref[...] = jnp.concatenate([w_up[:, :D_FF].astype(BF16), ff_pad], axis=1)
    wb_ref[...] = jnp.concatenate([w_up[:, D_FF:].astype(BF16), ff_pad], axis=1)
    dn_row = pl.program_id(0) * wdn_ref.shape[0] + lax.broadcasted_iota(jnp.int32, wdn_ref.shape, 0)
    wd_ref[...] = jnp.where(dn_row < D_FF, wdn_ref[...], 0.0).astype(BF16)
    wo_ref[...] = wout_ref[...].astype(BF16)

    h = (x_ref[...] * (1.0 + sc_ref[...]) + sh_ref[...]).astype(BF16)

    def proj(c):
        return _dot(h, w_ref[:, c * W_B:(c + 1) * W_B])

    def gmlp(u, acc_va):
        va = _layer_norm(jax.nn.gelu(acc_va), lng_ref[...], lnb_ref[...]).astype(BF16)
        n_chunks = tm // CHUNK
        row = lax.broadcasted_iota(jnp.int32, (CHUNK, CHUNK), 0)
        col = lax.broadcasted_iota(jnp.int32, (CHUNK, CHUNK), 1)
        groups = []
        for g in range(N_GROUPS_A):
            gs = slice(g * HEAD_DIM, (g + 1) * HEAD_DIM)
            w_g = jnp.where(row >= col, ws_ref[g], 0.0).astype(BF16)
            v_g = jnp.concatenate([va[c * CHUNK:(c + 1) * CHUNK, gs] for c in range(n_chunks)], axis=1)
            m_g = _dot(w_g, v_g)
            mixed = jnp.concatenate([m_g[:, c * HEAD_DIM:(c + 1) * HEAD_DIM] + bs_ref[g]
                                     for c in range(n_chunks)], axis=0)
            groups.append(u[:, gs] * mixed)
        gm_ref[...] = _rms_norm(jnp.concatenate(groups, axis=1), nag_ref[...]).astype(BF16)

    def put_q(acc):
        q_hi, q_lo = _split_bf16(acc * (LOG2_E * HEAD_DIM ** -0.5))
        qhi_ref[...] = q_hi
        qlo_ref[...] = q_lo

    def put_k(acc):
        k_ref[...] = acc
        kbf_ref[...] = acc.astype(BF16)
        km_ref[...] = jnp.mean(acc.reshape(tm // MOBA_BLOCK, MOBA_BLOCK, W_B), axis=1)

    def put_v(acc):
        v_ref[...] = acc
        vbf_ref[...] = acc.astype(BF16)

    acc_u = proj(_COL_U)
    acc_va = proj(_COL_VA)
    u = jax.nn.gelu(acc_u)
    acc_q = proj(_COL_Q)
    gmlp(u, acc_va)
    acc_k = proj(_COL_K)
    put_q(acc_q)
    acc_v = proj(_COL_V)
    put_k(acc_k)
    put_v(acc_v)


def _proj_prompt(x, mod_p, w_in, ln_g, ln_b, w_s, bs_full, norm_a_g, w_up, w_down, w_out, *, tm, seq):
    rows = x.shape[0]
    tiles_per_seq = seq // tm
    n_steps = rows // tm
    up_rows, dn_rows, out_rows = w_up.shape[0] // n_steps, D_FF_PAD // n_steps, w_out.shape[0] // n_steps
    assert up_rows * n_steps == w_up.shape[0] and dn_rows * n_steps == D_FF_PAD
    assert out_rows * n_steps == w_out.shape[0] and (n_steps - 1) * dn_rows < w_down.shape[0]
    assert up_rows % BF16_ROWS == 0 and dn_rows % BF16_ROWS == 0 and out_rows % BF16_ROWS == 0
    row_blk = lambda i: (i, 0)
    vec = lambda i: (0, 0)
    whole3 = lambda i: (0, 0, 0)
    out_shape = (
        jax.ShapeDtypeStruct((rows, W_B), BF16),
        jax.ShapeDtypeStruct((rows, W_B), BF16),
        jax.ShapeDtypeStruct((rows, W_B), F32),
        jax.ShapeDtypeStruct((rows, W_B), F32),
        jax.ShapeDtypeStruct((rows, W_B), BF16),
        jax.ShapeDtypeStruct((rows, W_B), BF16),
        jax.ShapeDtypeStruct((rows // tm, tm // MOBA_BLOCK, W_B), F32),
        jax.ShapeDtypeStruct((rows, W_A), BF16),
        jax.ShapeDtypeStruct((w_up.shape[0], D_FF_PAD), BF16),
        jax.ShapeDtypeStruct((w_up.shape[0], D_FF_PAD), BF16),
        jax.ShapeDtypeStruct((D_FF_PAD, w_down.shape[1]), BF16),
        jax.ShapeDtypeStruct(w_out.shape, BF16),
    )
    out_specs = (
        pl.BlockSpec((tm, W_B), row_blk), pl.BlockSpec((tm, W_B), row_blk),
        pl.BlockSpec((tm, W_B), row_blk), pl.BlockSpec((tm, W_B), row_blk),
        pl.BlockSpec((tm, W_B), row_blk), pl.BlockSpec((tm, W_B), row_blk),
        pl.BlockSpec((None, tm // MOBA_BLOCK, W_B), lambda i: (i, 0, 0)),
        pl.BlockSpec((tm, W_A), row_blk),
        pl.BlockSpec((up_rows, D_FF_PAD), row_blk), pl.BlockSpec((up_rows, D_FF_PAD), row_blk),
        pl.BlockSpec((dn_rows, w_down.shape[1]), row_blk),
        pl.BlockSpec((out_rows, w_out.shape[1]), row_blk),
    )
    return pl.pallas_call(
        _proj_prompt_kernel,
        grid=(rows // tm,),
        in_specs=[pl.BlockSpec((tm, D_MODEL), row_blk),
                  pl.BlockSpec((None, 1, D_MODEL), lambda i: (i // tiles_per_seq, 0, 1)),
                  pl.BlockSpec((None, 1, D_MODEL), lambda i: (i // tiles_per_seq, 0, 0)),
                  pl.BlockSpec(w_in.shape, vec, pipeline_mode=pl.Buffered(1)),
                  pl.BlockSpec((1, W_A), vec), pl.BlockSpec((1, W_A), vec),
                  pl.BlockSpec((N_GROUPS_A, CHUNK, CHUNK), whole3),
                  pl.BlockSpec((N_GROUPS_A, CHUNK, HEAD_DIM), whole3),
                  pl.BlockSpec((1, W_A), vec),
                  pl.BlockSpec((up_rows, w_up.shape[1]), row_blk),
                  pl.BlockSpec((dn_rows, w_down.shape[1]), row_blk),
                  pl.BlockSpec((out_rows, w_out.shape[1]), row_blk)],
        out_specs=out_specs,
        out_shape=out_shape,
        compiler_params=_params(1),
        name="proj_prompt",
    )(x, mod_p, mod_p, w_in, ln_g, ln_b, w_s, bs_full, norm_a_g, w_up, w_down, w_out)


def _proj_sample_kernel(n_tok, x_ref, sc_ref, sh_ref, w_ref, lng_ref, lnb_ref,
                        q_ref, k_ref, v_ref, u_ref, va_ref, h_scr):
    j = pl.program_id(0)
    n_seq = sc_ref.shape[0]

    @pl.when(j == 0)
    def _():
        for t in range(n_tok):
            x_t = x_ref[:, t * D_MODEL:(t + 1) * D_MODEL]
            h_scr[t * n_seq:(t + 1) * n_seq, :] = (x_t * (1.0 + sc_ref[...]) + sh_ref[...]).astype(BF16)

    acc = _dot(h_scr[...], w_ref[...])

    def put(o_ref, val):
        for t in range(n_tok):
            o_ref[:, t * W_B:(t + 1) * W_B] = val[t * n_seq:(t + 1) * n_seq, :]

    @pl.when(j == _COL_Q)
    def _():
        put(q_ref, acc * (HEAD_DIM ** -0.5))

    @pl.when(j == _COL_K)
    def _():
        put(k_ref, acc)

    @pl.when(j == _COL_V)
    def _():
        put(v_ref, acc)

    @pl.when(j == _COL_U)
    def _():
        put(u_ref, jax.nn.gelu(acc))

    @pl.when(j == _COL_VA)
    def _():
        put(va_ref, _layer_norm(jax.nn.gelu(acc), lng_ref[...], lnb_ref[...]))


def _proj_sample(x2d, mod_s, w_in, ln_g, ln_b, *, n_seq, n_tok):
    n_col = w_in.shape[1] // W_B
    whole = lambda j: (0, 0)
    out = jax.ShapeDtypeStruct((n_seq, n_tok * W_B), F32)
    return pl.pallas_call(
        functools.partial(_proj_sample_kernel, n_tok),
        grid=(n_col,),
        in_specs=[pl.BlockSpec((n_seq, n_tok * D_MODEL), whole),
                  pl.BlockSpec((n_seq, D_MODEL), lambda j: (0, 1)),
                  pl.BlockSpec((n_seq, D_MODEL), lambda j: (0, 0)),
                  pl.BlockSpec((D_MODEL, W_B), lambda j: (0, j)),
                  pl.BlockSpec((1, W_A), whole), pl.BlockSpec((1, W_A), whole)],
        out_specs=tuple(pl.BlockSpec((n_seq, n_tok * W_B), whole) for _ in range(5)),
        out_shape=(out,) * 5,
        scratch_shapes=[pltpu.VMEM((n_tok * n_seq, D_MODEL), BF16)],
        compiler_params=_params(1),
        name="proj_sample",
    )(x2d, mod_s, mod_s, w_in, ln_g, ln_b)


def _attn_prompt_kernel(qhi_ref, qlo_ref, k_ref, v_ref, km_ref, rev_ref, o_ref, bias_scr):
    n_blocks = k_ref.shape[0] // MOBA_BLOCK

    @pl.when(pl.program_id(1) == 0)
    def _():
        for d in range(n_blocks):
            r = jnp.broadcast_to(rev_ref[d:d + 1, :], (MOBA_BLOCK, 2 * MOBA_BLOCK))
            rolled = pltpu.roll(r, 0, 1, stride=1, stride_axis=0)
            bias_scr[d] = rolled[:, MOBA_BLOCK:]

    km_hi, km_lo = _split_bf16(km_ref[...])
    row = lax.broadcasted_iota(jnp.int32, (MOBA_BLOCK, MOBA_BLOCK), 0)
    col = lax.broadcasted_iota(jnp.int32, (MOBA_BLOCK, MOBA_BLOCK), 1)

    def rows_of(qb):
        return slice(qb * MOBA_BLOCK, (qb + 1) * MOBA_BLOCK)

    def qk(qb):
        return _dot_nt(qhi_ref[rows_of(qb), :], k_ref[0:(qb + 1) * MOBA_BLOCK, :])

    def softmax(qb, logits):
        sel = None
        if qb > MOBA_TOPK:
            q_hi, q_lo = qhi_ref[rows_of(qb), :], qlo_ref[rows_of(qb), :]
            s_t = _dot_nt(km_hi, q_hi) + _dot_nt(km_lo, q_hi) + _dot_nt(km_hi, q_lo)
            sel = _topk_select(s_t, qb, axis=0).T
        pieces = []
        for jb in range(qb + 1):
            lj = logits[:, jb * MOBA_BLOCK:(jb + 1) * MOBA_BLOCK] + bias_scr[qb - jb]
            if jb == qb:
                lj = jnp.where(row >= col, lj, NEG_INF)
            elif sel is not None:
                lj = jnp.where(sel[:, jb:jb + 1] > 0.0, lj, NEG_INF)
            pieces.append(lj)
        lg = jnp.concatenate(pieces, axis=1) if len(pieces) > 1 else pieces[0]
        m = jnp.max(lg, axis=-1, keepdims=True)
        p = jnp.exp2(lg - m)
        return p.astype(BF16), jnp.sum(p, axis=-1, keepdims=True)

    def pv(qb, p, z):
        o_ref[rows_of(qb), :] = _dot(p, v_ref[0:(qb + 1) * MOBA_BLOCK, :]) / z

    logits = qk(0)
    prev = None
    for qb in range(n_blocks):
        nxt = qk(qb + 1) if qb + 1 < n_blocks else None
        p, z = softmax(qb, logits)
        if prev is not None:
            pv(*prev)
        prev = (qb, p, z)
        logits = nxt
    pv(*prev)


def _attn_prompt(q_hi, q_lo, k_bf, v_bf, k_mean, rev_tab, *, n_seq, seq):
    n_blocks = seq // MOBA_BLOCK
    qkv = pl.BlockSpec((seq, HEAD_DIM), lambda h, n: (n, h))
    return pl.pallas_call(
        _attn_prompt_kernel,
        grid=(N_HEADS_B, n_seq),
        in_specs=[qkv, qkv, qkv, qkv,
                  pl.BlockSpec((None, n_blocks, HEAD_DIM), lambda h, n: (n, 0, h)),
                  pl.BlockSpec((None, n_blocks, 2 * MOBA_BLOCK), lambda h, n: (h, 0, 0))],
        out_specs=pl.BlockSpec((seq, HEAD_DIM), lambda h, n: (n, h)),
        out_shape=jax.ShapeDtypeStruct((n_seq * seq, W_B), F32),
        scratch_shapes=[pltpu.VMEM((n_blocks, MOBA_BLOCK, MOBA_BLOCK), F32)],
        compiler_params=_params(2),
        name="attn_prompt",
    )(q_hi, q_lo, k_bf, v_bf, k_mean, rev_tab)


def _attn_sample_kernel(n_pages, pt_ref, q_ref, kn_ref, vn_ref, bias_ref, biasn_ref, *refs):
    k_pages = refs[:n_pages]
    v_pages = refs[n_pages:2 * n_pages]
    o_ref = refs[2 * n_pages]
    n_tok = q_ref.shape[0]
    rows = n_tok * N_HEADS_B
    cols = PAGE_SIZE * N_HEADS_B
    pages_per_block = MOBA_BLOCK // PAGE_SIZE
    n_blocks = n_pages // pages_per_block

    q = q_ref[...].reshape(rows, HEAD_DIM)
    q_bf = q.astype(BF16)

    logit_pages = []
    k_sums = []
    for p in range(n_pages):
        kp = k_pages[p][...]
        k_sums.append(jnp.sum(kp, axis=0))
        logit_pages.append(_dot_nt(q_bf, kp.reshape(cols, HEAD_DIM).astype(BF16)))

    blk_lane = lax.broadcasted_iota(jnp.int32, (rows, n_blocks), 1)
    scores = jnp.zeros((rows, n_blocks), F32)
    for b in range(n_blocks):
        km = sum(k_sums[b * pages_per_block:(b + 1) * pages_per_block]) * (1.0 / MOBA_BLOCK)
        s_b = jnp.sum(q * jnp.concatenate([km] * n_tok, axis=0), axis=1, keepdims=True)
        scores = jnp.where(blk_lane == b, s_b, scores)
    sel = _topk_select(scores, n_blocks, axis=1)

    masked = []
    for p in range(n_pages):
        b = p // pages_per_block
        lp = logit_pages[p] + bias_ref[:, p * cols:(p + 1) * cols]
        masked.append(jnp.where(sel[:, b:b + 1] > 0.0, lp, NEG_INF))
    ln = _dot_nt(q_bf, kn_ref[...].reshape(rows, HEAD_DIM).astype(BF16)) + biasn_ref[...]

    m = jnp.maximum(jnp.max(functools.reduce(jnp.maximum, masked), axis=-1, keepdims=True),
                    jnp.max(ln, axis=-1, keepdims=True))
    p_new = jnp.exp(ln - m)
    acc = _dot(p_new.astype(BF16), vn_ref[...].reshape(rows, HEAD_DIM).astype(BF16))
    p_sum = None
    for p in range(n_pages):
        pp = jnp.exp(masked[p] - m)
        p_sum = pp if p_sum is None else p_sum + pp
        acc = acc + _dot(pp.astype(BF16), v_pages[p][...].reshape(cols, HEAD_DIM).astype(BF16))
    z = jnp.sum(p_sum, axis=-1, keepdims=True) + jnp.sum(p_new, axis=-1, keepdims=True)
    o_ref[...] = (acc / z).reshape(n_tok, N_HEADS_B, HEAD_DIM)


def _attn_sample(page_table, q, k_new, v_new, bias_past, bias_new, cache_k, cache_v):
    n_seq, n_pages = page_table.shape
    n_tok = q.shape[1]
    tok = pl.BlockSpec((None, n_tok, N_HEADS_B, HEAD_DIM), lambda n, pt: (n, 0, 0, 0))

    def page_spec(p):
        return pl.BlockSpec((None, None, PAGE_SIZE, N_HEADS_B, HEAD_DIM),
                            lambda n, pt: (0, pt[n * n_pages + p], 0, 0, 0))

    grid_spec = pltpu.PrefetchScalarGridSpec(
        num_scalar_prefetch=1,
        grid=(n_seq,),
        in_specs=[tok, tok, tok,
                  pl.BlockSpec(bias_past.shape, lambda n, pt: (0, 0)),
                  pl.BlockSpec(bias_new.shape, lambda n, pt: (0, 0))]
                 + [page_spec(p) for p in range(n_pages)] * 2,
        out_specs=tok,
    )
    return pl.pallas_call(
        functools.partial(_attn_sample_kernel, n_pages),
        grid_spec=grid_spec,
        out_shape=jax.ShapeDtypeStruct((n_seq, n_tok, N_HEADS_B, HEAD_DIM), F32),
        compiler_params=_params(1),
        name="attn_sample",
    )(page_table.reshape(-1), q, k_new, v_new, bias_past, bias_new,
      *([cache_k] * n_pages), *([cache_v] * n_pages))


def _mix_project(gm_n, attn, wo_ref, nbg_ref):
    a_n = _rms_norm(attn, nbg_ref[...]).astype(BF16)
    return _dot(gm_n, wo_ref[0:W_A, :]) + _dot(a_n, wo_ref[W_A:W_A + W_B, :])


def _mix_finish(x, o, g1, sc2, sh2, l1g_ref, l1b_ref):
    x1 = _layer_norm(ALPHA * x + g1 * o, l1g_ref[...], l1b_ref[...])
    return x1, (x1 * (1.0 + sc2) + sh2).astype(BF16)


def _mix_prompt_kernel(x_ref, gm_ref, at_ref, g1_ref, sc2_ref, sh2_ref, wo_ref, nbg_ref, l1g_ref, l1b_ref,
                       x1_ref, h2_ref):
    tm = x_ref.shape[0]
    parts = [slice(r, r + MIX_ROWS) for r in range(0, tm, MIX_ROWS)]
    outs = [_mix_project(gm_ref[rs, :], at_ref[rs, :], wo_ref, nbg_ref) for rs in parts]
    for rs, o in zip(parts, outs):
        x1_ref[rs, :], h2_ref[rs, :] = _mix_finish(x_ref[rs, :], o, g1_ref[...], sc2_ref[...], sh2_ref[...],
                                                   l1g_ref, l1b_ref)


def _mix_prompt(x, gm_n, attn, mod_p, w_out, norm_b_g, ln1_g, ln1_b, *, tm, seq):
    rows = x.shape[0]
    tiles_per_seq = seq // tm
    row_blk = lambda i: (i, 0)
    vec = lambda i: (0, 0)
    mod = lambda c: pl.BlockSpec((None, 1, D_MODEL), lambda i: (i // tiles_per_seq, 0, c))
    return pl.pallas_call(
        _mix_prompt_kernel,
        grid=(rows // tm,),
        in_specs=[pl.BlockSpec((tm, D_MODEL), row_blk), pl.BlockSpec((tm, W_A), row_blk),
                  pl.BlockSpec((tm, W_B), row_blk), mod(2), mod(4), mod(3),
                  pl.BlockSpec((D_MODEL, D_MODEL), vec), pl.BlockSpec((1, W_B), vec),
                  pl.BlockSpec((1, D_MODEL), vec), pl.BlockSpec((1, D_MODEL), vec)],
        out_specs=(pl.BlockSpec((tm, D_MODEL), row_blk), pl.BlockSpec((tm, D_MODEL), row_blk)),
        out_shape=(jax.ShapeDtypeStruct((rows, D_MODEL), F32), jax.ShapeDtypeStruct((rows, D_MODEL), BF16)),
        compiler_params=_params(1),
        name="mix_prompt",
    )(x, gm_n, attn, mod_p, mod_p, mod_p, w_out, norm_b_g, ln1_g, ln1_b)


def _mix_sample_kernel(x_ref, u_ref, va_ref, ws_ref, bs_ref, nag_ref, at_ref, g1_ref, sc2_ref, sh2_ref,
                       wo_ref, nbg_ref, l1g_ref, l1b_ref, x1_ref, h2_ref):
    n_tok = ws_ref.shape[0]
    mixed = bs_ref[...]
    for s in range(n_tok):
        mixed = mixed + ws_ref[s:s + 1, :] * va_ref[:, s * W_A:(s + 1) * W_A]
    gm_n = _rms_norm(u_ref[...] * mixed, nag_ref[...]).astype(BF16)
    o = _mix_project(gm_n, at_ref[...], wo_ref, nbg_ref)
    x1_ref[...], h2_ref[...] = _mix_finish(x_ref[...], o, g1_ref[...], sc2_ref[...], sh2_ref[...],
                                           l1g_ref, l1b_ref)


def _mix_sample(x2d, u, va, ws_tok, bs_tok, norm_a_g, attn2d, mod_s, w_out, norm_b_g, ln1_g, ln1_b,
                *, n_seq, n_tok):
    tok_blk = lambda t: (0, t)
    vec = lambda t: (0, 0)
    mod = lambda c: pl.BlockSpec((n_seq, D_MODEL), lambda t: (0, c))
    out_blk = pl.BlockSpec((n_seq, D_MODEL), lambda t: (t, 0))
    return pl.pallas_call(
        _mix_sample_kernel,
        grid=(n_tok,),
        in_specs=[pl.BlockSpec((n_seq, D_MODEL), tok_blk), pl.BlockSpec((n_seq, W_A), tok_blk),
                  pl.BlockSpec((n_seq, n_tok * W_A), vec),
                  pl.BlockSpec((None, n_tok, W_A), lambda t: (t, 0, 0)),
                  pl.BlockSpec((None, 1, W_A), lambda t: (t, 0, 0)),
                  pl.BlockSpec((1, W_A), vec),
                  pl.BlockSpec((n_seq, W_B), tok_blk), mod(2), mod(4), mod(3),
                  pl.BlockSpec((D_MODEL, D_MODEL), vec), pl.BlockSpec((1, W_B), vec),
                  pl.BlockSpec((1, D_MODEL), vec), pl.BlockSpec((1, D_MODEL), vec)],
        out_specs=(out_blk, out_blk),
        out_shape=(jax.ShapeDtypeStruct((n_tok * n_seq, D_MODEL), F32),
                   jax.ShapeDtypeStruct((n_tok * n_seq, D_MODEL), BF16)),
        compiler_params=_params(1),
        name="mix_sample",
    )(x2d, u, va, ws_tok, bs_tok, norm_a_g, attn2d, mod_s, mod_s, mod_s, w_out, norm_b_g, ln1_g, ln1_b)


def _ffn_act(conv_a, conv_b):
    return (jax.nn.silu(conv_a) * conv_b).astype(BF16)


def _ffn_prompt_kernel(tiles_per_seq, h_ref, halo_ref, x1_ref, g2_ref, wa_ref, wb_ref, wd_ref, wc_ref, bc_ref,
                       l2g_ref, l2b_ref, y_ref, st_ref, acc_scr, h_scr):
    i = pl.program_id(0)
    j = pl.program_id(1)
    tm = h_ref.shape[0]
    n_halo = halo_ref.shape[0]

    @pl.when(j == 0)
    def _():
        seq_start = (i % tiles_per_seq) == 0
        h_scr[0:n_halo, :] = jnp.where(seq_start, jnp.zeros_like(halo_ref[...]), halo_ref[...])
        h_scr[n_halo:n_halo + tm, :] = h_ref[...]
        acc_scr[...] = jnp.zeros_like(acc_scr)

    n_parts = tm // FFN_ROWS
    assert n_parts * FFN_ROWS == tm

    def up_proj(r, prev):
        if prev is None:
            h = h_scr[0:n_halo + FFN_ROWS, :]
            return _dot(h, wa_ref[...]), _dot(h, wb_ref[...])
        h = h_scr[n_halo + r * FFN_ROWS:n_halo + (r + 1) * FFN_ROWS, :]
        return tuple(jnp.concatenate([p[FFN_ROWS:, :], _dot(h, w_ref[...])], axis=0)
                     for p, w_ref in zip(prev, (wa_ref, wb_ref)))

    def conv(half, up, last):
        if last:
            st_ref[half] = up[n_halo + FFN_ROWS - SUBLANES:n_halo + FFN_ROWS, :]
        out = bc_ref[half:half + 1, :]
        for c in range(CONV_W):
            lag = CONV_W - 1 - c
            out = out + up[n_halo - lag:n_halo - lag + FFN_ROWS, :] * wc_ref[half, c:c + 1, :]
        return out

    def down(r, act):
        acc_scr[r * FFN_ROWS:(r + 1) * FFN_ROWS, :] += _dot(act, wd_ref[...])

    ups = up_proj(0, None)
    for r in range(n_parts):
        nxt = up_proj(r + 1, ups) if r + 1 < n_parts else None
        act = _ffn_act(conv(0, ups[0], r == n_parts - 1), conv(1, ups[1], r == n_parts - 1))
        down(r, act)
        ups = nxt

    @pl.when(j == pl.num_programs(1) - 1)
    def _():
        y_ref[...] = _layer_norm(ALPHA * x1_ref[...] + g2_ref[...] * acc_scr[...], l2g_ref[...], l2b_ref[...])


def _ffn_prompt(h2, x1, mod_p, w_up_a, w_up_b, w_down, w_conv, b_conv, ln2_g, ln2_b, *, tm, seq):
    rows = h2.shape[0]
    tiles_per_seq = seq // tm
    row_blk = lambda i, j: (i, 0)
    vec = lambda i, j: (0, 0)
    halo_blk = lambda i, j: (jnp.maximum(i * (tm // BF16_ROWS) - 1, 0), 0)
    return pl.pallas_call(
        functools.partial(_ffn_prompt_kernel, tiles_per_seq),
        grid=(rows // tm, N_FF_TILES),
        in_specs=[pl.BlockSpec((tm, D_MODEL), row_blk), pl.BlockSpec((BF16_ROWS, D_MODEL), halo_blk),
                  pl.BlockSpec((tm, D_MODEL), row_blk),
                  pl.BlockSpec((None, 1, D_MODEL), lambda i, j: (i // tiles_per_seq, 0, 5)),
                  pl.BlockSpec((D_MODEL, FF_TILE), lambda i, j: (0, j)),
                  pl.BlockSpec((D_MODEL, FF_TILE), lambda i, j: (0, j)),
                  pl.BlockSpec((FF_TILE, D_MODEL), lambda i, j: (j, 0)),
                  pl.BlockSpec((2, CONV_W, FF_TILE), lambda i, j: (0, 0, j)),
                  pl.BlockSpec((2, FF_TILE), lambda i, j: (0, j)),
                  pl.BlockSpec((1, D_MODEL), vec), pl.BlockSpec((1, D_MODEL), vec)],
        out_specs=(pl.BlockSpec((tm, D_MODEL), row_blk),
                   pl.BlockSpec((None, 2, SUBLANES, FF_TILE), lambda i, j: (i, 0, 0, j))),
        out_shape=(jax.ShapeDtypeStruct((rows, D_MODEL), F32),
                   jax.ShapeDtypeStruct((rows // tm, 2, SUBLANES, D_FF_PAD), F32)),
        scratch_shapes=[pltpu.VMEM((tm, D_MODEL), F32), pltpu.VMEM((BF16_ROWS + tm, D_MODEL), BF16)],
        compiler_params=_params(2),
        name="ffn_prompt",
    )(h2, h2, x1, mod_p, w_up_a, w_up_b, w_down, w_conv, b_conv, ln2_g, ln2_b)


def _ffn_sample_kernel(n_tok, h_ref, x1_ref, g2_ref, wa_ref, wb_ref, wd_ref, wc_ref, bc_ref, past_ref,
                       l2g_ref, l2b_ref, y_ref, st_ref, acc_scr):
    j = pl.program_id(0)
    n_seq = g2_ref.shape[0]
    h = h_ref[...]

    def conv(half, w_ref):
        up = _dot(h, w_ref[...])
        full = [past_ref[r, half] for r in range(CONV_W - 1)]
        full += [up[t * n_seq:(t + 1) * n_seq, :] for t in range(n_tok)]
        for r in range(CONV_W - 1):
            st_ref[r, half] = full[n_tok + r]
        w = wc_ref[half]
        b = bc_ref[half:half + 1, :]
        return jnp.concatenate(
            [b + sum(full[t + c] * w[c:c + 1, :] for c in range(CONV_W)) for t in range(n_tok)], axis=0)

    @pl.when(j == 0)
    def _():
        acc_scr[...] = jnp.zeros_like(acc_scr)

    acc_scr[...] += _dot(_ffn_act(conv(0, wa_ref), conv(1, wb_ref)), wd_ref[...])

    @pl.when(j == pl.num_programs(0) - 1)
    def _():
        g2 = g2_ref[...]
        for t in range(n_tok):
            rs = slice(t * n_seq, (t + 1) * n_seq)
            y_ref[:, t * D_MODEL:(t + 1) * D_MODEL] = _layer_norm(
                ALPHA * x1_ref[rs, :] + g2 * acc_scr[rs, :], l2g_ref[...], l2b_ref[...])


def _ffn_sample(h2, x1, mod_s, w_up_a, w_up_b, w_down, w_conv, b_conv, past, ln2_g, ln2_b, *, n_seq, n_tok):
    rows = n_seq * n_tok
    full = lambda j: (0, 0)
    state_blk = pl.BlockSpec((CONV_W - 1, 2, n_seq, FF_TILE), lambda j: (0, 0, 0, j))
    return pl.pallas_call(
        functools.partial(_ffn_sample_kernel, n_tok),
        grid=(N_FF_TILES,),
        in_specs=[pl.BlockSpec((rows, D_MODEL), full), pl.BlockSpec((rows, D_MODEL), full),
                  pl.BlockSpec((n_seq, D_MODEL), lambda j: (0, 5)),
                  pl.BlockSpec((D_MODEL, FF_TILE), lambda j: (0, j)),
                  pl.BlockSpec((D_MODEL, FF_TILE), lambda j: (0, j)),
                  pl.BlockSpec((FF_TILE, D_MODEL), lambda j: (j, 0)),
                  pl.BlockSpec((2, CONV_W, FF_TILE), lambda j: (0, 0, j)),
                  pl.BlockSpec((2, FF_TILE), lambda j: (0, j)),
                  state_blk,
                  pl.BlockSpec((1, D_MODEL), full), pl.BlockSpec((1, D_MODEL), full)],
        out_specs=(pl.BlockSpec((n_seq, n_tok * D_MODEL), full), state_blk),
        out_shape=(jax.ShapeDtypeStruct((n_seq, n_tok * D_MODEL), F32),
                   jax.ShapeDtypeStruct((CONV_W - 1, 2, n_seq, D_FF_PAD), F32)),
        scratch_shapes=[pltpu.VMEM((rows, D_MODEL), F32)],
        compiler_params=_params(1),
        name="ffn_sample",
    )(h2, x1, mod_s, w_up_a, w_up_b, w_down, w_conv, b_conv, past, ln2_g, ln2_b)


def _t5_bucket_table(n_rel):
    max_exact = NUM_BUCKETS // 2
    ratio = MAX_DISTANCE // max_exact
    log_ratio = ratio.bit_length() - 1
    assert 1 << log_ratio == ratio
    steps = NUM_BUCKETS - max_exact
    out = np.zeros((n_rel,), np.int32)
    for n in range(n_rel):
        if n < max_exact:
            out[n] = n
            continue
        k = 0
        while k + 1 < steps and n ** steps >= (max_exact ** steps) << (log_ratio * (k + 1)):
            k += 1
        out[n] = min(max_exact + k, NUM_BUCKETS - 1)
    return out


def _bias_tables(rel_bias, seq, past, n_tok):
    n_rel = max(seq, past + n_tok)
    onehot = np.zeros((NUM_BUCKETS, n_rel), np.float32)
    onehot[_t5_bucket_table(n_rel), np.arange(n_rel)] = 1.0
    tab = jnp.dot(rel_bias.astype(F32).T, onehot, precision=lax.Precision.HIGHEST)
    n_blocks = seq // MOBA_BLOCK
    tab_pad = jnp.pad(tab, ((0, 0), (MOBA_BLOCK, 1)))
    rev = jnp.stack([tab_pad[:, d * MOBA_BLOCK + 1:(d + 2) * MOBA_BLOCK + 1][:, ::-1]
                     for d in range(n_blocks)], axis=1)
    same_head = np.eye(N_HEADS_B, dtype=bool)
    by_key = jnp.stack([tab[:, t + 1:t + 1 + past][:, ::-1] for t in range(n_tok)], axis=0)
    n_pages = past // PAGE_SIZE
    spread = np.repeat(np.eye(PAGE_SIZE, dtype=np.float32), N_HEADS_B, axis=1)
    by_col = jnp.einsum('rpk,kc->rpc', by_key.reshape(n_tok * N_HEADS_B, n_pages, PAGE_SIZE), spread,
                        precision=lax.Precision.HIGHEST).reshape(n_tok * N_HEADS_B, past * N_HEADS_B)
    row_head = np.tile(np.arange(N_HEADS_B), n_tok)[:, None]
    col_head = np.tile(np.arange(N_HEADS_B), past)[None, :]
    bias_past = jnp.where(row_head == col_head, by_col, NEG_INF)
    new_rows = []
    for t in range(n_tok):
        by_tok = jnp.stack([tab[:, max(t - s, 0)] for s in range(n_tok)], axis=1)
        ok = same_head[:, None, :] & (np.arange(n_tok) <= t)[None, :, None]
        new_rows.append(jnp.where(ok, by_tok[:, :, None], NEG_INF).reshape(N_HEADS_B, n_tok * N_HEADS_B))
    bias_new = jnp.stack(new_rows, axis=0).reshape(n_tok * N_HEADS_B, n_tok * N_HEADS_B)
    return rev * LOG2_E, bias_past, bias_new


def _pad_ff(a, axis):
    pad = [(0, 0)] * a.ndim
    pad[axis] = (0, D_FF_PAD - D_FF)
    return jnp.pad(a, pad)


def kernel(x_prompt, x_sample, cache_k, cache_v, state_ffn_conv, page_table, c_prompt, c_sample,
           w_ada, b_ada, w_in, ln_sgu_g, ln_sgu_b, w_s, b_s, rel_bias, norm_a_g, norm_b_g, w_out,
           ln1_g, ln1_b, w_up, w_conv, b_conv, w_down, ln2_g, ln2_b):
    n_p, seq, _ = x_prompt.shape
    n_s, n_tok, _ = x_sample.shape
    n_pages = page_table.shape[1]
    past = n_pages * PAGE_SIZE
    assert w_ada.shape[0] == 1 and seq % MOBA_BLOCK == 0 and past % MOBA_BLOCK == 0 and n_tok <= CHUNK

    w_in_bf = w_in[0].astype(BF16)
    w_conv_p = _pad_ff(jnp.transpose(w_conv[0].reshape(CONV_W, 2, D_FF), (1, 0, 2)), 2)
    b_conv_p = _pad_ff(b_conv[0].reshape(2, D_FF), 1)
    row = lambda a: a.reshape(1, -1)
    ln_g, ln_b = row(ln_sgu_g[0]), row(ln_sgu_b[0])
    nag, nbg = row(norm_a_g[0]), row(norm_b_g[0])
    l1g, l1b, l2g, l2b = row(ln1_g[0]), row(ln1_b[0]), row(ln2_g[0]), row(ln2_b[0])
    bs_full = jnp.broadcast_to(b_s[0][:, :, None], (N_GROUPS_A, CHUNK, HEAD_DIM))
    tril_tok = np.tril(np.ones((n_tok, n_tok), np.float32))
    ws_tok = jnp.repeat(jnp.transpose(w_s[0][:, :n_tok, :n_tok] * tril_tok, (1, 2, 0)), HEAD_DIM, axis=2)
    bs_tok = jnp.repeat(b_s[0][:, :n_tok].T, HEAD_DIM, axis=1).reshape(n_tok, 1, W_A)
    rev_tab, bias_past, bias_new = _bias_tables(rel_bias, seq, past, n_tok)

    pad_rows = (-(n_s + n_p)) % SUBLANES
    c_all = jnp.concatenate([c_sample, c_prompt, jnp.zeros((pad_rows, D_MODEL), F32)], axis=0)
    mod = _ada(c_all, w_ada[0], b_ada[0])
    mod_s = mod[:n_s]
    mod_p = mod[n_s:n_s + n_p].reshape(n_p, 1, 6 * D_MODEL)

    xp = x_prompt.reshape(n_p * seq, D_MODEL)
    q_hi, q_lo, k_p, v_p, k_bf, v_bf, k_mean, gm_n, w_up_a, w_up_b, w_down_bf, w_out_bf = _proj_prompt(
        xp, mod_p, w_in_bf, ln_g, ln_b, w_s[0], bs_full, nag, w_up[0], w_down[0], w_out[0], tm=256, seq=seq)
    attn_p = _attn_prompt(q_hi, q_lo, k_bf, v_bf, k_mean.reshape(n_p, seq // MOBA_BLOCK, W_B), rev_tab,
                          n_seq=n_p, seq=seq)
    x1_p, h2_p = _mix_prompt(xp, gm_n, attn_p, mod_p, w_out_bf, nbg, l1g, l1b, tm=512, seq=seq)
    y_p, st_p = _ffn_prompt(h2_p, x1_p, mod_p, w_up_a, w_up_b, w_down_bf, w_conv_p, b_conv_p, l2g, l2b,
                            tm=512, seq=seq)
    tiles_per_seq = st_p.shape[0] // n_p
    conv_p = st_p[tiles_per_seq - 1::tiles_per_seq, :, SUBLANES - (CONV_W - 1):, :D_FF]
    conv_p = jnp.transpose(conv_p, (0, 2, 1, 3)).reshape(1, n_p, CONV_W - 1, 2 * D_FF)

    xs = x_sample.reshape(n_s, n_tok * D_MODEL)
    q_s, k_s, v_s, u_s, va_s = _proj_sample(xs, mod_s, w_in_bf, ln_g, ln_b, n_seq=n_s, n_tok=n_tok)
    heads = lambda a: a.reshape(n_s, n_tok, N_HEADS_B, HEAD_DIM)
    attn_s = _attn_sample(page_table, heads(q_s), heads(k_s), heads(v_s), bias_past, bias_new, cache_k, cache_v)
    x1_s, h2_s = _mix_sample(xs, u_s, va_s, ws_tok, bs_tok, nag, attn_s.reshape(n_s, n_tok * W_B), mod_s,
                             w_out_bf, nbg, l1g, l1b, n_seq=n_s, n_tok=n_tok)
    past_up = state_ffn_conv[0].reshape(n_s, CONV_W - 1, 2, D_FF)
    past_up = _pad_ff(jnp.transpose(past_up, (1, 2, 0, 3)), 3)
    y_s, st_s = _ffn_sample(h2_s, x1_s, mod_s, w_up_a, w_up_b, w_down_bf, w_conv_p, b_conv_p, past_up,
                            l2g, l2b, n_seq=n_s, n_tok=n_tok)
    conv_s = jnp.transpose(st_s[..., :D_FF], (2, 0, 1, 3)).reshape(1, n_s, CONV_W - 1, 2 * D_FF)

    heads_p = (1, n_p, seq, N_HEADS_B, HEAD_DIM)
    heads_s = (1, n_s, n_tok, N_HEADS_B, HEAD_DIM)
    return (y_p.reshape(n_p, seq, D_MODEL), y_s.reshape(n_s, n_tok, D_MODEL),
            k_p.reshape(heads_p), v_p.reshape(heads_p), conv_p,
            k_s.reshape(heads_s), v_s.reshape(heads_s), va_s.reshape(1, n_s, n_tok, W_A), conv_s)
```

```python
import functools

import numpy as np
import jax
import jax.numpy as jnp
from jax import lax
from jax.experimental import pallas as pl
from jax.experimental.pallas import tpu as pltpu

F32 = jnp.float32
BF16 = jnp.bfloat16

D_MODEL = 2048
HEAD_DIM = 128
N_HEADS_B = 8
N_GROUPS_A = 8
W_B = N_HEADS_B * HEAD_DIM
W_A = N_GROUPS_A * HEAD_DIM
CHUNK = 128
MOBA_BLOCK = 256
MOBA_TOPK = 3
NUM_BUCKETS = 32
MAX_DISTANCE = 1024
CONV_W = 3
D_FF = 5504
PAGE_SIZE = 128
ALPHA = 2.0 ** 0.25
LN_EPS = 1e-5
NEG_INF = -1e30
LOG2_E = 1.4426950408889634

FF_TILE = 512
D_FF_PAD = ((D_FF + FF_TILE - 1) // FF_TILE) * FF_TILE
N_FF_TILES = D_FF_PAD // FF_TILE
SUBLANES = 8
BF16_ROWS = 16
FFN_ROWS = 256
MIX_ROWS = 256
VMEM_LIMIT_BYTES = 56 * 1024 * 1024

_NT = (((1,), (1,)), ((), ()))


def _params(n_axes, flags=None):
    return pltpu.CompilerParams(dimension_semantics=("arbitrary",) * n_axes,
                                vmem_limit_bytes=VMEM_LIMIT_BYTES, flags=flags)


def _layer_norm(x, g, b):
    mu = jnp.mean(x, axis=-1, keepdims=True)
    xc = x - mu
    var = jnp.mean(xc * xc, axis=-1, keepdims=True)
    return xc * lax.rsqrt(var + LN_EPS) * g + b


def _rms_norm(x, g):
    return x * lax.rsqrt(jnp.mean(x * x, axis=-1, keepdims=True) + LN_EPS) * g


def _split_bf16(x):
    hi = x.astype(BF16)
    lo = (x - hi.astype(F32)).astype(BF16)
    return hi, lo


def _dot(a, b):
    return jnp.dot(a, b, preferred_element_type=F32)


def _dot_nt(a, b):
    return lax.dot_general(a, b, _NT, preferred_element_type=F32)


def _topk_select(s, n_valid, axis):
    idx = lax.broadcasted_iota(jnp.int32, s.shape, axis)
    rank = jnp.zeros(s.shape, F32)
    for i in range(n_valid):
        si = lax.slice_in_dim(s, i, i + 1, axis=axis)
        beats = (si > s) | ((si == s) & (i < idx))
        rank = rank + beats.astype(F32)
    return ((rank < MOBA_TOPK) & (idx < n_valid)).astype(F32)


def _ada_kernel(c_ref, w_ref, b_ref, o_ref):
    c = c_ref[...]
    a = jax.nn.silu(c).astype(BF16)
    o_ref[...] = _dot(a, w_ref[...].astype(BF16)) + b_ref[...]


def _ada(c_all, w_ada, b_ada):
    rows = c_all.shape[0]
    n_out = w_ada.shape[1]
    tn = 1024
    return pl.pallas_call(
        _ada_kernel,
        grid=(n_out // tn,),
        in_specs=[pl.BlockSpec((rows, D_MODEL), lambda j: (0, 0)),
                  pl.BlockSpec((D_MODEL, tn), lambda j: (0, j)),
                  pl.BlockSpec((1, tn), lambda j: (0, j))],
        out_specs=pl.BlockSpec((rows, tn), lambda j: (0, j)),
        out_shape=jax.ShapeDtypeStruct((rows, n_out), F32),
        compiler_params=_params(1),
        name="ada",
    )(c_all, w_ada, b_ada.reshape(1, n_out))


_COL_Q, _COL_K, _COL_V, _COL_U, _COL_VA = range(5)


def _proj_prompt_kernel(x_ref, sc_ref, sh_ref, w_ref, lng_ref, lnb_ref, ws_ref, bs_ref, nag_ref,
                        wup_ref, wdn_ref, wout_ref,
                        qhi_ref, qlo_ref, k_ref, v_ref, kbf_ref, vbf_ref, km_ref, gm_ref,
                        wa_ref, wb_ref, wd_ref, wo_ref):
    tm = x_ref.shape[0]
    w_up = wup_ref[...]
    ff_pad = jnp.zeros((w_up.shape[0], D_FF_PAD - D_FF), BF16)
    wa_ref[...] = jnp.concatenate([w_up[:, :D_FF].astype(BF16), ff_pad], axis=1)
    wb_ref[...] = jnp.concatenate([w_up[:, D_FF:].astype(BF16), ff_pad], axis=1)
    dn_row = pl.program_id(0) * wdn_ref.shape[0] + lax.broadcasted_iota(jnp.int32, wdn_ref.shape, 0)
    wd_ref[...] = jnp.where(dn_row < D_FF, wdn_ref[...], 0.0).astype(BF16)
    wo_ref[...] = wout_ref[...].astype(BF16)

    h = (x_ref[...] * (1.0 + sc_ref[...]) + sh_ref[...]).astype(BF16)

    def proj(c):
        return _dot(h, w_ref[:, c * W_B:(c + 1) * W_B])

    def gmlp(u, acc_va):
        va = _layer_norm(jax.nn.gelu(acc_va), lng_ref[...], lnb_ref[...]).astype(BF16)
        n_chunks = tm // CHUNK
        row = lax.broadcasted_iota(jnp.int32, (CHUNK, CHUNK), 0)
        col = lax.broadcasted_iota(jnp.int32, (CHUNK, CHUNK), 1)
        groups = []
        for g in range(N_GROUPS_A):
            gs = slice(g * HEAD_DIM, (g + 1) * HEAD_DIM)
            w_g = jnp.where(row >= col, ws_ref[g], 0.0).astype(BF16)
            v_g = jnp.concatenate([va[c * CHUNK:(c + 1) * CHUNK, gs] for c in range(n_chunks)], axis=1)
            m_g = _dot(w_g, v_g)
            mixed = jnp.concatenate([m_g[:, c * HEAD_DIM:(c + 1) * HEAD_DIM] + bs_ref[g]
                                     for c in range(n_chunks)], axis=0)
            groups.append(u[:, gs] * mixed)
        gm_ref[...] = _rms_norm(jnp.concatenate(groups, axis=1), nag_ref[...]).astype(BF16)

    def put_q(acc):
        q_hi, q_lo = _split_bf16(acc * (LOG2_E * HEAD_DIM ** -0.5))
        qhi_ref[...] = q_hi
        qlo_ref[...] = q_lo

    def put_k(acc):
        k_ref[...] = acc
        kbf_ref[...] = acc.astype(BF16)
        km_ref[...] = jnp.mean(acc.reshape(tm // MOBA_BLOCK, MOBA_BLOCK, W_B), axis=1)

    def put_v(acc):
        v_ref[...] = acc
        vbf_ref[...] = acc.astype(BF16)

    acc_u = proj(_COL_U)
    acc_va = proj(_COL_VA)
    u = jax.nn.gelu(acc_u)
    acc_q = proj(_COL_Q)
    gmlp(u, acc_va)
    acc_k = proj(_COL_K)
    put_q(acc_q)
    acc_v = proj(_COL_V)
    put_k(acc_k)
    put_v(acc_v)


def _proj_prompt(x, mod_p, w_in, ln_g, ln_b, w_s, bs_full, norm_a_g, w_up, w_down, w_out, *, tm, seq):
    rows = x.shape[0]
    tiles_per_seq = seq // tm
    n_steps = rows // tm
    up_rows, dn_rows, out_rows = w_up.shape[0] // n_steps, D_FF_PAD // n_steps, w_out.shape[0] // n_steps
    assert up_rows * n_steps == w_up.shape[0] and dn_rows * n_steps == D_FF_PAD
    assert out_rows * n_steps == w_out.shape[0] and (n_steps - 1) * dn_rows < w_down.shape[0]
    assert up_rows % BF16_ROWS == 0 and dn_rows % BF16_ROWS == 0 and out_rows % BF16_ROWS == 0
    row_blk = lambda i: (i, 0)
    vec = lambda i: (0, 0)
    whole3 = lambda i: (0, 0, 0)
    out_shape = (
        jax.ShapeDtypeStruct((rows, W_B), BF16),
        jax.ShapeDtypeStruct((rows, W_B), BF16),
        jax.ShapeDtypeStruct((rows, W_B), F32),
        jax.ShapeDtypeStruct((rows, W_B), F32),
        jax.ShapeDtypeStruct((rows, W_B), BF16),
        jax.ShapeDtypeStruct((rows, W_B), BF16),
        jax.ShapeDtypeStruct((rows // tm, tm // MOBA_BLOCK, W_B), F32),
        jax.ShapeDtypeStruct((rows, W_A), BF16),
        jax.ShapeDtypeStruct((w_up.shape[0], D_FF_PAD), BF16),
        jax.ShapeDtypeStruct((w_up.shape[0], D_FF_PAD), BF16),
        jax.ShapeDtypeStruct((D_FF_PAD, w_down.shape[1]), BF16),
        jax.ShapeDtypeStruct(w_out.shape, BF16),
    )
    out_specs = (
        pl.BlockSpec((tm, W_B), row_blk), pl.BlockSpec((tm, W_B), row_blk),
        pl.BlockSpec((tm, W_B), row_blk), pl.BlockSpec((tm, W_B), row_blk),
        pl.BlockSpec((tm, W_B), row_blk), pl.BlockSpec((tm, W_B), row_blk),
        pl.BlockSpec((None, tm // MOBA_BLOCK, W_B), lambda i: (i, 0, 0)),
        pl.BlockSpec((tm, W_A), row_blk),
        pl.BlockSpec((up_rows, D_FF_PAD), row_blk), pl.BlockSpec((up_rows, D_FF_PAD), row_blk),
        pl.BlockSpec((dn_rows, w_down.shape[1]), row_blk),
        pl.BlockSpec((out_rows, w_out.shape[1]), row_blk),
    )
    return pl.pallas_call(
        _proj_prompt_kernel,
        grid=(rows // tm,),
        in_specs=[pl.BlockSpec((tm, D_MODEL), row_blk),
                  pl.BlockSpec((None, 1, D_MODEL), lambda i: (i // tiles_per_seq, 0, 1)),
                  pl.BlockSpec((None, 1, D_MODEL), lambda i: (i // tiles_per_seq, 0, 0)),
                  pl.BlockSpec(w_in.shape, vec, pipeline_mode=pl.Buffered(1)),
                  pl.BlockSpec((1, W_A), vec), pl.BlockSpec((1, W_A), vec),
                  pl.BlockSpec((N_GROUPS_A, CHUNK, CHUNK), whole3),
                  pl.BlockSpec((N_GROUPS_A, CHUNK, HEAD_DIM), whole3),
                  pl.BlockSpec((1, W_A), vec),
                  pl.BlockSpec((up_rows, w_up.shape[1]), row_blk),
                  pl.BlockSpec((dn_rows, w_down.shape[1]), row_blk),
                  pl.BlockSpec((out_rows, w_out.shape[1]), row_blk)],
        out_specs=out_specs,
        out_shape=out_shape,
        compiler_params=_params(1),
        name="proj_prompt",
    )(x, mod_p, mod_p, w_in, ln_g, ln_b, w_s, bs_full, norm_a_g, w_up, w_down, w_out)


def _proj_sample_kernel(n_tok, x_ref, sc_ref, sh_ref, w_ref, lng_ref, lnb_ref,
                        q_ref, k_ref, v_ref, u_ref, va_ref, h_scr):
    j = pl.program_id(0)
    n_seq = sc_ref.shape[0]

    @pl.when(j == 0)
    def _():
        for t in range(n_tok):
            x_t = x_ref[:, t * D_MODEL:(t + 1) * D_MODEL]
            h_scr[t * n_seq:(t + 1) * n_seq, :] = (x_t * (1.0 + sc_ref[...]) + sh_ref[...]).astype(BF16)

    acc = _dot(h_scr[...], w_ref[...])

    def put(o_ref, val):
        for t in range(n_tok):
            o_ref[:, t * W_B:(t + 1) * W_B] = val[t * n_seq:(t + 1) * n_seq, :]

    @pl.when(j == _COL_Q)
    def _():
        put(q_ref, acc * (HEAD_DIM ** -0.5))

    @pl.when(j == _COL_K)
    def _():
        put(k_ref, acc)

    @pl.when(j == _COL_V)
    def _():
        put(v_ref, acc)

    @pl.when(j == _COL_U)
    def _():
        put(u_ref, jax.nn.gelu(acc))

    @pl.when(j == _COL_VA)
    def _():
        put(va_ref, _layer_norm(jax.nn.gelu(acc), lng_ref[...], lnb_ref[...]))


def _proj_sample(x2d, mod_s, w_in, ln_g, ln_b, *, n_seq, n_tok):
    n_col = w_in.shape[1] // W_B
    whole = lambda j: (0, 0)
    out = jax.ShapeDtypeStruct((n_seq, n_tok * W_B), F32)
    return pl.pallas_call(
        functools.partial(_proj_sample_kernel, n_tok),
        grid=(n_col,),
        in_specs=[pl.BlockSpec((n_seq, n_tok * D_MODEL), whole),
                  pl.BlockSpec((n_seq, D_MODEL), lambda j: (0, 1)),
                  pl.BlockSpec((n_seq, D_MODEL), lambda j: (0, 0)),
                  pl.BlockSpec((D_MODEL, W_B), lambda j: (0, j)),
                  pl.BlockSpec((1, W_A), whole), pl.BlockSpec((1, W_A), whole)],
        out_specs=tuple(pl.BlockSpec((n_seq, n_tok * W_B), whole) for _ in range(5)),
        out_shape=(out,) * 5,
        scratch_shapes=[pltpu.VMEM((n_tok * n_seq, D_MODEL), BF16)],
        compiler_params=_params(1),
        name="proj_sample",
    )(x2d, mod_s, mod_s, w_in, ln_g, ln_b)


def _attn_prompt_kernel(qhi_ref, qlo_ref, k_ref, v_ref, km_ref, rev_ref, o_ref, bias_scr):
    n_blocks = k_ref.shape[0] // MOBA_BLOCK

    @pl.when(pl.program_id(1) == 0)
    def _():
        for d in range(n_blocks):
            r = jnp.broadcast_to(rev_ref[d:d + 1, :], (MOBA_BLOCK, 2 * MOBA_BLOCK))
            rolled = pltpu.roll(r, 0, 1, stride=1, stride_axis=0)
            bias_scr[d] = rolled[:, MOBA_BLOCK:]

    km_hi, km_lo = _split_bf16(km_ref[...])
    row = lax.broadcasted_iota(jnp.int32, (MOBA_BLOCK, MOBA_BLOCK), 0)
    col = lax.broadcasted_iota(jnp.int32, (MOBA_BLOCK, MOBA_BLOCK), 1)

    def rows_of(qb):
        return slice(qb * MOBA_BLOCK, (qb + 1) * MOBA_BLOCK)

    def qk(qb):
        return _dot_nt(qhi_ref[rows_of(qb), :], k_ref[0:(qb + 1) * MOBA_BLOCK, :])

    def softmax(qb, logits):
        sel = None
        if qb > MOBA_TOPK:
            q_hi, q_lo = qhi_ref[rows_of(qb), :], qlo_ref[rows_of(qb), :]
            s_t = _dot_nt(km_hi, q_hi) + _dot_nt(km_lo, q_hi) + _dot_nt(km_hi, q_lo)
            sel = _topk_select(s_t, qb, axis=0).T
        pieces = []
        for jb in range(qb + 1):
            lj = logits[:, jb * MOBA_BLOCK:(jb + 1) * MOBA_BLOCK] + bias_scr[qb - jb]
            if jb == qb:
                lj = jnp.where(row >= col, lj, NEG_INF)
            elif sel is not None:
                lj = jnp.where(sel[:, jb:jb + 1] > 0.0, lj, NEG_INF)
            pieces.append(lj)
        lg = jnp.concatenate(pieces, axis=1) if len(pieces) > 1 else pieces[0]
        m = jnp.max(lg, axis=-1, keepdims=True)
        p = jnp.exp2(lg - m)
        return p.astype(BF16), jnp.sum(p, axis=-1, keepdims=True)

    def pv(qb, p, z):
        o_ref[rows_of(qb), :] = _dot(p, v_ref[0:(qb + 1) * MOBA_BLOCK, :]) / z

    logits = qk(0)
    prev = None
    for qb in range(n_blocks):
        nxt = qk(qb + 1) if qb + 1 < n_blocks else None
        p, z = softmax(qb, logits)
        if prev is not None:
            pv(*prev)
        prev = (qb, p, z)
        logits = nxt
    pv(*prev)


def _attn_prompt(q_hi, q_lo, k_bf, v_bf, k_mean, rev_tab, *, n_seq, seq):
    n_blocks = seq // MOBA_BLOCK
    qkv = pl.BlockSpec((seq, HEAD_DIM), lambda h, n: (n, h))
    return pl.pallas_call(
        _attn_prompt_kernel,
        grid=(N_HEADS_B, n_seq),
        in_specs=[qkv, qkv, qkv, qkv,
                  pl.BlockSpec((None, n_blocks, HEAD_DIM), lambda h, n: (n, 0, h)),
                  pl.BlockSpec((None, n_blocks, 2 * MOBA_BLOCK), lambda h, n: (h, 0, 0))],
        out_specs=pl.BlockSpec((seq, HEAD_DIM), lambda h, n: (n, h)),
        out_shape=jax.ShapeDtypeStruct((n_seq * seq, W_B), F32),
        scratch_shapes=[pltpu.VMEM((n_blocks, MOBA_BLOCK, MOBA_BLOCK), F32)],
        compiler_params=_params(2),
        name="attn_prompt",
    )(q_hi, q_lo, k_bf, v_bf, k_mean, rev_tab)


def _attn_sample_kernel(n_pages, pt_ref, q_ref, kn_ref, vn_ref, bias_ref, biasn_ref, *refs):
    k_pages = refs[:n_pages]
    v_pages = refs[n_pages:2 * n_pages]
    o_ref = refs[2 * n_pages]
    n_tok = q_ref.shape[0]
    rows = n_tok * N_HEADS_B
    cols = PAGE_SIZE * N_HEADS_B
    pages_per_block = MOBA_BLOCK // PAGE_SIZE
    n_blocks = n_pages // pages_per_block

    q = q_ref[...].reshape(rows, HEAD_DIM)
    q_bf = q.astype(BF16)

    logit_pages = []
    k_sums = []
    for p in range(n_pages):
        kp = k_pages[p][...]
        k_sums.append(jnp.sum(kp, axis=0))
        logit_pages.append(_dot_nt(q_bf, kp.reshape(cols, HEAD_DIM).astype(BF16)))

    blk_lane = lax.broadcasted_iota(jnp.int32, (rows, n_blocks), 1)
    scores = jnp.zeros((rows, n_blocks), F32)
    for b in range(n_blocks):
        km = sum(k_sums[b * pages_per_block:(b + 1) * pages_per_block]) * (1.0 / MOBA_BLOCK)
        s_b = jnp.sum(q * jnp.concatenate([km] * n_tok, axis=0), axis=1, keepdims=True)
        scores = jnp.where(blk_lane == b, s_b, scores)
    sel = _topk_select(scores, n_blocks, axis=1)

    masked = []
    for p in range(n_pages):
        b = p // pages_per_block
        lp = logit_pages[p] + bias_ref[:, p * cols:(p + 1) * cols]
        masked.append(jnp.where(sel[:, b:b + 1] > 0.0, lp, NEG_INF))
    ln = _dot_nt(q_bf, kn_ref[...].reshape(rows, HEAD_DIM).astype(BF16)) + biasn_ref[...]

    m = jnp.maximum(jnp.max(functools.reduce(jnp.maximum, masked), axis=-1, keepdims=True),
                    jnp.max(ln, axis=-1, keepdims=True))
    p_new = jnp.exp(ln - m)
    acc = _dot(p_new.astype(BF16), vn_ref[...].reshape(rows, HEAD_DIM).astype(BF16))
    p_sum = None
    for p in range(n_pages):
        pp = jnp.exp(masked[p] - m)
        p_sum = pp if p_sum is None else p_sum + pp
        acc = acc + _dot(pp.astype(BF16), v_pages[p][...].reshape(cols, HEAD_DIM).astype(BF16))
    z = jnp.sum(p_sum, axis=-1, keepdims=True) + jnp.sum(p_new, axis=-1, keepdims=True)
    o_ref[...] = (acc / z).reshape(n_tok, N_HEADS_B, HEAD_DIM)


def _attn_sample(page_table, q, k_new, v_new, bias_past, bias_new, cache_k, cache_v):
    n_seq, n_pages = page_table.shape
    n_tok = q.shape[1]
    tok = pl.BlockSpec((None, n_tok, N_HEADS_B, HEAD_DIM), lambda n, pt: (n, 0, 0, 0))

    def page_spec(p):
        return pl.BlockSpec((None, None, PAGE_SIZE, N_HEADS_B, HEAD_DIM),
                            lambda n, pt: (0, pt[n * n_pages + p], 0, 0, 0))

    grid_spec = pltpu.PrefetchScalarGridSpec(
        num_scalar_prefetch=1,
        grid=(n_seq,),
        in_specs=[tok, tok, tok,
                  pl.BlockSpec(bias_past.shape, lambda n, pt: (0, 0)),
                  pl.BlockSpec(bias_new.shape, lambda n, pt: (0, 0))]
                 + [page_spec(p) for p in range(n_pages)] * 2,
        out_specs=tok,
    )
    return pl.pallas_call(
        functools.partial(_attn_sample_kernel, n_pages),
        grid_spec=grid_spec,
        out_shape=jax.ShapeDtypeStruct((n_seq, n_tok, N_HEADS_B, HEAD_DIM), F32),
        compiler_params=_params(1),
        name="attn_sample",
    )(page_table.reshape(-1), q, k_new, v_new, bias_past, bias_new,
      *([cache_k] * n_pages), *([cache_v] * n_pages))


def _mix_project(gm_n, attn, wo_ref, nbg_ref):
    a_n = _rms_norm(attn, nbg_ref[...]).astype(BF16)
    return _dot(gm_n, wo_ref[0:W_A, :]) + _dot(a_n, wo_ref[W_A:W_A + W_B, :])


def _mix_finish(x, o, g1, sc2, sh2, l1g_ref, l1b_ref):
    x1 = _layer_norm(ALPHA * x + g1 * o, l1g_ref[...], l1b_ref[...])
    return x1, (x1 * (1.0 + sc2) + sh2).astype(BF16)


def _mix_prompt_kernel(x_ref, gm_ref, at_ref, g1_ref, sc2_ref, sh2_ref, wo_ref, nbg_ref, l1g_ref, l1b_ref,
                       x1_ref, h2_ref):
    tm = x_ref.shape[0]
    parts = [slice(r, r + MIX_ROWS) for r in range(0, tm, MIX_ROWS)]
    outs = [_mix_project(gm_ref[rs, :], at_ref[rs, :], wo_ref, nbg_ref) for rs in parts]
    for rs, o in zip(parts, outs):
        x1_ref[rs, :], h2_ref[rs, :] = _mix_finish(x_ref[rs, :], o, g1_ref[...], sc2_ref[...], sh2_ref[...],
                                                   l1g_ref, l1b_ref)


def _mix_prompt(x, gm_n, attn, mod_p, w_out, norm_b_g, ln1_g, ln1_b, *, tm, seq):
    rows = x.shape[0]
    tiles_per_seq = seq // tm
    row_blk = lambda i: (i, 0)
    vec = lambda i: (0, 0)
    mod = lambda c: pl.BlockSpec((None, 1, D_MODEL), lambda i: (i // tiles_per_seq, 0, c))
    return pl.pallas_call(
        _mix_prompt_kernel,
        grid=(rows // tm,),
        in_specs=[pl.BlockSpec((tm, D_MODEL), row_blk), pl.BlockSpec((tm, W_A), row_blk),
                  pl.BlockSpec((tm, W_B), row_blk), mod(2), mod(4), mod(3),
                  pl.BlockSpec((D_MODEL, D_MODEL), vec), pl.BlockSpec((1, W_B), vec),
                  pl.BlockSpec((1, D_MODEL), vec), pl.BlockSpec((1, D_MODEL), vec)],
        out_specs=(pl.BlockSpec((tm, D_MODEL), row_blk), pl.BlockSpec((tm, D_MODEL), row_blk)),
        out_shape=(jax.ShapeDtypeStruct((rows, D_MODEL), F32), jax.ShapeDtypeStruct((rows, D_MODEL), BF16)),
        compiler_params=_params(1),
        name="mix_prompt",
    )(x, gm_n, attn, mod_p, mod_p, mod_p, w_out, norm_b_g, ln1_g, ln1_b)


def _mix_sample_kernel(x_ref, u_ref, va_ref, ws_ref, bs_ref, nag_ref, at_ref, g1_ref, sc2_ref, sh2_ref,
                       wo_ref, nbg_ref, l1g_ref, l1b_ref, x1_ref, h2_ref):
    n_tok = ws_ref.shape[0]
    mixed = bs_ref[...]
    for s in range(n_tok):
        mixed = mixed + ws_ref[s:s + 1, :] * va_ref[:, s * W_A:(s + 1) * W_A]
    gm_n = _rms_norm(u_ref[...] * mixed, nag_ref[...]).astype(BF16)
    o = _mix_project(gm_n, at_ref[...], wo_ref, nbg_ref)
    x1_ref[...], h2_ref[...] = _mix_finish(x_ref[...], o, g1_ref[...], sc2_ref[...], sh2_ref[...],
                                           l1g_ref, l1b_ref)


def _mix_sample(x2d, u, va, ws_tok, bs_tok, norm_a_g, attn2d, mod_s, w_out, norm_b_g, ln1_g, ln1_b,
                *, n_seq, n_tok):
    tok_blk = lambda t: (0, t)
    vec = lambda t: (0, 0)
    mod = lambda c: pl.BlockSpec((n_seq, D_MODEL), lambda t: (0, c))
    out_blk = pl.BlockSpec((n_seq, D_MODEL), lambda t: (t, 0))
    return pl.pallas_call(
        _mix_sample_kernel,
        grid=(n_tok,),
        in_specs=[pl.BlockSpec((n_seq, D_MODEL), tok_blk), pl.BlockSpec((n_seq, W_A), tok_blk),
                  pl.BlockSpec((n_seq, n_tok * W_A), vec),
                  pl.BlockSpec((None, n_tok, W_A), lambda t: (t, 0, 0)),
                  pl.BlockSpec((None, 1, W_A), lambda t: (t, 0, 0)),
                  pl.BlockSpec((1, W_A), vec),
                  pl.BlockSpec((n_seq, W_B), tok_blk), mod(2), mod(4), mod(3),
                  pl.BlockSpec((D_MODEL, D_MODEL), vec), pl.BlockSpec((1, W_B), vec),
                  pl.BlockSpec((1, D_MODEL), vec), pl.BlockSpec((1, D_MODEL), vec)],
        out_specs=(out_blk, out_blk),
        out_shape=(jax.ShapeDtypeStruct((n_tok * n_seq, D_MODEL), F32),
                   jax.ShapeDtypeStruct((n_tok * n_seq, D_MODEL), BF16)),
        compiler_params=_params(1),
        name="mix_sample",
    )(x2d, u, va, ws_tok, bs_tok, norm_a_g, attn2d, mod_s, mod_s, mod_s, w_out, norm_b_g, ln1_g, ln1_b)


def _ffn_act(conv_a, conv_b):
    return (jax.nn.silu(conv_a) * conv_b).astype(BF16)


def _ffn_prompt_kernel(tiles_per_seq, h_ref, halo_ref, x1_ref, g2_ref, wa_ref, wb_ref, wd_ref, wc_ref, bc_ref,
                       l2g_ref, l2b_ref, y_ref, st_ref, acc_scr, h_scr):
    i = pl.program_id(0)
    j = pl.program_id(1)
    tm = h_ref.shape[0]
    n_halo = halo_ref.shape[0]

    @pl.when(j == 0)
    def _():
        seq_start = (i % tiles_per_seq) == 0
        h_scr[0:n_halo, :] = jnp.where(seq_start, jnp.zeros_like(halo_ref[...]), halo_ref[...])
        h_scr[n_halo:n_halo + tm, :] = h_ref[...]
        acc_scr[...] = jnp.zeros_like(acc_scr)

    n_parts = tm // FFN_ROWS
    assert n_parts * FFN_ROWS == tm

    def up_proj(r, prev):
        if prev is None:
            h = h_scr[0:n_halo + FFN_ROWS, :]
            return _dot(h, wa_ref[...]), _dot(h, wb_ref[...])
        h = h_scr[n_halo + r * FFN_ROWS:n_halo + (r + 1) * FFN_ROWS, :]
        return tuple(jnp.concatenate([p[FFN_ROWS:, :], _dot(h, w_ref[...])], axis=0)
                     for p, w_ref in zip(prev, (wa_ref, wb_ref)))

    def conv(half, up, last):
        if last:
            st_ref[half] = up[n_halo + FFN_ROWS - SUBLANES:n_halo + FFN_ROWS, :]
        out = bc_ref[half:half + 1, :]
        for c in range(CONV_W):
            lag = CONV_W - 1 - c
            out = out + up[n_halo - lag:n_halo - lag + FFN_ROWS, :] * wc_ref[half, c:c + 1, :]
        return out

    def down(r, act):
        acc_scr[r * FFN_ROWS:(r + 1) * FFN_ROWS, :] += _dot(act, wd_ref[...])

    ups = up_proj(0, None)
    for r in range(n_parts):
        nxt = up_proj(r + 1, ups) if r + 1 < n_parts else None
        act = _ffn_act(conv(0, ups[0], r == n_parts - 1), conv(1, ups[1], r == n_parts - 1))
        down(r, act)
        ups = nxt

    @pl.when(j == pl.num_programs(1) - 1)
    def _():
        y_ref[...] = _layer_norm(ALPHA * x1_ref[...] + g2_ref[...] * acc_scr[...], l2g_ref[...], l2b_ref[...])


def _ffn_prompt(h2, x1, mod_p, w_up_a, w_up_b, w_down, w_conv, b_conv, ln2_g, ln2_b, *, tm, seq):
    rows = h2.shape[0]
    tiles_per_seq = seq // tm
    row_blk = lambda i, j: (i, 0)
    vec = lambda i, j: (0, 0)
    halo_blk = lambda i, j: (jnp.maximum(i * (tm // BF16_ROWS) - 1, 0), 0)
    return pl.pallas_call(
        functools.partial(_ffn_prompt_kernel, tiles_per_seq),
        grid=(rows // tm, N_FF_TILES),
        in_specs=[pl.BlockSpec((tm, D_MODEL), row_blk), pl.BlockSpec((BF16_ROWS, D_MODEL), halo_blk),
                  pl.BlockSpec((tm, D_MODEL), row_blk),
                  pl.BlockSpec((None, 1, D_MODEL), lambda i, j: (i // tiles_per_seq, 0, 5)),
                  pl.BlockSpec((D_MODEL, FF_TILE), lambda i, j: (0, j)),
                  pl.BlockSpec((D_MODEL, FF_TILE), lambda i, j: (0, j)),
                  pl.BlockSpec((FF_TILE, D_MODEL), lambda i, j: (j, 0)),
                  pl.BlockSpec((2, CONV_W, FF_TILE), lambda i, j: (0, 0, j)),
                  pl.BlockSpec((2, FF_TILE), lambda i, j: (0, j)),
                  pl.BlockSpec((1, D_MODEL), vec), pl.BlockSpec((1, D_MODEL), vec)],
        out_specs=(pl.BlockSpec((tm, D_MODEL), row_blk),
                   pl.BlockSpec((None, 2, SUBLANES, FF_TILE), lambda i, j: (i, 0, 0, j))),
        out_shape=(jax.ShapeDtypeStruct((rows, D_MODEL), F32),
                   jax.ShapeDtypeStruct((rows // tm, 2, SUBLANES, D_FF_PAD), F32)),
        scratch_shapes=[pltpu.VMEM((tm, D_MODEL), F32), pltpu.VMEM((BF16_ROWS + tm, D_MODEL), BF16)],
        compiler_params=_params(2),
        name="ffn_prompt",
    )(h2, h2, x1, mod_p, w_up_a, w_up_b, w_down, w_conv, b_conv, ln2_g, ln2_b)


def _ffn_sample_kernel(n_tok, h_ref, x1_ref, g2_ref, wa_ref, wb_ref, wd_ref, wc_ref, bc_ref, past_ref,
                       l2g_ref, l2b_ref, y_ref, st_ref, acc_scr):
    j = pl.program_id(0)
    n_seq = g2_ref.shape[0]
    h = h_ref[...]

    def conv(half, w_ref):
        up = _dot(h, w_ref[...])
        full = [past_ref[r, half] for r in range(CONV_W - 1)]
        full += [up[t * n_seq:(t + 1) * n_seq, :] for t in range(n_tok)]
        for r in range(CONV_W - 1):
            st_ref[r, half] = full[n_tok + r]
        w = wc_ref[half]
        b = bc_ref[half:half + 1, :]
        return jnp.concatenate(
            [b + sum(full[t + c] * w[c:c + 1, :] for c in range(CONV_W)) for t in range(n_tok)], axis=0)

    @pl.when(j == 0)
    def _():
        acc_scr[...] = jnp.zeros_like(acc_scr)

    acc_scr[...] += _dot(_ffn_act(conv(0, wa_ref), conv(1, wb_ref)), wd_ref[...])

    @pl.when(j == pl.num_programs(0) - 1)
    def _():
        g2 = g2_ref[...]
        for t in range(n_tok):
            rs = slice(t * n_seq, (t + 1) * n_seq)
            y_ref[:, t * D_MODEL:(t + 1) * D_MODEL] = _layer_norm(
                ALPHA * x1_ref[rs, :] + g2 * acc_scr[rs, :], l2g_ref[...], l2b_ref[...])


def _ffn_sample(h2, x1, mod_s, w_up_a, w_up_b, w_down, w_conv, b_conv, past, ln2_g, ln2_b, *, n_seq, n_tok):
    rows = n_seq * n_tok
    full = lambda j: (0, 0)
    state_blk = pl.BlockSpec((CONV_W - 1, 2, n_seq, FF_TILE), lambda j: (0, 0, 0, j))
    return pl.pallas_call(
        functools.partial(_ffn_sample_kernel, n_tok),
        grid=(N_FF_TILES,),
        in_specs=[pl.BlockSpec((rows, D_MODEL), full), pl.BlockSpec((rows, D_MODEL), full),
                  pl.BlockSpec((n_seq, D_MODEL), lambda j: (0, 5)),
                  pl.BlockSpec((D_MODEL, FF_TILE), lambda j: (0, j)),
                  pl.BlockSpec((D_MODEL, FF_TILE), lambda j: (0, j)),
                  pl.BlockSpec((FF_TILE, D_MODEL), lambda j: (j, 0)),
                  pl.BlockSpec((2, CONV_W, FF_TILE), lambda j: (0, 0, j)),
                  pl.BlockSpec((2, FF_TILE), lambda j: (0, j)),
                  state_blk,
                  pl.BlockSpec((1, D_MODEL), full), pl.BlockSpec((1, D_MODEL), full)],
        out_specs=(pl.BlockSpec((n_seq, n_tok * D_MODEL), full), state_blk),
        out_shape=(jax.ShapeDtypeStruct((n_seq, n_tok * D_MODEL), F32),
                   jax.ShapeDtypeStruct((CONV_W - 1, 2, n_seq, D_FF_PAD), F32)),
        scratch_shapes=[pltpu.VMEM((rows, D_MODEL), F32)],
        compiler_params=_params(1),
        name="ffn_sample",
    )(h2, x1, mod_s, w_up_a, w_up_b, w_down, w_conv, b_conv, past, ln2_g, ln2_b)


def _t5_bucket_table(n_rel):
    max_exact = NUM_BUCKETS // 2
    ratio = MAX_DISTANCE // max_exact
    log_ratio = ratio.bit_length() - 1
    assert 1 << log_ratio == ratio
    steps = NUM_BUCKETS - max_exact
    out = np.zeros((n_rel,), np.int32)
    for n in range(n_rel):
        if n < max_exact:
            out[n] = n
            continue
        k = 0
        while k + 1 < steps and n ** steps >= (max_exact ** steps) << (log_ratio * (k + 1)):
            k += 1
        out[n] = min(max_exact + k, NUM_BUCKETS - 1)
    return out


def _bias_tables(rel_bias, seq, past, n_tok):
    n_rel = max(seq, past + n_tok)
    onehot = np.zeros((NUM_BUCKETS, n_rel), np.float32)
    onehot[_t5_bucket_table(n_rel), np.arange(n_rel)] = 1.0
    tab = jnp.dot(rel_bias.astype(F32).T, onehot, precision=lax.Precision.HIGHEST)
    n_blocks = seq // MOBA_BLOCK
    tab_pad = jnp.pad(tab, ((0, 0), (MOBA_BLOCK, 1)))
    rev = jnp.stack([tab_pad[:, d * MOBA_BLOCK + 1:(d + 2) * MOBA_BLOCK + 1][:, ::-1]
                     for d in range(n_blocks)], axis=1)
    same_head = np.eye(N_HEADS_B, dtype=bool)
    by_key = jnp.stack([tab[:, t + 1:t + 1 + past][:, ::-1] for t in range(n_tok)], axis=0)
    n_pages = past // PAGE_SIZE
    spread = np.repeat(np.eye(PAGE_SIZE, dtype=np.float32), N_HEADS_B, axis=1)
    by_col = jnp.einsum('rpk,kc->rpc', by_key.reshape(n_tok * N_HEADS_B, n_pages, PAGE_SIZE), spread,
                        precision=lax.Precision.HIGHEST).reshape(n_tok * N_HEADS_B, past * N_HEADS_B)
    row_head = np.tile(np.arange(N_HEADS_B), n_tok)[:, None]
    col_head = np.tile(np.arange(N_HEADS_B), past)[None, :]
    bias_past = jnp.where(row_head == col_head, by_col, NEG_INF)
    new_rows = []
    for t in range(n_tok):
        by_tok = jnp.stack([tab[:, max(t - s, 0)] for s in range(n_tok)], axis=1)
        ok = same_head[:, None, :] & (np.arange(n_tok) <= t)[None, :, None]
        new_rows.append(jnp.where(ok, by_tok[:, :, None], NEG_INF).reshape(N_HEADS_B, n_tok * N_HEADS_B))
    bias_new = jnp.stack(new_rows, axis=0).reshape(n_tok * N_HEADS_B, n_tok * N_HEADS_B)
    return rev * LOG2_E, bias_past, bias_new


def _pad_ff(a, axis):
    pad = [(0, 0)] * a.ndim
    pad[axis] = (0, D_FF_PAD - D_FF)
    return jnp.pad(a, pad)


def kernel(x_prompt, x_sample, cache_k, cache_v, state_ffn_conv, page_table, c_prompt, c_sample,
           w_ada, b_ada, w_in, ln_sgu_g, ln_sgu_b, w_s, b_s, rel_bias, norm_a_g, norm_b_g, w_out,
           ln1_g, ln1_b, w_up, w_conv, b_conv, w_down, ln2_g, ln2_b):
    n_p, seq, _ = x_prompt.shape
    n_s, n_tok, _ = x_sample.shape
    n_pages = page_table.shape[1]
    past = n_pages * PAGE_SIZE
    assert w_ada.shape[0] == 1 and seq % MOBA_BLOCK == 0 and past % MOBA_BLOCK == 0 and n_tok <= CHUNK
    assert n_s % SUBLANES == 0

    w_in_bf = w_in[0].astype(BF16)
    w_conv_p = _pad_ff(jnp.transpose(w_conv[0].reshape(CONV_W, 2, D_FF), (1, 0, 2)), 2)
    b_conv_p = _pad_ff(b_conv[0].reshape(2, D_FF), 1)
    row = lambda a: a.reshape(1, -1)
    ln_g, ln_b = row(ln_sgu_g[0]), row(ln_sgu_b[0])
    nag, nbg = row(norm_a_g[0]), row(norm_b_g[0])
    l1g, l1b, l2g, l2b = row(ln1_g[0]), row(ln1_b[0]), row(ln2_g[0]), row(ln2_b[0])
    bs_full = jnp.broadcast_to(b_s[0][:, :, None], (N_GROUPS_A, CHUNK, HEAD_DIM))
    tril_tok = np.tril(np.ones((n_tok, n_tok), np.float32))
    ws_tok = jnp.repeat(jnp.transpose(w_s[0][:, :n_tok, :n_tok] * tril_tok, (1, 2, 0)), HEAD_DIM, axis=2)
    bs_tok = jnp.repeat(b_s[0][:, :n_tok].T, HEAD_DIM, axis=1).reshape(n_tok, 1, W_A)
    rev_tab, bias_past, bias_new = _bias_tables(rel_bias, seq, past, n_tok)

    pad_rows = (-(n_s + n_p)) % SUBLANES
    c_all = jnp.concatenate([c_sample, c_prompt, jnp.zeros((pad_rows, D_MODEL), F32)], axis=0)
    mod = _ada(c_all, w_ada[0], b_ada[0])
    mod_s = mod
    mod_p = mod[n_s:n_s + n_p].reshape(n_p, 1, 6 * D_MODEL)

    xp = x_prompt.reshape(n_p * seq, D_MODEL)
    q_hi, q_lo, k_p, v_p, k_bf, v_bf, k_mean, gm_n, w_up_a, w_up_b, w_down_bf, w_out_bf = _proj_prompt(
        xp, mod_p, w_in_bf, ln_g, ln_b, w_s[0], bs_full, nag, w_up[0], w_down[0], w_out[0], tm=256, seq=seq)
    attn_p = _attn_prompt(q_hi, q_lo, k_bf, v_bf, k_mean.reshape(n_p, seq // MOBA_BLOCK, W_B), rev_tab,
                          n_seq=n_p, seq=seq)
    x1_p, h2_p = _mix_prompt(xp, gm_n, attn_p, mod_p, w_out_bf, nbg, l1g, l1b, tm=512, seq=seq)
    y_p, st_p = _ffn_prompt(h2_p, x1_p, mod_p, w_up_a, w_up_b, w_down_bf, w_conv_p, b_conv_p, l2g, l2b,
                            tm=512, seq=seq)
    tiles_per_seq = st_p.shape[0] // n_p
    conv_p = st_p[tiles_per_seq - 1::tiles_per_seq, :, SUBLANES - (CONV_W - 1):, :D_FF]
    conv_p = jnp.transpose(conv_p, (0, 2, 1, 3)).reshape(1, n_p, CONV_W - 1, 2 * D_FF)

    xs = x_sample.reshape(n_s, n_tok * D_MODEL)
    q_s, k_s, v_s, u_s, va_s = _proj_sample(xs, mod_s, w_in_bf, ln_g, ln_b, n_seq=n_s, n_tok=n_tok)
    heads = lambda a: a.reshape(n_s, n_tok, N_HEADS_B, HEAD_DIM)
    attn_s = _attn_sample(page_table, heads(q_s), heads(k_s), heads(v_s), bias_past, bias_new, cache_k, cache_v)
    x1_s, h2_s = _mix_sample(xs, u_s, va_s, ws_tok, bs_tok, nag, attn_s.reshape(n_s, n_tok * W_B), mod_s,
                             w_out_bf, nbg, l1g, l1b, n_seq=n_s, n_tok=n_tok)
    past_up = state_ffn_conv[0].reshape(n_s, CONV_W - 1, 2, D_FF)
    past_up = _pad_ff(jnp.transpose(past_up, (1, 2, 0, 3)), 3)
    y_s, st_s = _ffn_sample(h2_s, x1_s, mod_s, w_up_a, w_up_b, w_down_bf, w_conv_p, b_conv_p, past_up,
                            l2g, l2b, n_seq=n_s, n_tok=n_tok)
    conv_s = jnp.transpose(st_s[..., :D_FF], (2, 0, 1, 3)).reshape(1, n_s, CONV_W - 1, 2 * D_FF)

    heads_p = (1, n_p, seq, N_HEADS_B, HEAD_DIM)
    heads_s = (1, n_s, n_tok, N_HEADS_B, HEAD_DIM)
    return (y_p.reshape(n_p, seq, D_MODEL), y_s.reshape(n_s, n_tok, D_MODEL),
            k_p.reshape(heads_p), v_p.reshape(heads_p), conv_p,
            k_s.reshape(heads_s), v_s.reshape(heads_s), va_s.reshape(1, n_s, n_tok, W_A), conv_s)
```

```python
import functools

import numpy as np
import jax
import jax.numpy as jnp
from jax import lax
from jax.experimental import pallas as pl
from jax.experimental.pallas import tpu as pltpu

F32 = jnp.float32
BF16 = jnp.bfloat16

D_MODEL = 2048
HEAD_DIM = 128
N_HEADS_B = 8
N_GROUPS_A = 8
W_B = N_HEADS_B * HEAD_DIM
W_A = N_GROUPS_A * HEAD_DIM
CHUNK = 128
MOBA_BLOCK = 256
MOBA_TOPK = 3
NUM_BUCKETS = 32
MAX_DISTANCE = 1024
CONV_W = 3
D_FF = 5504
PAGE_SIZE = 128
ALPHA = 2.0 ** 0.25
LN_EPS = 1e-5
NEG_INF = -1e30
LOG2_E = 1.4426950408889634

FF_TILE = 512
D_FF_PAD = ((D_FF + FF_TILE - 1) // FF_TILE) * FF_TILE
N_FF_TILES = D_FF_PAD // FF_TILE
SUBLANES = 8
BF16_ROWS = 16
ADA_COLS = 1024
PROJ_TILE = 256
ROW_TILE = 512
FFN_ROWS = 256
QK_AHEAD = 1
PV_LAG = 1
MIX_ROWS = 256
VMEM_LIMIT_BYTES = 56 * 1024 * 1024

_NT = (((1,), (1,)), ((), ()))


def _params(n_axes):
    return pltpu.CompilerParams(dimension_semantics=("arbitrary",) * n_axes,
                                vmem_limit_bytes=VMEM_LIMIT_BYTES)


def _layer_norm(x, g, b):
    mu = jnp.mean(x, axis=-1, keepdims=True)
    xc = x - mu
    var = jnp.mean(xc * xc, axis=-1, keepdims=True)
    return xc * lax.rsqrt(var + LN_EPS) * g + b


def _rms_norm(x, g):
    return x * lax.rsqrt(jnp.mean(x * x, axis=-1, keepdims=True) + LN_EPS) * g


def _split_bf16(x):
    hi = x.astype(BF16)
    lo = (x - hi.astype(F32)).astype(BF16)
    return hi, lo


def _dot(a, b):
    return jnp.dot(a, b, preferred_element_type=F32)


def _dot_nt(a, b):
    return lax.dot_general(a, b, _NT, preferred_element_type=F32)


def _topk_select(s, n_valid, axis):
    idx = lax.broadcasted_iota(jnp.int32, s.shape, axis)
    rank = jnp.zeros(s.shape, F32)
    for i in range(n_valid):
        si = lax.slice_in_dim(s, i, i + 1, axis=axis)
        beats = (si > s) | ((si == s) & (i < idx))
        rank = rank + beats.astype(F32)
    return ((rank < MOBA_TOPK) & (idx < n_valid)).astype(F32)


def _ada_kernel(c_ref, w_ref, b_ref, o_ref):
    c = c_ref[...]
    a = jax.nn.silu(c).astype(BF16)
    o_ref[...] = _dot(a, w_ref[...].astype(BF16)) + b_ref[...]


def _ada(c_all, w_ada, b_ada):
    rows = c_all.shape[0]
    n_out = w_ada.shape[1]
    tn = ADA_COLS
    return pl.pallas_call(
        _ada_kernel,
        grid=(n_out // tn,),
        in_specs=[pl.BlockSpec((rows, D_MODEL), lambda j: (0, 0)),
                  pl.BlockSpec((D_MODEL, tn), lambda j: (0, j)),
                  pl.BlockSpec((1, tn), lambda j: (0, j))],
        out_specs=pl.BlockSpec((rows, tn), lambda j: (0, j)),
        out_shape=jax.ShapeDtypeStruct((rows, n_out), F32),
        compiler_params=_params(1),
        name="ada",
    )(c_all, w_ada, b_ada.reshape(1, n_out))


_COL_Q, _COL_K, _COL_V, _COL_U, _COL_VA = range(5)


def _proj_prompt_kernel(x_ref, sc_ref, sh_ref, w_ref, lng_ref, lnb_ref, ws_ref, bs_ref, nag_ref,
                        wup_ref, wdn_ref, wout_ref,
                        qhi_ref, qlo_ref, k_ref, v_ref, kbf_ref, vbf_ref, km_ref, gm_ref,
                        wa_ref, wb_ref, wd_ref, wo_ref):
    tm = x_ref.shape[0]
    w_up = wup_ref[...]
    ff_pad = jnp.zeros((w_up.shape[0], D_FF_PAD - D_FF), BF16)
    wa_ref[...] = jnp.concatenate([w_up[:, :D_FF].astype(BF16), ff_pad], axis=1)
    wb_ref[...] = jnp.concatenate([w_up[:, D_FF:].astype(BF16), ff_pad], axis=1)
    dn_row = pl.program_id(0) * wdn_ref.shape[0] + lax.broadcasted_iota(jnp.int32, wdn_ref.shape, 0)
    wd_ref[...] = jnp.where(dn_row < D_FF, wdn_ref[...], 0.0).astype(BF16)
    wo_ref[...] = wout_ref[...].astype(BF16)

    h = (x_ref[...] * (1.0 + sc_ref[...]) + sh_ref[...]).astype(BF16)

    def proj(c):
        return _dot(h, w_ref[:, c * W_B:(c + 1) * W_B])

    def gmlp(u, acc_va):
        va = _layer_norm(jax.nn.gelu(acc_va), lng_ref[...], lnb_ref[...]).astype(BF16)
        n_chunks = tm // CHUNK
        row = lax.broadcasted_iota(jnp.int32, (CHUNK, CHUNK), 0)
        col = lax.broadcasted_iota(jnp.int32, (CHUNK, CHUNK), 1)
        groups = []
        for g in range(N_GROUPS_A):
            gs = slice(g * HEAD_DIM, (g + 1) * HEAD_DIM)
            w_g = jnp.where(row >= col, ws_ref[g], 0.0).astype(BF16)
            v_g = jnp.concatenate([va[c * CHUNK:(c + 1) * CHUNK, gs] for c in range(n_chunks)], axis=1)
            m_g = _dot(w_g, v_g)
            mixed = jnp.concatenate([m_g[:, c * HEAD_DIM:(c + 1) * HEAD_DIM] + bs_ref[g]
                                     for c in range(n_chunks)], axis=0)
            groups.append(u[:, gs] * mixed)
        gm_ref[...] = _rms_norm(jnp.concatenate(groups, axis=1), nag_ref[...]).astype(BF16)

    def put_q(acc):
        q_hi, q_lo = _split_bf16(acc * (LOG2_E * HEAD_DIM ** -0.5))
        qhi_ref[...] = q_hi
        qlo_ref[...] = q_lo

    def put_k(acc):
        k_ref[...] = acc
        kbf_ref[...] = acc.astype(BF16)
        km_ref[...] = jnp.mean(acc.reshape(tm // MOBA_BLOCK, MOBA_BLOCK, W_B), axis=1)

    def put_v(acc):
        v_ref[...] = acc
        vbf_ref[...] = acc.astype(BF16)

    acc_u = proj(_COL_U)
    acc_va = proj(_COL_VA)
    u = jax.nn.gelu(acc_u)
    acc_q = proj(_COL_Q)
    gmlp(u, acc_va)
    acc_k = proj(_COL_K)
    put_q(acc_q)
    acc_v = proj(_COL_V)
    put_k(acc_k)
    put_v(acc_v)


def _proj_prompt(x, mod_p, w_in, ln_g, ln_b, w_s, bs_full, norm_a_g, w_up, w_down, w_out, *, tm, seq):
    rows = x.shape[0]
    tiles_per_seq = seq // tm
    n_steps = rows // tm
    up_rows, dn_rows, out_rows = w_up.shape[0] // n_steps, D_FF_PAD // n_steps, w_out.shape[0] // n_steps
    assert up_rows * n_steps == w_up.shape[0] and dn_rows * n_steps == D_FF_PAD
    assert out_rows * n_steps == w_out.shape[0] and (n_steps - 1) * dn_rows < w_down.shape[0]
    assert up_rows % BF16_ROWS == 0 and dn_rows % BF16_ROWS == 0 and out_rows % BF16_ROWS == 0
    row_blk = lambda i: (i, 0)
    vec = lambda i: (0, 0)
    whole3 = lambda i: (0, 0, 0)
    out_shape = (
        jax.ShapeDtypeStruct((rows, W_B), BF16),
        jax.ShapeDtypeStruct((rows, W_B), BF16),
        jax.ShapeDtypeStruct((rows, W_B), F32),
        jax.ShapeDtypeStruct((rows, W_B), F32),
        jax.ShapeDtypeStruct((rows, W_B), BF16),
        jax.ShapeDtypeStruct((rows, W_B), BF16),
        jax.ShapeDtypeStruct((rows // tm, tm // MOBA_BLOCK, W_B), F32),
        jax.ShapeDtypeStruct((rows, W_A), BF16),
        jax.ShapeDtypeStruct((w_up.shape[0], D_FF_PAD), BF16),
        jax.ShapeDtypeStruct((w_up.shape[0], D_FF_PAD), BF16),
        jax.ShapeDtypeStruct((D_FF_PAD, w_down.shape[1]), BF16),
        jax.ShapeDtypeStruct(w_out.shape, BF16),
    )
    out_specs = (
        pl.BlockSpec((tm, W_B), row_blk), pl.BlockSpec((tm, W_B), row_blk),
        pl.BlockSpec((tm, W_B), row_blk), pl.BlockSpec((tm, W_B), row_blk),
        pl.BlockSpec((tm, W_B), row_blk), pl.BlockSpec((tm, W_B), row_blk),
        pl.BlockSpec((None, tm // MOBA_BLOCK, W_B), lambda i: (i, 0, 0)),
        pl.BlockSpec((tm, W_A), row_blk),
        pl.BlockSpec((up_rows, D_FF_PAD), row_blk), pl.BlockSpec((up_rows, D_FF_PAD), row_blk),
        pl.BlockSpec((dn_rows, w_down.shape[1]), row_blk),
        pl.BlockSpec((out_rows, w_out.shape[1]), row_blk),
    )
    return pl.pallas_call(
        _proj_prompt_kernel,
        grid=(rows // tm,),
        in_specs=[pl.BlockSpec((tm, D_MODEL), row_blk),
                  pl.BlockSpec((None, 1, D_MODEL), lambda i: (i // tiles_per_seq, 0, 1)),
                  pl.BlockSpec((None, 1, D_MODEL), lambda i: (i // tiles_per_seq, 0, 0)),
                  pl.BlockSpec(w_in.shape, vec, pipeline_mode=pl.Buffered(1)),
                  pl.BlockSpec((1, W_A), vec), pl.BlockSpec((1, W_A), vec),
                  pl.BlockSpec((N_GROUPS_A, CHUNK, CHUNK), whole3),
                  pl.BlockSpec((N_GROUPS_A, CHUNK, HEAD_DIM), whole3),
                  pl.BlockSpec((1, W_A), vec),
                  pl.BlockSpec((up_rows, w_up.shape[1]), row_blk),
                  pl.BlockSpec((dn_rows, w_down.shape[1]), row_blk),
                  pl.BlockSpec((out_rows, w_out.shape[1]), row_blk)],
        out_specs=out_specs,
        out_shape=out_shape,
        compiler_params=_params(1),
        name="proj_prompt",
    )(x, mod_p, mod_p, w_in, ln_g, ln_b, w_s, bs_full, norm_a_g, w_up, w_down, w_out)


def _proj_sample_kernel(n_tok, x_ref, sc_ref, sh_ref, w_ref, lng_ref, lnb_ref,
                        q_ref, k_ref, v_ref, u_ref, va_ref, h_scr):
    j = pl.program_id(0)
    n_seq = sc_ref.shape[0]

    @pl.when(j == 0)
    def _():
        for t in range(n_tok):
            x_t = x_ref[:, t * D_MODEL:(t + 1) * D_MODEL]
            h_scr[t * n_seq:(t + 1) * n_seq, :] = (x_t * (1.0 + sc_ref[...]) + sh_ref[...]).astype(BF16)

    acc = _dot(h_scr[...], w_ref[...])

    def put(o_ref, val):
        for t in range(n_tok):
            o_ref[:, t * W_B:(t + 1) * W_B] = val[t * n_seq:(t + 1) * n_seq, :]

    @pl.when(j == _COL_Q)
    def _():
        put(q_ref, acc * (HEAD_DIM ** -0.5))

    @pl.when(j == _COL_K)
    def _():
        put(k_ref, acc)

    @pl.when(j == _COL_V)
    def _():
        put(v_ref, acc)

    @pl.when(j == _COL_U)
    def _():
        put(u_ref, jax.nn.gelu(acc))

    @pl.when(j == _COL_VA)
    def _():
        put(va_ref, _layer_norm(jax.nn.gelu(acc), lng_ref[...], lnb_ref[...]))


def _proj_sample(x2d, mod_s, w_in, ln_g, ln_b, *, n_seq, n_tok):
    n_col = w_in.shape[1] // W_B
    whole = lambda j: (0, 0)
    out = jax.ShapeDtypeStruct((n_seq, n_tok * W_B), F32)
    return pl.pallas_call(
        functools.partial(_proj_sample_kernel, n_tok),
        grid=(n_col,),
        in_specs=[pl.BlockSpec((n_seq, n_tok * D_MODEL), whole),
                  pl.BlockSpec((n_seq, D_MODEL), lambda j: (0, 1)),
                  pl.BlockSpec((n_seq, D_MODEL), lambda j: (0, 0)),
                  pl.BlockSpec((D_MODEL, W_B), lambda j: (0, j)),
                  pl.BlockSpec((1, W_A), whole), pl.BlockSpec((1, W_A), whole)],
        out_specs=tuple(pl.BlockSpec((n_seq, n_tok * W_B), whole) for _ in range(5)),
        out_shape=(out,) * 5,
        scratch_shapes=[pltpu.VMEM((n_tok * n_seq, D_MODEL), BF16)],
        compiler_params=_params(1),
        name="proj_sample",
    )(x2d, mod_s, mod_s, w_in, ln_g, ln_b)


def _attn_prompt_kernel(qhi_ref, qlo_ref, k_ref, v_ref, km_ref, rev_ref, o_ref, bias_scr):
    n_blocks = k_ref.shape[0] // MOBA_BLOCK

    @pl.when(pl.program_id(1) == 0)
    def _():
        row = lax.broadcasted_iota(jnp.int32, (MOBA_BLOCK, MOBA_BLOCK), 0)
        col = lax.broadcasted_iota(jnp.int32, (MOBA_BLOCK, MOBA_BLOCK), 1)
        for d in range(n_blocks):
            r = jnp.broadcast_to(rev_ref[d:d + 1, :], (MOBA_BLOCK, 2 * MOBA_BLOCK))
            tile = pltpu.roll(r, 0, 1, stride=1, stride_axis=0)[:, MOBA_BLOCK:]
            bias_scr[d] = jnp.where(row >= col, tile, NEG_INF) if d == 0 else tile

    km_hi, km_lo = _split_bf16(km_ref[...])

    def rows_of(qb):
        return slice(qb * MOBA_BLOCK, (qb + 1) * MOBA_BLOCK)

    def qk(qb):
        q_hi = qhi_ref[rows_of(qb), :]
        return [_dot_nt(q_hi, k_ref[rows_of(jb), :]) for jb in range(qb + 1)]

    def select(qb):
        q_hi, q_lo = qhi_ref[rows_of(qb), :], qlo_ref[rows_of(qb), :]
        s_t = _dot_nt(km_hi, q_hi) + _dot_nt(km_lo, q_hi) + _dot_nt(km_hi, q_lo)
        return jnp.where(_topk_select(s_t, qb, axis=0) > 0.0, 0.0, NEG_INF).T

    sels = {qb: select(qb) for qb in range(MOBA_TOPK + 1, n_blocks)}

    def softmax(qb, logits):
        sel = sels.get(qb)
        pieces = []
        for jb in range(qb + 1):
            lj = logits[jb] + bias_scr[qb - jb]
            if jb < qb and sel is not None:
                lj = lj + sel[:, jb:jb + 1]
            pieces.append(lj)
        m = jnp.max(functools.reduce(jnp.maximum, pieces), axis=-1, keepdims=True)
        ps = [jnp.exp2(lj - m) for lj in pieces]
        z = jnp.sum(functools.reduce(jnp.add, ps), axis=-1, keepdims=True)
        return jnp.concatenate([pj.astype(BF16) for pj in ps], axis=1), z

    def pv(qb, p, z):
        o_ref[rows_of(qb), :] = _dot(p, v_ref[0:(qb + 1) * MOBA_BLOCK, :]) / z

    ahead = [qk(t) for t in range(min(QK_AHEAD, n_blocks))]
    pending = []
    for qb in range(n_blocks):
        if qb + QK_AHEAD < n_blocks:
            ahead.append(qk(qb + QK_AHEAD))
        pending.append((qb,) + softmax(qb, ahead.pop(0)))
        if len(pending) > PV_LAG:
            pv(*pending.pop(0))
    for item in pending:
        pv(*item)


def _attn_prompt(q_hi, q_lo, k_bf, v_bf, k_mean, rev_tab, *, n_seq, seq):
    n_blocks = seq // MOBA_BLOCK
    qkv = pl.BlockSpec((seq, HEAD_DIM), lambda h, n: (n, h))
    return pl.pallas_call(
        _attn_prompt_kernel,
        grid=(N_HEADS_B, n_seq),
        in_specs=[qkv, qkv, qkv, qkv,
                  pl.BlockSpec((None, n_blocks, HEAD_DIM), lambda h, n: (n, 0, h)),
                  pl.BlockSpec((None, n_blocks, 2 * MOBA_BLOCK), lambda h, n: (h, 0, 0))],
        out_specs=pl.BlockSpec((seq, HEAD_DIM), lambda h, n: (n, h)),
        out_shape=jax.ShapeDtypeStruct((n_seq * seq, W_B), F32),
        scratch_shapes=[pltpu.VMEM((n_blocks, MOBA_BLOCK, MOBA_BLOCK), F32)],
        compiler_params=_params(2),
        name="attn_prompt",
    )(q_hi, q_lo, k_bf, v_bf, k_mean, rev_tab)


def _attn_sample_kernel(n_pages, pt_ref, q_ref, kn_ref, vn_ref, bias_ref, biasn_ref, *refs):
    k_pages = refs[:n_pages]
    v_pages = refs[n_pages:2 * n_pages]
    o_ref = refs[2 * n_pages]
    n_tok = q_ref.shape[0]
    rows = n_tok * N_HEADS_B
    cols = PAGE_SIZE * N_HEADS_B
    pages_per_block = MOBA_BLOCK // PAGE_SIZE
    n_blocks = n_pages // pages_per_block

    q = q_ref[...].reshape(rows, HEAD_DIM)
    q_bf = q.astype(BF16)

    logit_pages = []
    k_sums = []
    for p in range(n_pages):
        kp = k_pages[p][...]
        k_sums.append(jnp.sum(kp, axis=0))
        logit_pages.append(_dot_nt(q_bf, kp.reshape(cols, HEAD_DIM).astype(BF16)))

    blk_lane = lax.broadcasted_iota(jnp.int32, (rows, n_blocks), 1)
    scores = jnp.zeros((rows, n_blocks), F32)
    for b in range(n_blocks):
        km = sum(k_sums[b * pages_per_block:(b + 1) * pages_per_block]) * (1.0 / MOBA_BLOCK)
        s_b = jnp.sum(q * jnp.concatenate([km] * n_tok, axis=0), axis=1, keepdims=True)
        scores = jnp.where(blk_lane == b, s_b, scores)
    sel = _topk_select(scores, n_blocks, axis=1)

    masked = []
    for p in range(n_pages):
        b = p // pages_per_block
        lp = logit_pages[p] + bias_ref[:, p * cols:(p + 1) * cols]
        masked.append(jnp.where(sel[:, b:b + 1] > 0.0, lp, NEG_INF))
    ln = _dot_nt(q_bf, kn_ref[...].reshape(rows, HEAD_DIM).astype(BF16)) + biasn_ref[...]

    m = jnp.maximum(jnp.max(functools.reduce(jnp.maximum, masked), axis=-1, keepdims=True),
                    jnp.max(ln, axis=-1, keepdims=True))
    p_new = jnp.exp(ln - m)
    acc = _dot(p_new.astype(BF16), vn_ref[...].reshape(rows, HEAD_DIM).astype(BF16))
    p_sum = None
    for p in range(n_pages):
        pp = jnp.exp(masked[p] - m)
        p_sum = pp if p_sum is None else p_sum + pp
        acc = acc + _dot(pp.astype(BF16), v_pages[p][...].reshape(cols, HEAD_DIM).astype(BF16))
    z = jnp.sum(p_sum, axis=-1, keepdims=True) + jnp.sum(p_new, axis=-1, keepdims=True)
    o_ref[...] = (acc / z).reshape(n_tok, N_HEADS_B, HEAD_DIM)


def _attn_sample(page_table, q, k_new, v_new, bias_past, bias_new, cache_k, cache_v):
    n_seq, n_pages = page_table.shape
    n_tok = q.shape[1]
    tok = pl.BlockSpec((None, n_tok, N_HEADS_B, HEAD_DIM), lambda n, pt: (n, 0, 0, 0))

    def page_spec(p):
        return pl.BlockSpec((None, None, PAGE_SIZE, N_HEADS_B, HEAD_DIM),
                            lambda n, pt: (0, pt[n * n_pages + p], 0, 0, 0))

    grid_spec = pltpu.PrefetchScalarGridSpec(
        num_scalar_prefetch=1,
        grid=(n_seq,),
        in_specs=[tok, tok, tok,
                  pl.BlockSpec(bias_past.shape, lambda n, pt: (0, 0)),
                  pl.BlockSpec(bias_new.shape, lambda n, pt: (0, 0))]
                 + [page_spec(p) for p in range(n_pages)] * 2,
        out_specs=tok,
    )
    return pl.pallas_call(
        functools.partial(_attn_sample_kernel, n_pages),
        grid_spec=grid_spec,
        out_shape=jax.ShapeDtypeStruct((n_seq, n_tok, N_HEADS_B, HEAD_DIM), F32),
        compiler_params=_params(1),
        name="attn_sample",
    )(page_table.reshape(-1), q, k_new, v_new, bias_past, bias_new,
      *([cache_k] * n_pages), *([cache_v] * n_pages))


def _mix_project(gm_n, attn, wo_ref, nbg_ref):
    a_n = _rms_norm(attn, nbg_ref[...]).astype(BF16)
    return _dot(gm_n, wo_ref[0:W_A, :]) + _dot(a_n, wo_ref[W_A:W_A + W_B, :])


def _mix_finish(x, o, g1, sc2, sh2, l1g_ref, l1b_ref):
    x1 = _layer_norm(ALPHA * x + g1 * o, l1g_ref[...], l1b_ref[...])
    return x1, (x1 * (1.0 + sc2) + sh2).astype(BF16)


def _mix_prompt_kernel(x_ref, gm_ref, at_ref, g1_ref, sc2_ref, sh2_ref, wo_ref, nbg_ref, l1g_ref, l1b_ref,
                       x1_ref, h2_ref):
    tm = x_ref.shape[0]
    parts = [slice(r, r + MIX_ROWS) for r in range(0, tm, MIX_ROWS)]
    outs = [_mix_project(gm_ref[rs, :], at_ref[rs, :], wo_ref, nbg_ref) for rs in parts]
    for rs, o in zip(parts, outs):
        x1_ref[rs, :], h2_ref[rs, :] = _mix_finish(x_ref[rs, :], o, g1_ref[...], sc2_ref[...], sh2_ref[...],
                                                   l1g_ref, l1b_ref)


def _mix_prompt(x, gm_n, attn, mod_p, w_out, norm_b_g, ln1_g, ln1_b, *, tm, seq):
    rows = x.shape[0]
    tiles_per_seq = seq // tm
    row_blk = lambda i: (i, 0)
    vec = lambda i: (0, 0)
    mod = lambda c: pl.BlockSpec((None, 1, D_MODEL), lambda i: (i // tiles_per_seq, 0, c))
    return pl.pallas_call(
        _mix_prompt_kernel,
        grid=(rows // tm,),
        in_specs=[pl.BlockSpec((tm, D_MODEL), row_blk), pl.BlockSpec((tm, W_A), row_blk),
                  pl.BlockSpec((tm, W_B), row_blk), mod(2), mod(4), mod(3),
                  pl.BlockSpec((D_MODEL, D_MODEL), vec), pl.BlockSpec((1, W_B), vec),
                  pl.BlockSpec((1, D_MODEL), vec), pl.BlockSpec((1, D_MODEL), vec)],
        out_specs=(pl.BlockSpec((tm, D_MODEL), row_blk), pl.BlockSpec((tm, D_MODEL), row_blk)),
        out_shape=(jax.ShapeDtypeStruct((rows, D_MODEL), F32), jax.ShapeDtypeStruct((rows, D_MODEL), BF16)),
        compiler_params=_params(1),
        name="mix_prompt",
    )(x, gm_n, attn, mod_p, mod_p, mod_p, w_out, norm_b_g, ln1_g, ln1_b)


def _mix_sample_kernel(x_ref, u_ref, va_ref, ws_ref, bs_ref, nag_ref, at_ref, g1_ref, sc2_ref, sh2_ref,
                       wo_ref, nbg_ref, l1g_ref, l1b_ref, x1_ref, h2_ref):
    n_tok = ws_ref.shape[0]
    mixed = bs_ref[...]
    for s in range(n_tok):
        mixed = mixed + ws_ref[s:s + 1, :] * va_ref[:, s * W_A:(s + 1) * W_A]
    gm_n = _rms_norm(u_ref[...] * mixed, nag_ref[...]).astype(BF16)
    o = _mix_project(gm_n, at_ref[...], wo_ref, nbg_ref)
    x1_ref[...], h2_ref[...] = _mix_finish(x_ref[...], o, g1_ref[...], sc2_ref[...], sh2_ref[...],
                                           l1g_ref, l1b_ref)


def _mix_sample(x2d, u, va, ws_tok, bs_tok, norm_a_g, attn2d, mod_s, w_out, norm_b_g, ln1_g, ln1_b,
                *, n_seq, n_tok):
    tok_blk = lambda t: (0, t)
    vec = lambda t: (0, 0)
    mod = lambda c: pl.BlockSpec((n_seq, D_MODEL), lambda t: (0, c))
    out_blk = pl.BlockSpec((n_seq, D_MODEL), lambda t: (t, 0))
    return pl.pallas_call(
        _mix_sample_kernel,
        grid=(n_tok,),
        in_specs=[pl.BlockSpec((n_seq, D_MODEL), tok_blk), pl.BlockSpec((n_seq, W_A), tok_blk),
                  pl.BlockSpec((n_seq, n_tok * W_A), vec),
                  pl.BlockSpec((None, n_tok, W_A), lambda t: (t, 0, 0)),
                  pl.BlockSpec((None, 1, W_A), lambda t: (t, 0, 0)),
                  pl.BlockSpec((1, W_A), vec),
                  pl.BlockSpec((n_seq, W_B), tok_blk), mod(2), mod(4), mod(3),
                  pl.BlockSpec((D_MODEL, D_MODEL), vec), pl.BlockSpec((1, W_B), vec),
                  pl.BlockSpec((1, D_MODEL), vec), pl.BlockSpec((1, D_MODEL), vec)],
        out_specs=(out_blk, out_blk),
        out_shape=(jax.ShapeDtypeStruct((n_tok * n_seq, D_MODEL), F32),
                   jax.ShapeDtypeStruct((n_tok * n_seq, D_MODEL), BF16)),
        compiler_params=_params(1),
        name="mix_sample",
    )(x2d, u, va, ws_tok, bs_tok, norm_a_g, attn2d, mod_s, mod_s, mod_s, w_out, norm_b_g, ln1_g, ln1_b)


def _ffn_act(conv_a, conv_b):
    return (jax.nn.silu(conv_a) * conv_b).astype(BF16)


def _ffn_prompt_kernel(tiles_per_seq, h_ref, halo_ref, x1_ref, g2_ref, wa_ref, wb_ref, wd_ref, wc_ref, bc_ref,
                       l2g_ref, l2b_ref, y_ref, st_ref, acc_scr, h_scr):
    i = pl.program_id(0)
    j = pl.program_id(1)
    tm = h_ref.shape[0]
    n_halo = halo_ref.shape[0]

    @pl.when(j == 0)
    def _():
        seq_start = (i % tiles_per_seq) == 0
        h_scr[0:n_halo, :] = jnp.where(seq_start, jnp.zeros_like(halo_ref[...]), halo_ref[...])
        h_scr[n_halo:n_halo + tm, :] = h_ref[...]
        acc_scr[...] = jnp.zeros_like(acc_scr)

    n_parts = tm // FFN_ROWS
    assert n_parts * FFN_ROWS == tm

    def up_proj(r, prev):
        if prev is None:
            h = h_scr[0:n_halo + FFN_ROWS, :]
            return _dot(h, wa_ref[...]), _dot(h, wb_ref[...])
        h = h_scr[n_halo + r * FFN_ROWS:n_halo + (r + 1) * FFN_ROWS, :]
        return tuple(jnp.concatenate([p[FFN_ROWS:, :], _dot(h, w_ref[...])], axis=0)
                     for p, w_ref in zip(prev, (wa_ref, wb_ref)))

    def conv(half, up, last):
        if last:
            st_ref[half] = up[n_halo + FFN_ROWS - SUBLANES:n_halo + FFN_ROWS, :]
        out = bc_ref[half:half + 1, :]
        for c in range(CONV_W):
            lag = CONV_W - 1 - c
            out = out + up[n_halo - lag:n_halo - lag + FFN_ROWS, :] * wc_ref[half, c:c + 1, :]
        return out

    def down(r, act):
        acc_scr[r * FFN_ROWS:(r + 1) * FFN_ROWS, :] += _dot(act, wd_ref[...])

    ups = up_proj(0, None)
    for r in range(n_parts):
        nxt = up_proj(r + 1, ups) if r + 1 < n_parts else None
        act = _ffn_act(conv(0, ups[0], r == n_parts - 1), conv(1, ups[1], r == n_parts - 1))
        down(r, act)
        ups = nxt

    @pl.when(j == pl.num_programs(1) - 1)
    def _():
        y_ref[...] = _layer_norm(ALPHA * x1_ref[...] + g2_ref[...] * acc_scr[...], l2g_ref[...], l2b_ref[...])


def _ffn_prompt(h2, x1, mod_p, w_up_a, w_up_b, w_down, w_conv, b_conv, ln2_g, ln2_b, *, tm, seq):
    rows = h2.shape[0]
    tiles_per_seq = seq // tm
    row_blk = lambda i, j: (i, 0)
    vec = lambda i, j: (0, 0)
    halo_blk = lambda i, j: (jnp.maximum(i * (tm // BF16_ROWS) - 1, 0), 0)
    return pl.pallas_call(
        functools.partial(_ffn_prompt_kernel, tiles_per_seq),
        grid=(rows // tm, N_FF_TILES),
        in_specs=[pl.BlockSpec((tm, D_MODEL), row_blk), pl.BlockSpec((BF16_ROWS, D_MODEL), halo_blk),
                  pl.BlockSpec((tm, D_MODEL), row_blk),
                  pl.BlockSpec((None, 1, D_MODEL), lambda i, j: (i // tiles_per_seq, 0, 5)),
                  pl.BlockSpec((D_MODEL, FF_TILE), lambda i, j: (0, j)),
                  pl.BlockSpec((D_MODEL, FF_TILE), lambda i, j: (0, j)),
                  pl.BlockSpec((FF_TILE, D_MODEL), lambda i, j: (j, 0)),
                  pl.BlockSpec((2, CONV_W, FF_TILE), lambda i, j: (0, 0, j)),
                  pl.BlockSpec((2, FF_TILE), lambda i, j: (0, j)),
                  pl.BlockSpec((1, D_MODEL), vec), pl.BlockSpec((1, D_MODEL), vec)],
        out_specs=(pl.BlockSpec((tm, D_MODEL), row_blk),
                   pl.BlockSpec((None, 2, SUBLANES, FF_TILE), lambda i, j: (i, 0, 0, j))),
        out_shape=(jax.ShapeDtypeStruct((rows, D_MODEL), F32),
                   jax.ShapeDtypeStruct((rows // tm, 2, SUBLANES, D_FF_PAD), F32)),
        scratch_shapes=[pltpu.VMEM((tm, D_MODEL), F32), pltpu.VMEM((BF16_ROWS + tm, D_MODEL), BF16)],
        compiler_params=_params(2),
        name="ffn_prompt",
    )(h2, h2, x1, mod_p, w_up_a, w_up_b, w_down, w_conv, b_conv, ln2_g, ln2_b)


def _ffn_sample_kernel(n_tok, h_ref, x1_ref, g2_ref, wa_ref, wb_ref, wd_ref, wc_ref, bc_ref, past_ref,
                       l2g_ref, l2b_ref, y_ref, st_ref, acc_scr):
    j = pl.program_id(0)
    n_seq = g2_ref.shape[0]
    h = h_ref[...]

    def conv(half, w_ref):
        up = _dot(h, w_ref[...])
        full = [past_ref[r, half] for r in range(CONV_W - 1)]
        full += [up[t * n_seq:(t + 1) * n_seq, :] for t in range(n_tok)]
        for r in range(CONV_W - 1):
            st_ref[r, half] = full[n_tok + r]
        w = wc_ref[half]
        b = bc_ref[half:half + 1, :]
        return jnp.concatenate(
            [b + sum(full[t + c] * w[c:c + 1, :] for c in range(CONV_W)) for t in range(n_tok)], axis=0)

    @pl.when(j == 0)
    def _():
        acc_scr[...] = jnp.zeros_like(acc_scr)

    acc_scr[...] += _dot(_ffn_act(conv(0, wa_ref), conv(1, wb_ref)), wd_ref[...])

    @pl.when(j == pl.num_programs(0) - 1)
    def _():
        g2 = g2_ref[...]
        for t in range(n_tok):
            rs = slice(t * n_seq, (t + 1) * n_seq)
            y_ref[:, t * D_MODEL:(t + 1) * D_MODEL] = _layer_norm(
                ALPHA * x1_ref[rs, :] + g2 * acc_scr[rs, :], l2g_ref[...], l2b_ref[...])


def _ffn_sample(h2, x1, mod_s, w_up_a, w_up_b, w_down, w_conv, b_conv, past, ln2_g, ln2_b, *, n_seq, n_tok):
    rows = n_seq * n_tok
    full = lambda j: (0, 0)
    state_blk = pl.BlockSpec((CONV_W - 1, 2, n_seq, FF_TILE), lambda j: (0, 0, 0, j))
    return pl.pallas_call(
        functools.partial(_ffn_sample_kernel, n_tok),
        grid=(N_FF_TILES,),
        in_specs=[pl.BlockSpec((rows, D_MODEL), full), pl.BlockSpec((rows, D_MODEL), full),
                  pl.BlockSpec((n_seq, D_MODEL), lambda j: (0, 5)),
                  pl.BlockSpec((D_MODEL, FF_TILE), lambda j: (0, j)),
                  pl.BlockSpec((D_MODEL, FF_TILE), lambda j: (0, j)),
                  pl.BlockSpec((FF_TILE, D_MODEL), lambda j: (j, 0)),
                  pl.BlockSpec((2, CONV_W, FF_TILE), lambda j: (0, 0, j)),
                  pl.BlockSpec((2, FF_TILE), lambda j: (0, j)),
                  state_blk,
                  pl.BlockSpec((1, D_MODEL), full), pl.BlockSpec((1, D_MODEL), full)],
        out_specs=(pl.BlockSpec((n_seq, n_tok * D_MODEL), full), state_blk),
        out_shape=(jax.ShapeDtypeStruct((n_seq, n_tok * D_MODEL), F32),
                   jax.ShapeDtypeStruct((CONV_W - 1, 2, n_seq, D_FF_PAD), F32)),
        scratch_shapes=[pltpu.VMEM((rows, D_MODEL), F32)],
        compiler_params=_params(1),
        name="ffn_sample",
    )(h2, x1, mod_s, w_up_a, w_up_b, w_down, w_conv, b_conv, past, ln2_g, ln2_b)


def _t5_bucket_table(n_rel):
    max_exact = NUM_BUCKETS // 2
    ratio = MAX_DISTANCE // max_exact
    log_ratio = ratio.bit_length() - 1
    assert 1 << log_ratio == ratio
    steps = NUM_BUCKETS - max_exact
    out = np.zeros((n_rel,), np.int32)
    for n in range(n_rel):
        if n < max_exact:
            out[n] = n
            continue
        k = 0
        while k + 1 < steps and n ** steps >= (max_exact ** steps) << (log_ratio * (k + 1)):
            k += 1
        out[n] = min(max_exact + k, NUM_BUCKETS - 1)
    return out


def _bias_tables(rel_bias, seq, past, n_tok):
    n_rel = max(seq, past + n_tok)
    onehot = np.zeros((NUM_BUCKETS, n_rel), np.float32)
    onehot[_t5_bucket_table(n_rel), np.arange(n_rel)] = 1.0
    tab = jnp.dot(rel_bias.astype(F32).T, onehot, precision=lax.Precision.HIGHEST)
    n_blocks = seq // MOBA_BLOCK
    tab_pad = jnp.pad(tab, ((0, 0), (MOBA_BLOCK, 1)))
    rev = jnp.stack([tab_pad[:, d * MOBA_BLOCK + 1:(d + 2) * MOBA_BLOCK + 1][:, ::-1]
                     for d in range(n_blocks)], axis=1)
    same_head = np.eye(N_HEADS_B, dtype=bool)
    by_key = jnp.stack([tab[:, t + 1:t + 1 + past][:, ::-1] for t in range(n_tok)], axis=0)
    n_pages = past // PAGE_SIZE
    spread = np.repeat(np.eye(PAGE_SIZE, dtype=np.float32), N_HEADS_B, axis=1)
    by_col = jnp.einsum('rpk,kc->rpc', by_key.reshape(n_tok * N_HEADS_B, n_pages, PAGE_SIZE), spread,
                        precision=lax.Precision.HIGHEST).reshape(n_tok * N_HEADS_B, past * N_HEADS_B)
    row_head = np.tile(np.arange(N_HEADS_B), n_tok)[:, None]
    col_head = np.tile(np.arange(N_HEADS_B), past)[None, :]
    bias_past = jnp.where(row_head == col_head, by_col, NEG_INF)
    new_rows = []
    for t in range(n_tok):
        by_tok = jnp.stack([tab[:, max(t - s, 0)] for s in range(n_tok)], axis=1)
        ok = same_head[:, None, :] & (np.arange(n_tok) <= t)[None, :, None]
        new_rows.append(jnp.where(ok, by_tok[:, :, None], NEG_INF).reshape(N_HEADS_B, n_tok * N_HEADS_B))
    bias_new = jnp.stack(new_rows, axis=0).reshape(n_tok * N_HEADS_B, n_tok * N_HEADS_B)
    return rev * LOG2_E, bias_past, bias_new


def _pad_ff(a, axis):
    pad = [(0, 0)] * a.ndim
    pad[axis] = (0, D_FF_PAD - D_FF)
    return jnp.pad(a, pad)


def kernel(x_prompt, x_sample, cache_k, cache_v, state_ffn_conv, page_table, c_prompt, c_sample,
           w_ada, b_ada, w_in, ln_sgu_g, ln_sgu_b, w_s, b_s, rel_bias, norm_a_g, norm_b_g, w_out,
           ln1_g, ln1_b, w_up, w_conv, b_conv, w_down, ln2_g, ln2_b):
    n_p, seq, _ = x_prompt.shape
    n_s, n_tok, _ = x_sample.shape
    n_pages = page_table.shape[1]
    past = n_pages * PAGE_SIZE
    assert w_ada.shape[0] == 1 and seq % MOBA_BLOCK == 0 and past % MOBA_BLOCK == 0 and n_tok <= CHUNK
    assert n_s % SUBLANES == 0

    w_in_bf = w_in[0].astype(BF16)
    w_conv_p = _pad_ff(jnp.transpose(w_conv[0].reshape(CONV_W, 2, D_FF), (1, 0, 2)), 2)
    b_conv_p = _pad_ff(b_conv[0].reshape(2, D_FF), 1)
    row = lambda a: a.reshape(1, -1)
    ln_g, ln_b = row(ln_sgu_g[0]), row(ln_sgu_b[0])
    nag, nbg = row(norm_a_g[0]), row(norm_b_g[0])
    l1g, l1b, l2g, l2b = row(ln1_g[0]), row(ln1_b[0]), row(ln2_g[0]), row(ln2_b[0])
    bs_full = jnp.broadcast_to(b_s[0][:, :, None], (N_GROUPS_A, CHUNK, HEAD_DIM))
    tril_tok = np.tril(np.ones((n_tok, n_tok), np.float32))
    ws_tok = jnp.repeat(jnp.transpose(w_s[0][:, :n_tok, :n_tok] * tril_tok, (1, 2, 0)), HEAD_DIM, axis=2)
    bs_tok = jnp.repeat(b_s[0][:, :n_tok].T, HEAD_DIM, axis=1).reshape(n_tok, 1, W_A)
    rev_tab, bias_past, bias_new = _bias_tables(rel_bias, seq, past, n_tok)

    pad_rows = (-(n_s + n_p)) % SUBLANES
    c_all = jnp.concatenate([c_sample, c_prompt, jnp.zeros((pad_rows, D_MODEL), F32)], axis=0)
    mod = _ada(c_all, w_ada[0], b_ada[0])
    mod_s = mod
    mod_p = mod[n_s:n_s + n_p].reshape(n_p, 1, 6 * D_MODEL)

    xp = x_prompt.reshape(n_p * seq, D_MODEL)
    q_hi, q_lo, k_p, v_p, k_bf, v_bf, k_mean, gm_n, w_up_a, w_up_b, w_down_bf, w_out_bf = _proj_prompt(
        xp, mod_p, w_in_bf, ln_g, ln_b, w_s[0], bs_full, nag, w_up[0], w_down[0], w_out[0], tm=PROJ_TILE, seq=seq)
    attn_p = _attn_prompt(q_hi, q_lo, k_bf, v_bf, k_mean.reshape(n_p, seq // MOBA_BLOCK, W_B), rev_tab,
                          n_seq=n_p, seq=seq)
    x1_p, h2_p = _mix_prompt(xp, gm_n, attn_p, mod_p, w_out_bf, nbg, l1g, l1b, tm=ROW_TILE, seq=seq)
    y_p, st_p = _ffn_prompt(h2_p, x1_p, mod_p, w_up_a, w_up_b, w_down_bf, w_conv_p, b_conv_p, l2g, l2b,
                            tm=ROW_TILE, seq=seq)
    tiles_per_seq = st_p.shape[0] // n_p
    conv_p = st_p[tiles_per_seq - 1::tiles_per_seq, :, SUBLANES - (CONV_W - 1):, :D_FF]
    conv_p = jnp.transpose(conv_p, (0, 2, 1, 3)).reshape(1, n_p, CONV_W - 1, 2 * D_FF)

    xs = x_sample.reshape(n_s, n_tok * D_MODEL)
    q_s, k_s, v_s, u_s, va_s = _proj_sample(xs, mod_s, w_in_bf, ln_g, ln_b, n_seq=n_s, n_tok=n_tok)
    heads = lambda a: a.reshape(n_s, n_tok, N_HEADS_B, HEAD_DIM)
    attn_s = _attn_sample(page_table, heads(q_s), heads(k_s), heads(v_s), bias_past, bias_new, cache_k, cache_v)
    x1_s, h2_s = _mix_sample(xs, u_s, va_s, ws_tok, bs_tok, nag, attn_s.reshape(n_s, n_tok * W_B), mod_s,
                             w_out_bf, nbg, l1g, l1b, n_seq=n_s, n_tok=n_tok)
    past_up = state_ffn_conv[0].reshape(n_s, CONV_W - 1, 2, D_FF)
    past_up = _pad_ff(jnp.transpose(past_up, (1, 2, 0, 3)), 3)
    y_s, st_s = _ffn_sample(h2_s, x1_s, mod_s, w_up_a, w_up_b, w_down_bf, w_conv_p, b_conv_p, past_up,
                            l2g, l2b, n_seq=n_s, n_tok=n_tok)
    conv_s = jnp.transpose(st_s[..., :D_FF], (2, 0, 1, 3)).reshape(1, n_s, CONV_W - 1, 2 * D_FF)

    heads_p = (1, n_p, seq, N_HEADS_B, HEAD_DIM)
    heads_s = (1, n_s, n_tok, N_HEADS_B, HEAD_DIM)
    return (y_p.reshape(n_p, seq, D_MODEL), y_s.reshape(n_s, n_tok, D_MODEL),
            k_p.reshape(heads_p), v_p.reshape(heads_p), conv_p,
            k_s.reshape(heads_s), v_s.reshape(heads_s), va_s.reshape(1, n_s, n_tok, W_A), conv_s)
```

```python
import functools

import numpy as np
import jax
import jax.numpy as jnp
from jax import lax
from jax.experimental import pallas as pl
from jax.experimental.pallas import tpu as pltpu

F32 = jnp.float32
BF16 = jnp.bfloat16

D_MODEL = 2048
HEAD_DIM = 128
N_HEADS_B = 8
N_GROUPS_A = 8
W_B = N_HEADS_B * HEAD_DIM
W_A = N_GROUPS_A * HEAD_DIM
CHUNK = 128
MOBA_BLOCK = 256
MOBA_TOPK = 3
NUM_BUCKETS = 32
MAX_DISTANCE = 1024
CONV_W = 3
D_FF = 5504
PAGE_SIZE = 128
ALPHA = 2.0 ** 0.25
LN_EPS = 1e-5
NEG_INF = -1e30
LOG2_E = 1.4426950408889634

FF_TILE = 512
D_FF_PAD = ((D_FF + FF_TILE - 1) // FF_TILE) * FF_TILE
N_FF_TILES = D_FF_PAD // FF_TILE
SUBLANES = 8
BF16_ROWS = 16
ADA_COLS = 1024
PROJ_TILE = 256
ROW_TILE = 512
FFN_ROWS = 256
QK_AHEAD = 1
PV_LAG = 1
MIX_ROWS = 256
VMEM_LIMIT_BYTES = 56 * 1024 * 1024

_NT = (((1,), (1,)), ((), ()))


def _params(n_axes):
    return pltpu.CompilerParams(dimension_semantics=("arbitrary",) * n_axes,
                                vmem_limit_bytes=VMEM_LIMIT_BYTES)


def _layer_norm(x, g, b):
    mu = jnp.mean(x, axis=-1, keepdims=True)
    xc = x - mu
    var = jnp.mean(xc * xc, axis=-1, keepdims=True)
    return xc * lax.rsqrt(var + LN_EPS) * g + b


def _rms_norm(x, g):
    return x * lax.rsqrt(jnp.mean(x * x, axis=-1, keepdims=True) + LN_EPS) * g


def _split_bf16(x):
    hi = x.astype(BF16)
    lo = (x - hi.astype(F32)).astype(BF16)
    return hi, lo


def _dot(a, b):
    return jnp.dot(a, b, preferred_element_type=F32)


def _dot_nt(a, b):
    return lax.dot_general(a, b, _NT, preferred_element_type=F32)


def _topk_select(s, n_valid, axis):
    idx = lax.broadcasted_iota(jnp.int32, s.shape, axis)
    rank = jnp.zeros(s.shape, F32)
    for i in range(n_valid):
        si = lax.slice_in_dim(s, i, i + 1, axis=axis)
        beats = (si > s) | ((si == s) & (i < idx))
        rank = rank + beats.astype(F32)
    return ((rank < MOBA_TOPK) & (idx < n_valid)).astype(F32)


def _ada_kernel(c_ref, w_ref, b_ref, o_ref):
    c = c_ref[...]
    a = jax.nn.silu(c).astype(BF16)
    o_ref[...] = _dot(a, w_ref[...].astype(BF16)) + b_ref[...]


def _ada(c_all, w_ada, b_ada):
    rows = c_all.shape[0]
    n_out = w_ada.shape[1]
    tn = ADA_COLS
    return pl.pallas_call(
        _ada_kernel,
        grid=(n_out // tn,),
        in_specs=[pl.BlockSpec((rows, D_MODEL), lambda j: (0, 0)),
                  pl.BlockSpec((D_MODEL, tn), lambda j: (0, j)),
                  pl.BlockSpec((1, tn), lambda j: (0, j))],
        out_specs=pl.BlockSpec((rows, tn), lambda j: (0, j)),
        out_shape=jax.ShapeDtypeStruct((rows, n_out), F32),
        compiler_params=_params(1),
        name="ada",
    )(c_all, w_ada, b_ada.reshape(1, n_out))


_COL_Q, _COL_K, _COL_V, _COL_U, _COL_VA = range(5)


def _proj_prompt_kernel(x_ref, sc_ref, sh_ref, w_ref, lng_ref, lnb_ref, ws_ref, bs_ref, nag_ref,
                        wup_ref, wdn_ref, wout_ref,
                        qhi_ref, qlo_ref, k_ref, v_ref, kbf_ref, vbf_ref, km_ref, gm_ref,
                        wa_ref, wb_ref, wd_ref, wo_ref):
    tm = x_ref.shape[0]
    w_up = wup_ref[...]
    ff_pad = jnp.zeros((w_up.shape[0], D_FF_PAD - D_FF), BF16)
    wa_ref[...] = jnp.concatenate([w_up[:, :D_FF].astype(BF16), ff_pad], axis=1)
    wb_ref[...] = jnp.concatenate([w_up[:, D_FF:].astype(BF16), ff_pad], axis=1)
    dn_row = pl.program_id(0) * wdn_ref.shape[0] + lax.broadcasted_iota(jnp.int32, wdn_ref.shape, 0)
    wd_ref[...] = jnp.where(dn_row < D_FF, wdn_ref[...], 0.0).astype(BF16)
    wo_ref[...] = wout_ref[...].astype(BF16)

    h = (x_ref[...] * (1.0 + sc_ref[...]) + sh_ref[...]).astype(BF16)

    def proj(c):
        return _dot(h, w_ref[:, c * W_B:(c + 1) * W_B])

    def gmlp(u, acc_va):
        va = _layer_norm(jax.nn.gelu(acc_va), lng_ref[...], lnb_ref[...]).astype(BF16)
        n_chunks = tm // CHUNK
        row = lax.broadcasted_iota(jnp.int32, (CHUNK, CHUNK), 0)
        col = lax.broadcasted_iota(jnp.int32, (CHUNK, CHUNK), 1)
        groups = []
        for g in range(N_GROUPS_A):
            gs = slice(g * HEAD_DIM, (g + 1) * HEAD_DIM)
            w_g = jnp.where(row >= col, ws_ref[g], 0.0).astype(BF16)
            v_g = jnp.concatenate([va[c * CHUNK:(c + 1) * CHUNK, gs] for c in range(n_chunks)], axis=1)
            m_g = _dot(w_g, v_g)
            mixed = jnp.concatenate([m_g[:, c * HEAD_DIM:(c + 1) * HEAD_DIM] + bs_ref[g]
                                     for c in range(n_chunks)], axis=0)
            groups.append(u[:, gs] * mixed)
        gm_ref[...] = _rms_norm(jnp.concatenate(groups, axis=1), nag_ref[...]).astype(BF16)

    def put_q(acc):
        q_hi, q_lo = _split_bf16(acc * (LOG2_E * HEAD_DIM ** -0.5))
        qhi_ref[...] = q_hi
        qlo_ref[...] = q_lo

    def put_k(acc):
        k_ref[...] = acc
        kbf_ref[...] = acc.astype(BF16)
        km_ref[...] = jnp.mean(acc.reshape(tm // MOBA_BLOCK, MOBA_BLOCK, W_B), axis=1)

    def put_v(acc):
        v_ref[...] = acc
        vbf_ref[...] = acc.astype(BF16)

    acc_u = proj(_COL_U)
    acc_va = proj(_COL_VA)
    u = jax.nn.gelu(acc_u)
    acc_q = proj(_COL_Q)
    gmlp(u, acc_va)
    acc_k = proj(_COL_K)
    put_q(acc_q)
    acc_v = proj(_COL_V)
    put_k(acc_k)
    put_v(acc_v)


def _proj_prompt(x, mod_p, w_in, ln_g, ln_b, w_s, bs_full, norm_a_g, w_up, w_down, w_out, *, tm, seq):
    rows = x.shape[0]
    tiles_per_seq = seq // tm
    n_steps = rows // tm
    up_rows, dn_rows, out_rows = w_up.shape[0] // n_steps, D_FF_PAD // n_steps, w_out.shape[0] // n_steps
    assert up_rows * n_steps == w_up.shape[0] and dn_rows * n_steps == D_FF_PAD
    assert out_rows * n_steps == w_out.shape[0] and (n_steps - 1) * dn_rows < w_down.shape[0]
    assert up_rows % BF16_ROWS == 0 and dn_rows % BF16_ROWS == 0 and out_rows % BF16_ROWS == 0
    row_blk = lambda i: (i, 0)
    vec = lambda i: (0, 0)
    whole3 = lambda i: (0, 0, 0)
    out_shape = (
        jax.ShapeDtypeStruct((rows, W_B), BF16),
        jax.ShapeDtypeStruct((rows, W_B), BF16),
        jax.ShapeDtypeStruct((rows, W_B), F32),
        jax.ShapeDtypeStruct((rows, W_B), F32),
        jax.ShapeDtypeStruct((rows, W_B), BF16),
        jax.ShapeDtypeStruct((rows, W_B), BF16),
        jax.ShapeDtypeStruct((rows // tm, tm // MOBA_BLOCK, W_B), F32),
        jax.ShapeDtypeStruct((rows, W_A), BF16),
        jax.ShapeDtypeStruct((w_up.shape[0], D_FF_PAD), BF16),
        jax.ShapeDtypeStruct((w_up.shape[0], D_FF_PAD), BF16),
        jax.ShapeDtypeStruct((D_FF_PAD, w_down.shape[1]), BF16),
        jax.ShapeDtypeStruct(w_out.shape, BF16),
    )
    out_specs = (
        pl.BlockSpec((tm, W_B), row_blk), pl.BlockSpec((tm, W_B), row_blk),
        pl.BlockSpec((tm, W_B), row_blk), pl.BlockSpec((tm, W_B), row_blk),
        pl.BlockSpec((tm, W_B), row_blk), pl.BlockSpec((tm, W_B), row_blk),
        pl.BlockSpec((None, tm // MOBA_BLOCK, W_B), lambda i: (i, 0, 0)),
        pl.BlockSpec((tm, W_A), row_blk),
        pl.BlockSpec((up_rows, D_FF_PAD), row_blk), pl.BlockSpec((up_rows, D_FF_PAD), row_blk),
        pl.BlockSpec((dn_rows, w_down.shape[1]), row_blk),
        pl.BlockSpec((out_rows, w_out.shape[1]), row_blk),
    )
    return pl.pallas_call(
        _proj_prompt_kernel,
        grid=(rows // tm,),
        in_specs=[pl.BlockSpec((tm, D_MODEL), row_blk),
                  pl.BlockSpec((None, 1, D_MODEL), lambda i: (i // tiles_per_seq, 0, 1)),
                  pl.BlockSpec((None, 1, D_MODEL), lambda i: (i // tiles_per_seq, 0, 0)),
                  pl.BlockSpec(w_in.shape, vec, pipeline_mode=pl.Buffered(1)),
                  pl.BlockSpec((1, W_A), vec), pl.BlockSpec((1, W_A), vec),
                  pl.BlockSpec((N_GROUPS_A, CHUNK, CHUNK), whole3),
                  pl.BlockSpec((N_GROUPS_A, CHUNK, HEAD_DIM), whole3),
                  pl.BlockSpec((1, W_A), vec),
                  pl.BlockSpec((up_rows, w_up.shape[1]), row_blk),
                  pl.BlockSpec((dn_rows, w_down.shape[1]), row_blk),
                  pl.BlockSpec((out_rows, w_out.shape[1]), row_blk)],
        out_specs=out_specs,
        out_shape=out_shape,
        compiler_params=_params(1),
        name="proj_prompt",
    )(x, mod_p, mod_p, w_in, ln_g, ln_b, w_s, bs_full, norm_a_g, w_up, w_down, w_out)


def _proj_sample_kernel(n_tok, x_ref, sc_ref, sh_ref, w_ref, lng_ref, lnb_ref,
                        q_ref, k_ref, v_ref, u_ref, va_ref, h_scr):
    j = pl.program_id(0)
    n_seq = sc_ref.shape[0]

    @pl.when(j == 0)
    def _():
        for t in range(n_tok):
            x_t = x_ref[:, t * D_MODEL:(t + 1) * D_MODEL]
            h_scr[t * n_seq:(t + 1) * n_seq, :] = (x_t * (1.0 + sc_ref[...]) + sh_ref[...]).astype(BF16)

    acc = _dot(h_scr[...], w_ref[...])

    def put(o_ref, val):
        for t in range(n_tok):
            o_ref[:, t * W_B:(t + 1) * W_B] = val[t * n_seq:(t + 1) * n_seq, :]

    @pl.when(j == _COL_Q)
    def _():
        put(q_ref, acc * (HEAD_DIM ** -0.5))

    @pl.when(j == _COL_K)
    def _():
        put(k_ref, acc)

    @pl.when(j == _COL_V)
    def _():
        put(v_ref, acc)

    @pl.when(j == _COL_U)
    def _():
        put(u_ref, jax.nn.gelu(acc))

    @pl.when(j == _COL_VA)
    def _():
        put(va_ref, _layer_norm(jax.nn.gelu(acc), lng_ref[...], lnb_ref[...]))


def _proj_sample(x2d, mod_s, w_in, ln_g, ln_b, *, n_seq, n_tok):
    n_col = w_in.shape[1] // W_B
    whole = lambda j: (0, 0)
    out = jax.ShapeDtypeStruct((n_seq, n_tok * W_B), F32)
    return pl.pallas_call(
        functools.partial(_proj_sample_kernel, n_tok),
        grid=(n_col,),
        in_specs=[pl.BlockSpec((n_seq, n_tok * D_MODEL), whole),
                  pl.BlockSpec((n_seq, D_MODEL), lambda j: (0, 1)),
                  pl.BlockSpec((n_seq, D_MODEL), lambda j: (0, 0)),
                  pl.BlockSpec((D_MODEL, W_B), lambda j: (0, j)),
                  pl.BlockSpec((1, W_A), whole), pl.BlockSpec((1, W_A), whole)],
        out_specs=tuple(pl.BlockSpec((n_seq, n_tok * W_B), whole) for _ in range(5)),
        out_shape=(out,) * 5,
        scratch_shapes=[pltpu.VMEM((n_tok * n_seq, D_MODEL), BF16)],
        compiler_params=_params(1),
        name="proj_sample",
    )(x2d, mod_s, mod_s, w_in, ln_g, ln_b)


def _attn_prompt_kernel(qhi_ref, qlo_ref, k_ref, v_ref, km_ref, rev_ref, o_ref, bias_scr, ones_scr):
    n_blocks = k_ref.shape[0] // MOBA_BLOCK

    @pl.when(pl.program_id(1) == 0)
    def _():
        ones_scr[...] = (lax.broadcasted_iota(jnp.int32, ones_scr.shape, 1) == 0).astype(BF16)
        row = lax.broadcasted_iota(jnp.int32, (MOBA_BLOCK, MOBA_BLOCK), 0)
        col = lax.broadcasted_iota(jnp.int32, (MOBA_BLOCK, MOBA_BLOCK), 1)
        for d in range(n_blocks):
            r = jnp.broadcast_to(rev_ref[d:d + 1, :], (MOBA_BLOCK, 2 * MOBA_BLOCK))
            tile = pltpu.roll(r, 0, 1, stride=1, stride_axis=0)[:, MOBA_BLOCK:]
            bias_scr[d] = jnp.where(row >= col, tile, NEG_INF) if d == 0 else tile

    km_hi, km_lo = _split_bf16(km_ref[...])

    def rows_of(qb):
        return slice(qb * MOBA_BLOCK, (qb + 1) * MOBA_BLOCK)

    def qk(qb):
        q_hi = qhi_ref[rows_of(qb), :]
        return [_dot_nt(q_hi, k_ref[rows_of(jb), :]) for jb in range(qb + 1)]

    def select(qb):
        q_hi, q_lo = qhi_ref[rows_of(qb), :], qlo_ref[rows_of(qb), :]
        s_t = _dot_nt(km_hi, q_hi) + _dot_nt(km_lo, q_hi) + _dot_nt(km_hi, q_lo)
        return jnp.where(_topk_select(s_t, qb, axis=0) > 0.0, 0.0, NEG_INF).T

    sels = {qb: select(qb) for qb in range(MOBA_TOPK + 1, n_blocks)}

    def softmax(qb, logits):
        sel = sels.get(qb)
        pieces = []
        for jb in range(qb + 1):
            lj = logits[jb] + bias_scr[qb - jb]
            if jb < qb and sel is not None:
                lj = lj + sel[:, jb:jb + 1]
            pieces.append(lj)
        m = jnp.max(functools.reduce(jnp.maximum, pieces), axis=-1, keepdims=True)
        ps = [jnp.exp2(lj - m) for lj in pieces]
        return (jnp.concatenate([pj.astype(BF16) for pj in ps], axis=1),)

    def pv(qb, p):
        n_keys = (qb + 1) * MOBA_BLOCK
        v_ext = jnp.concatenate([v_ref[0:n_keys, :], ones_scr[0:n_keys, :]], axis=1)
        out = _dot(p, v_ext)
        o_ref[rows_of(qb), :] = out[:, 0:HEAD_DIM] / out[:, HEAD_DIM:HEAD_DIM + 1]

    ahead = [qk(t) for t in range(min(QK_AHEAD, n_blocks))]
    pending = []
    for qb in range(n_blocks):
        if qb + QK_AHEAD < n_blocks:
            ahead.append(qk(qb + QK_AHEAD))
        pending.append((qb,) + softmax(qb, ahead.pop(0)))
        if len(pending) > PV_LAG:
            pv(*pending.pop(0))
    for item in pending:
        pv(*item)


def _attn_prompt(q_hi, q_lo, k_bf, v_bf, k_mean, rev_tab, *, n_seq, seq):
    n_blocks = seq // MOBA_BLOCK
    qkv = pl.BlockSpec((seq, HEAD_DIM), lambda h, n: (n, h))
    return pl.pallas_call(
        _attn_prompt_kernel,
        grid=(N_HEADS_B, n_seq),
        in_specs=[qkv, qkv, qkv, qkv,
                  pl.BlockSpec((None, n_blocks, HEAD_DIM), lambda h, n: (n, 0, h)),
                  pl.BlockSpec((None, n_blocks, 2 * MOBA_BLOCK), lambda h, n: (h, 0, 0))],
        out_specs=pl.BlockSpec((seq, HEAD_DIM), lambda h, n: (n, h)),
        out_shape=jax.ShapeDtypeStruct((n_seq * seq, W_B), F32),
        scratch_shapes=[pltpu.VMEM((n_blocks, MOBA_BLOCK, MOBA_BLOCK), F32),
                        pltpu.VMEM((seq, HEAD_DIM), BF16)],
        compiler_params=_params(2),
        name="attn_prompt",
    )(q_hi, q_lo, k_bf, v_bf, k_mean, rev_tab)


def _attn_sample_kernel(n_pages, pt_ref, q_ref, kn_ref, vn_ref, bias_ref, biasn_ref, *refs):
    k_pages = refs[:n_pages]
    v_pages = refs[n_pages:2 * n_pages]
    o_ref = refs[2 * n_pages]
    n_tok = q_ref.shape[0]
    rows = n_tok * N_HEADS_B
    cols = PAGE_SIZE * N_HEADS_B
    pages_per_block = MOBA_BLOCK // PAGE_SIZE
    n_blocks = n_pages // pages_per_block

    q = q_ref[...].reshape(rows, HEAD_DIM)
    q_bf = q.astype(BF16)

    logit_pages = []
    k_sums = []
    for p in range(n_pages):
        kp = k_pages[p][...]
        k_sums.append(jnp.sum(kp, axis=0))
        logit_pages.append(_dot_nt(q_bf, kp.reshape(cols, HEAD_DIM).astype(BF16)))

    blk_lane = lax.broadcasted_iota(jnp.int32, (rows, n_blocks), 1)
    scores = jnp.zeros((rows, n_blocks), F32)
    for b in range(n_blocks):
        km = sum(k_sums[b * pages_per_block:(b + 1) * pages_per_block]) * (1.0 / MOBA_BLOCK)
        s_b = jnp.sum(q * jnp.concatenate([km] * n_tok, axis=0), axis=1, keepdims=True)
        scores = jnp.where(blk_lane == b, s_b, scores)
    sel = _topk_select(scores, n_blocks, axis=1)

    masked = []
    for p in range(n_pages):
        b = p // pages_per_block
        lp = logit_pages[p] + bias_ref[:, p * cols:(p + 1) * cols]
        masked.append(jnp.where(sel[:, b:b + 1] > 0.0, lp, NEG_INF))
    ln = _dot_nt(q_bf, kn_ref[...].reshape(rows, HEAD_DIM).astype(BF16)) + biasn_ref[...]

    m = jnp.maximum(jnp.max(functools.reduce(jnp.maximum, masked), axis=-1, keepdims=True),
                    jnp.max(ln, axis=-1, keepdims=True))
    p_new = jnp.exp(ln - m)
    acc = _dot(p_new.astype(BF16), vn_ref[...].reshape(rows, HEAD_DIM).astype(BF16))
    p_sum = None
    for p in range(n_pages):
        pp = jnp.exp(masked[p] - m)
        p_sum = pp if p_sum is None else p_sum + pp
        acc = acc + _dot(pp.astype(BF16), v_pages[p][...].reshape(cols, HEAD_DIM).astype(BF16))
    z = jnp.sum(p_sum, axis=-1, keepdims=True) + jnp.sum(p_new, axis=-1, keepdims=True)
    o_ref[...] = (acc / z).reshape(n_tok, N_HEADS_B, HEAD_DIM)


def _attn_sample(page_table, q, k_new, v_new, bias_past, bias_new, cache_k, cache_v):
    n_seq, n_pages = page_table.shape
    n_tok = q.shape[1]
    tok = pl.BlockSpec((None, n_tok, N_HEADS_B, HEAD_DIM), lambda n, pt: (n, 0, 0, 0))

    def page_spec(p):
        return pl.BlockSpec((None, None, PAGE_SIZE, N_HEADS_B, HEAD_DIM),
                            lambda n, pt: (0, pt[n * n_pages + p], 0, 0, 0))

    grid_spec = pltpu.PrefetchScalarGridSpec(
        num_scalar_prefetch=1,
        grid=(n_seq,),
        in_specs=[tok, tok, tok,
                  pl.BlockSpec(bias_past.shape, lambda n, pt: (0, 0)),
                  pl.BlockSpec(bias_new.shape, lambda n, pt: (0, 0))]
                 + [page_spec(p) for p in range(n_pages)] * 2,
        out_specs=tok,
    )
    return pl.pallas_call(
        functools.partial(_attn_sample_kernel, n_pages),
        grid_spec=grid_spec,
        out_shape=jax.ShapeDtypeStruct((n_seq, n_tok, N_HEADS_B, HEAD_DIM), F32),
        compiler_params=_params(1),
        name="attn_sample",
    )(page_table.reshape(-1), q, k_new, v_new, bias_past, bias_new,
      *([cache_k] * n_pages), *([cache_v] * n_pages))


def _mix_project(gm_n, attn, wo_ref, nbg_ref):
    a_n = _rms_norm(attn, nbg_ref[...]).astype(BF16)
    return _dot(gm_n, wo_ref[0:W_A, :]) + _dot(a_n, wo_ref[W_A:W_A + W_B, :])


def _mix_finish(x, o, g1, sc2, sh2, l1g_ref, l1b_ref):
    x1 = _layer_norm(ALPHA * x + g1 * o, l1g_ref[...], l1b_ref[...])
    return x1, (x1 * (1.0 + sc2) + sh2).astype(BF16)


def _mix_prompt_kernel(x_ref, gm_ref, at_ref, g1_ref, sc2_ref, sh2_ref, wo_ref, nbg_ref, l1g_ref, l1b_ref,
                       x1_ref, h2_ref):
    tm = x_ref.shape[0]
    parts = [slice(r, r + MIX_ROWS) for r in range(0, tm, MIX_ROWS)]
    outs = [_mix_project(gm_ref[rs, :], at_ref[rs, :], wo_ref, nbg_ref) for rs in parts]
    for rs, o in zip(parts, outs):
        x1_ref[rs, :], h2_ref[rs, :] = _mix_finish(x_ref[rs, :], o, g1_ref[...], sc2_ref[...], sh2_ref[...],
                                                   l1g_ref, l1b_ref)


def _mix_prompt(x, gm_n, attn, mod_p, w_out, norm_b_g, ln1_g, ln1_b, *, tm, seq):
    rows = x.shape[0]
    tiles_per_seq = seq // tm
    row_blk = lambda i: (i, 0)
    vec = lambda i: (0, 0)
    mod = lambda c: pl.BlockSpec((None, 1, D_MODEL), lambda i: (i // tiles_per_seq, 0, c))
    return pl.pallas_call(
        _mix_prompt_kernel,
        grid=(rows // tm,),
        in_specs=[pl.BlockSpec((tm, D_MODEL), row_blk), pl.BlockSpec((tm, W_A), row_blk),
                  pl.BlockSpec((tm, W_B), row_blk), mod(2), mod(4), mod(3),
                  pl.BlockSpec((D_MODEL, D_MODEL), vec), pl.BlockSpec((1, W_B), vec),
                  pl.BlockSpec((1, D_MODEL), vec), pl.BlockSpec((1, D_MODEL), vec)],
        out_specs=(pl.BlockSpec((tm, D_MODEL), row_blk), pl.BlockSpec((tm, D_MODEL), row_blk)),
        out_shape=(jax.ShapeDtypeStruct((rows, D_MODEL), F32), jax.ShapeDtypeStruct((rows, D_MODEL), BF16)),
        compiler_params=_params(1),
        name="mix_prompt",
    )(x, gm_n, attn, mod_p, mod_p, mod_p, w_out, norm_b_g, ln1_g, ln1_b)


def _mix_sample_kernel(x_ref, u_ref, va_ref, ws_ref, bs_ref, nag_ref, at_ref, g1_ref, sc2_ref, sh2_ref,
                       wo_ref, nbg_ref, l1g_ref, l1b_ref, x1_ref, h2_ref):
    n_tok = ws_ref.shape[0]
    mixed = bs_ref[...]
    for s in range(n_tok):
        mixed = mixed + ws_ref[s:s + 1, :] * va_ref[:, s * W_A:(s + 1) * W_A]
    gm_n = _rms_norm(u_ref[...] * mixed, nag_ref[...]).astype(BF16)
    o = _mix_project(gm_n, at_ref[...], wo_ref, nbg_ref)
    x1_ref[...], h2_ref[...] = _mix_finish(x_ref[...], o, g1_ref[...], sc2_ref[...], sh2_ref[...],
                                           l1g_ref, l1b_ref)


def _mix_sample(x2d, u, va, ws_tok, bs_tok, norm_a_g, attn2d, mod_s, w_out, norm_b_g, ln1_g, ln1_b,
                *, n_seq, n_tok):
    tok_blk = lambda t: (0, t)
    vec = lambda t: (0, 0)
    mod = lambda c: pl.BlockSpec((n_seq, D_MODEL), lambda t: (0, c))
    out_blk = pl.BlockSpec((n_seq, D_MODEL), lambda t: (t, 0))
    return pl.pallas_call(
        _mix_sample_kernel,
        grid=(n_tok,),
        in_specs=[pl.BlockSpec((n_seq, D_MODEL), tok_blk), pl.BlockSpec((n_seq, W_A), tok_blk),
                  pl.BlockSpec((n_seq, n_tok * W_A), vec),
                  pl.BlockSpec((None, n_tok, W_A), lambda t: (t, 0, 0)),
                  pl.BlockSpec((None, 1, W_A), lambda t: (t, 0, 0)),
                  pl.BlockSpec((1, W_A), vec),
                  pl.BlockSpec((n_seq, W_B), tok_blk), mod(2), mod(4), mod(3),
                  pl.BlockSpec((D_MODEL, D_MODEL), vec), pl.BlockSpec((1, W_B), vec),
                  pl.BlockSpec((1, D_MODEL), vec), pl.BlockSpec((1, D_MODEL), vec)],
        out_specs=(out_blk, out_blk),
        out_shape=(jax.ShapeDtypeStruct((n_tok * n_seq, D_MODEL), F32),
                   jax.ShapeDtypeStruct((n_tok * n_seq, D_MODEL), BF16)),
        compiler_params=_params(1),
        name="mix_sample",
    )(x2d, u, va, ws_tok, bs_tok, norm_a_g, attn2d, mod_s, mod_s, mod_s, w_out, norm_b_g, ln1_g, ln1_b)


def _ffn_act(conv_a, conv_b):
    return (jax.nn.silu(conv_a) * conv_b).astype(BF16)


def _ffn_prompt_kernel(tiles_per_seq, h_ref, halo_ref, x1_ref, g2_ref, wa_ref, wb_ref, wd_ref, wc_ref, bc_ref,
                       l2g_ref, l2b_ref, y_ref, st_ref, acc_scr, h_scr):
    i = pl.program_id(0)
    j = pl.program_id(1)
    tm = h_ref.shape[0]
    n_halo = halo_ref.shape[0]

    @pl.when(j == 0)
    def _():
        seq_start = (i % tiles_per_seq) == 0
        h_scr[0:n_halo, :] = jnp.where(seq_start, jnp.zeros_like(halo_ref[...]), halo_ref[...])
        h_scr[n_halo:n_halo + tm, :] = h_ref[...]
        acc_scr[...] = jnp.zeros_like(acc_scr)

    n_parts = tm // FFN_ROWS
    assert n_parts * FFN_ROWS == tm

    def up_proj(r, prev):
        if prev is None:
            h = h_scr[0:n_halo + FFN_ROWS, :]
            return _dot(h, wa_ref[...]), _dot(h, wb_ref[...])
        h = h_scr[n_halo + r * FFN_ROWS:n_halo + (r + 1) * FFN_ROWS, :]
        return tuple(jnp.concatenate([p[FFN_ROWS:, :], _dot(h, w_ref[...])], axis=0)
                     for p, w_ref in zip(prev, (wa_ref, wb_ref)))

    def conv(half, up, last):
        if last:
            st_ref[half] = up[n_halo + FFN_ROWS - SUBLANES:n_halo + FFN_ROWS, :]
        out = bc_ref[half:half + 1, :]
        for c in range(CONV_W):
            lag = CONV_W - 1 - c
            out = out + up[n_halo - lag:n_halo - lag + FFN_ROWS, :] * wc_ref[half, c:c + 1, :]
        return out

    def down(r, act):
        acc_scr[r * FFN_ROWS:(r + 1) * FFN_ROWS, :] += _dot(act, wd_ref[...])

    ups = up_proj(0, None)
    for r in range(n_parts):
        nxt = up_proj(r + 1, ups) if r + 1 < n_parts else None
        act = _ffn_act(conv(0, ups[0], r == n_parts - 1), conv(1, ups[1], r == n_parts - 1))
        down(r, act)
        ups = nxt

    @pl.when(j == pl.num_programs(1) - 1)
    def _():
        y_ref[...] = _layer_norm(ALPHA * x1_ref[...] + g2_ref[...] * acc_scr[...], l2g_ref[...], l2b_ref[...])


def _ffn_prompt(h2, x1, mod_p, w_up_a, w_up_b, w_down, w_conv, b_conv, ln2_g, ln2_b, *, tm, seq):
    rows = h2.shape[0]
    tiles_per_seq = seq // tm
    row_blk = lambda i, j: (i, 0)
    vec = lambda i, j: (0, 0)
    halo_blk = lambda i, j: (jnp.maximum(i * (tm // BF16_ROWS) - 1, 0), 0)
    return pl.pallas_call(
        functools.partial(_ffn_prompt_kernel, tiles_per_seq),
        grid=(rows // tm, N_FF_TILES),
        in_specs=[pl.BlockSpec((tm, D_MODEL), row_blk), pl.BlockSpec((BF16_ROWS, D_MODEL), halo_blk),
                  pl.BlockSpec((tm, D_MODEL), row_blk),
                  pl.BlockSpec((None, 1, D_MODEL), lambda i, j: (i // tiles_per_seq, 0, 5)),
                  pl.BlockSpec((D_MODEL, FF_TILE), lambda i, j: (0, j)),
                  pl.BlockSpec((D_MODEL, FF_TILE), lambda i, j: (0, j)),
                  pl.BlockSpec((FF_TILE, D_MODEL), lambda i, j: (j, 0)),
                  pl.BlockSpec((2, CONV_W, FF_TILE), lambda i, j: (0, 0, j)),
                  pl.BlockSpec((2, FF_TILE), lambda i, j: (0, j)),
                  pl.BlockSpec((1, D_MODEL), vec), pl.BlockSpec((1, D_MODEL), vec)],
        out_specs=(pl.BlockSpec((tm, D_MODEL), row_blk),
                   pl.BlockSpec((None, 2, SUBLANES, FF_TILE), lambda i, j: (i, 0, 0, j))),
        out_shape=(jax.ShapeDtypeStruct((rows, D_MODEL), F32),
                   jax.ShapeDtypeStruct((rows // tm, 2, SUBLANES, D_FF_PAD), F32)),
        scratch_shapes=[pltpu.VMEM((tm, D_MODEL), F32), pltpu.VMEM((BF16_ROWS + tm, D_MODEL), BF16)],
        compiler_params=_params(2),
        name="ffn_prompt",
    )(h2, h2, x1, mod_p, w_up_a, w_up_b, w_down, w_conv, b_conv, ln2_g, ln2_b)


def _ffn_sample_kernel(n_tok, h_ref, x1_ref, g2_ref, wa_ref, wb_ref, wd_ref, wc_ref, bc_ref, past_ref,
                       l2g_ref, l2b_ref, y_ref, st_ref, acc_scr):
    j = pl.program_id(0)
    n_seq = g2_ref.shape[0]
    h = h_ref[...]

    def conv(half, w_ref):
        up = _dot(h, w_ref[...])
        full = [past_ref[r, half] for r in range(CONV_W - 1)]
        full += [up[t * n_seq:(t + 1) * n_seq, :] for t in range(n_tok)]
        for r in range(CONV_W - 1):
            st_ref[r, half] = full[n_tok + r]
        w = wc_ref[half]
        b = bc_ref[half:half + 1, :]
        return jnp.concatenate(
            [b + sum(full[t + c] * w[c:c + 1, :] for c in range(CONV_W)) for t in range(n_tok)], axis=0)

    @pl.when(j == 0)
    def _():
        acc_scr[...] = jnp.zeros_like(acc_scr)

    acc_scr[...] += _dot(_ffn_act(conv(0, wa_ref), conv(1, wb_ref)), wd_ref[...])

    @pl.when(j == pl.num_programs(0) - 1)
    def _():
        g2 = g2_ref[...]
        for t in range(n_tok):
            rs = slice(t * n_seq, (t + 1) * n_seq)
            y_ref[:, t * D_MODEL:(t + 1) * D_MODEL] = _layer_norm(
                ALPHA * x1_ref[rs, :] + g2 * acc_scr[rs, :], l2g_ref[...], l2b_ref[...])


def _ffn_sample(h2, x1, mod_s, w_up_a, w_up_b, w_down, w_conv, b_conv, past, ln2_g, ln2_b, *, n_seq, n_tok):
    rows = n_seq * n_tok
    full = lambda j: (0, 0)
    state_blk = pl.BlockSpec((CONV_W - 1, 2, n_seq, FF_TILE), lambda j: (0, 0, 0, j))
    return pl.pallas_call(
        functools.partial(_ffn_sample_kernel, n_tok),
        grid=(N_FF_TILES,),
        in_specs=[pl.BlockSpec((rows, D_MODEL), full), pl.BlockSpec((rows, D_MODEL), full),
                  pl.BlockSpec((n_seq, D_MODEL), lambda j: (0, 5)),
                  pl.BlockSpec((D_MODEL, FF_TILE), lambda j: (0, j)),
                  pl.BlockSpec((D_MODEL, FF_TILE), lambda j: (0, j)),
                  pl.BlockSpec((FF_TILE, D_MODEL), lambda j: (j, 0)),
                  pl.BlockSpec((2, CONV_W, FF_TILE), lambda j: (0, 0, j)),
                  pl.BlockSpec((2, FF_TILE), lambda j: (0, j)),
                  state_blk,
                  pl.BlockSpec((1, D_MODEL), full), pl.BlockSpec((1, D_MODEL), full)],
        out_specs=(pl.BlockSpec((n_seq, n_tok * D_MODEL), full), state_blk),
        out_shape=(jax.ShapeDtypeStruct((n_seq, n_tok * D_MODEL), F32),
                   jax.ShapeDtypeStruct((CONV_W - 1, 2, n_seq, D_FF_PAD), F32)),
        scratch_shapes=[pltpu.VMEM((rows, D_MODEL), F32)],
        compiler_params=_params(1),
        name="ffn_sample",
    )(h2, x1, mod_s, w_up_a, w_up_b, w_down, w_conv, b_conv, past, ln2_g, ln2_b)


def _t5_bucket_table(n_rel):
    max_exact = NUM_BUCKETS // 2
    ratio = MAX_DISTANCE // max_exact
    log_ratio = ratio.bit_length() - 1
    assert 1 << log_ratio == ratio
    steps = NUM_BUCKETS - max_exact
    out = np.zeros((n_rel,), np.int32)
    for n in range(n_rel):
        if n < max_exact:
            out[n] = n
            continue
        k = 0
        while k + 1 < steps and n ** steps >= (max_exact ** steps) << (log_ratio * (k + 1)):
            k += 1
        out[n] = min(max_exact + k, NUM_BUCKETS - 1)
    return out


def _bias_tables(rel_bias, seq, past, n_tok):
    n_rel = max(seq, past + n_tok)
    onehot = np.zeros((NUM_BUCKETS, n_rel), np.float32)
    onehot[_t5_bucket_table(n_rel), np.arange(n_rel)] = 1.0
    tab = jnp.dot(rel_bias.astype(F32).T, onehot, precision=lax.Precision.HIGHEST)
    n_blocks = seq // MOBA_BLOCK
    tab_pad = jnp.pad(tab, ((0, 0), (MOBA_BLOCK, 1)))
    rev = jnp.stack([tab_pad[:, d * MOBA_BLOCK + 1:(d + 2) * MOBA_BLOCK + 1][:, ::-1]
                     for d in range(n_blocks)], axis=1)
    same_head = np.eye(N_HEADS_B, dtype=bool)
    by_key = jnp.stack([tab[:, t + 1:t + 1 + past][:, ::-1] for t in range(n_tok)], axis=0)
    n_pages = past // PAGE_SIZE
    spread = np.repeat(np.eye(PAGE_SIZE, dtype=np.float32), N_HEADS_B, axis=1)
    by_col = jnp.einsum('rpk,kc->rpc', by_key.reshape(n_tok * N_HEADS_B, n_pages, PAGE_SIZE), spread,
                        precision=lax.Precision.HIGHEST).reshape(n_tok * N_HEADS_B, past * N_HEADS_B)
    row_head = np.tile(np.arange(N_HEADS_B), n_tok)[:, None]
    col_head = np.tile(np.arange(N_HEADS_B), past)[None, :]
    bias_past = jnp.where(row_head == col_head, by_col, NEG_INF)
    new_rows = []
    for t in range(n_tok):
        by_tok = jnp.stack([tab[:, max(t - s, 0)] for s in range(n_tok)], axis=1)
        ok = same_head[:, None, :] & (np.arange(n_tok) <= t)[None, :, None]
        new_rows.append(jnp.where(ok, by_tok[:, :, None], NEG_INF).reshape(N_HEADS_B, n_tok * N_HEADS_B))
    bias_new = jnp.stack(new_rows, axis=0).reshape(n_tok * N_HEADS_B, n_tok * N_HEADS_B)
    return rev * LOG2_E, bias_past, bias_new


def _pad_ff(a, axis):
    pad = [(0, 0)] * a.ndim
    pad[axis] = (0, D_FF_PAD - D_FF)
    return jnp.pad(a, pad)


def kernel(x_prompt, x_sample, cache_k, cache_v, state_ffn_conv, page_table, c_prompt, c_sample,
           w_ada, b_ada, w_in, ln_sgu_g, ln_sgu_b, w_s, b_s, rel_bias, norm_a_g, norm_b_g, w_out,
           ln1_g, ln1_b, w_up, w_conv, b_conv, w_down, ln2_g, ln2_b):
    n_p, seq, _ = x_prompt.shape
    n_s, n_tok, _ = x_sample.shape
    n_pages = page_table.shape[1]
    past = n_pages * PAGE_SIZE
    assert w_ada.shape[0] == 1 and seq % MOBA_BLOCK == 0 and past % MOBA_BLOCK == 0 and n_tok <= CHUNK
    assert n_s % SUBLANES == 0

    w_in_bf = w_in[0].astype(BF16)
    w_conv_p = _pad_ff(jnp.transpose(w_conv[0].reshape(CONV_W, 2, D_FF), (1, 0, 2)), 2)
    b_conv_p = _pad_ff(b_conv[0].reshape(2, D_FF), 1)
    row = lambda a: a.reshape(1, -1)
    ln_g, ln_b = row(ln_sgu_g[0]), row(ln_sgu_b[0])
    nag, nbg = row(norm_a_g[0]), row(norm_b_g[0])
    l1g, l1b, l2g, l2b = row(ln1_g[0]), row(ln1_b[0]), row(ln2_g[0]), row(ln2_b[0])
    bs_full = jnp.broadcast_to(b_s[0][:, :, None], (N_GROUPS_A, CHUNK, HEAD_DIM))
    tril_tok = np.tril(np.ones((n_tok, n_tok), np.float32))
    ws_tok = jnp.repeat(jnp.transpose(w_s[0][:, :n_tok, :n_tok] * tril_tok, (1, 2, 0)), HEAD_DIM, axis=2)
    bs_tok = jnp.repeat(b_s[0][:, :n_tok].T, HEAD_DIM, axis=1).reshape(n_tok, 1, W_A)
    rev_tab, bias_past, bias_new = _bias_tables(rel_bias, seq, past, n_tok)

    pad_rows = (-(n_s + n_p)) % SUBLANES
    c_all = jnp.concatenate([c_sample, c_prompt, jnp.zeros((pad_rows, D_MODEL), F32)], axis=0)
    mod = _ada(c_all, w_ada[0], b_ada[0])
    mod_s = mod
    mod_p = mod[n_s:n_s + n_p].reshape(n_p, 1, 6 * D_MODEL)

    xp = x_prompt.reshape(n_p * seq, D_MODEL)
    q_hi, q_lo, k_p, v_p, k_bf, v_bf, k_mean, gm_n, w_up_a, w_up_b, w_down_bf, w_out_bf = _proj_prompt(
        xp, mod_p, w_in_bf, ln_g, ln_b, w_s[0], bs_full, nag, w_up[0], w_down[0], w_out[0], tm=PROJ_TILE, seq=seq)
    attn_p = _attn_prompt(q_hi, q_lo, k_bf, v_bf, k_mean.reshape(n_p, seq // MOBA_BLOCK, W_B), rev_tab,
                          n_seq=n_p, seq=seq)
    x1_p, h2_p = _mix_prompt(xp, gm_n, attn_p, mod_p, w_out_bf, nbg, l1g, l1b, tm=ROW_TILE, seq=seq)
    y_p, st_p = _ffn_prompt(h2_p, x1_p, mod_p, w_up_a, w_up_b, w_down_bf, w_conv_p, b_conv_p, l2g, l2b,
                            tm=ROW_TILE, seq=seq)
    tiles_per_seq = st_p.shape[0] // n_p
    conv_p = st_p[tiles_per_seq - 1::tiles_per_seq, :, SUBLANES - (CONV_W - 1):, :D_FF]
    conv_p = jnp.transpose(conv_p, (0, 2, 1, 3)).reshape(1, n_p, CONV_W - 1, 2 * D_FF)

    xs = x_sample.reshape(n_s, n_tok * D_MODEL)
    q_s, k_s, v_s, u_s, va_s = _proj_sample(xs, mod_s, w_in_bf, ln_g, ln_b, n_seq=n_s, n_tok=n_tok)
    heads = lambda a: a.reshape(n_s, n_tok, N_HEADS_B, HEAD_DIM)
    attn_s = _attn_sample(page_table, heads(q_s), heads(k_s), heads(v_s), bias_past, bias_new, cache_k, cache_v)
    x1_s, h2_s = _mix_sample(xs, u_s, va_s, ws_tok, bs_tok, nag, attn_s.reshape(n_s, n_tok * W_B), mod_s,
                             w_out_bf, nbg, l1g, l1b, n_seq=n_s, n_tok=n_tok)
    past_up = state_ffn_conv[0].reshape(n_s, CONV_W - 1, 2, D_FF)
    past_up = _pad_ff(jnp.transpose(past_up, (1, 2, 0, 3)), 3)
    y_s, st_s = _ffn_sample(h2_s, x1_s, mod_s, w_up_a, w_up_b, w_down_bf, w_conv_p, b_conv_p, past_up,
                            l2g, l2b, n_seq=n_s, n_tok=n_tok)
    conv_s = jnp.transpose(st_s[..., :D_FF], (2, 0, 1, 3)).reshape(1, n_s, CONV_W - 1, 2 * D_FF)

    heads_p = (1, n_p, seq, N_HEADS_B, HEAD_DIM)
    heads_s = (1, n_s, n_tok, N_HEADS_B, HEAD_DIM)
    return (y_p.reshape(n_p, seq, D_MODEL), y_s.reshape(n_s, n_tok, D_MODEL),
            k_p.reshape(heads_p), v_p.reshape(heads_p), conv_p,
            k_s.reshape(heads_s), v_s.reshape(heads_s), va_s.reshape(1, n_s, n_tok, W_A), conv_s)
```

```python
import functools

import numpy as np
import jax
import jax.numpy as jnp
from jax import lax
from jax.experimental import pallas as pl
from jax.experimental.pallas import tpu as pltpu

F32 = jnp.float32
BF16 = jnp.bfloat16

D_MODEL = 2048
HEAD_DIM = 128
N_HEADS_B = 8
N_GROUPS_A = 8
W_B = N_HEADS_B * HEAD_DIM
W_A = N_GROUPS_A * HEAD_DIM
CHUNK = 128
MOBA_BLOCK = 256
MOBA_TOPK = 3
NUM_BUCKETS = 32
MAX_DISTANCE = 1024
CONV_W = 3
D_FF = 5504
PAGE_SIZE = 128
ALPHA = 2.0 ** 0.25
LN_EPS = 1e-5
NEG_INF = -1e30
LOG2_E = 1.4426950408889634

FF_TILE = 512
D_FF_PAD = ((D_FF + FF_TILE - 1) // FF_TILE) * FF_TILE
N_FF_TILES = D_FF_PAD // FF_TILE
SUBLANES = 8
BF16_ROWS = 16
ADA_COLS = 1024
PROJ_TILE = 256
ROW_TILE = 512
FFN_ROWS = 256
QK_AHEAD = 1
PV_LAG = 1
MIX_ROWS = 256
VMEM_LIMIT_BYTES = 56 * 1024 * 1024

_NT = (((1,), (1,)), ((), ()))


def _params(n_axes):
    return pltpu.CompilerParams(dimension_semantics=("arbitrary",) * n_axes,
                                vmem_limit_bytes=VMEM_LIMIT_BYTES)


def _layer_norm(x, g, b):
    mu = jnp.mean(x, axis=-1, keepdims=True)
    xc = x - mu
    var = jnp.mean(xc * xc, axis=-1, keepdims=True)
    return xc * lax.rsqrt(var + LN_EPS) * g + b


def _rms_norm(x, g):
    return x * lax.rsqrt(jnp.mean(x * x, axis=-1, keepdims=True) + LN_EPS) * g


def _split_bf16(x):
    hi = x.astype(BF16)
    lo = (x - hi.astype(F32)).astype(BF16)
    return hi, lo


def _dot(a, b):
    return jnp.dot(a, b, preferred_element_type=F32)


def _dot_nt(a, b):
    return lax.dot_general(a, b, _NT, preferred_element_type=F32)


def _topk_select(s, n_valid, axis):
    idx = lax.broadcasted_iota(jnp.int32, s.shape, axis)
    rank = jnp.zeros(s.shape, F32)
    for i in range(n_valid):
        si = lax.slice_in_dim(s, i, i + 1, axis=axis)
        beats = (si > s) | ((si == s) & (i < idx))
        rank = rank + beats.astype(F32)
    return ((rank < MOBA_TOPK) & (idx < n_valid)).astype(F32)


def _ada_kernel(c_ref, w_ref, b_ref, o_ref):
    c = c_ref[...]
    a = jax.nn.silu(c).astype(BF16)
    o_ref[...] = _dot(a, w_ref[...].astype(BF16)) + b_ref[...]


def _ada(c_all, w_ada, b_ada):
    rows = c_all.shape[0]
    n_out = w_ada.shape[1]
    tn = ADA_COLS
    return pl.pallas_call(
        _ada_kernel,
        grid=(n_out // tn,),
        in_specs=[pl.BlockSpec((rows, D_MODEL), lambda j: (0, 0)),
                  pl.BlockSpec((D_MODEL, tn), lambda j: (0, j)),
                  pl.BlockSpec((1, tn), lambda j: (0, j))],
        out_specs=pl.BlockSpec((rows, tn), lambda j: (0, j)),
        out_shape=jax.ShapeDtypeStruct((rows, n_out), F32),
        compiler_params=_params(1),
        name="ada",
    )(c_all, w_ada, b_ada.reshape(1, n_out))


_COL_Q, _COL_K, _COL_V, _COL_U, _COL_VA = range(5)


def _proj_prompt_kernel(x_ref, sc_ref, sh_ref, w_ref, lng_ref, lnb_ref, ws_ref, bs_ref, nag_ref,
                        wup_ref, wdn_ref, wout_ref,
                        qhi_ref, qlo_ref, k_ref, v_ref, kbf_ref, vbf_ref, km_ref, gm_ref,
                        wa_ref, wb_ref, wd_ref, wo_ref):
    tm = x_ref.shape[0]
    w_up = wup_ref[...]
    ff_pad = jnp.zeros((w_up.shape[0], D_FF_PAD - D_FF), BF16)
    wa_ref[...] = jnp.concatenate([w_up[:, :D_FF].astype(BF16), ff_pad], axis=1)
    wb_ref[...] = jnp.concatenate([w_up[:, D_FF:].astype(BF16), ff_pad], axis=1)
    dn_row = pl.program_id(0) * wdn_ref.shape[0] + lax.broadcasted_iota(jnp.int32, wdn_ref.shape, 0)
    wd_ref[...] = jnp.where(dn_row < D_FF, wdn_ref[...], 0.0).astype(BF16)
    wo_ref[...] = wout_ref[...].astype(BF16)

    h = (x_ref[...] * (1.0 + sc_ref[...]) + sh_ref[...]).astype(BF16)

    def proj(c):
        return _dot(h, w_ref[:, c * W_B:(c + 1) * W_B])

    def gmlp(u, acc_va):
        va = _layer_norm(jax.nn.gelu(acc_va), lng_ref[...], lnb_ref[...]).astype(BF16)
        n_chunks = tm // CHUNK
        row = lax.broadcasted_iota(jnp.int32, (CHUNK, CHUNK), 0)
        col = lax.broadcasted_iota(jnp.int32, (CHUNK, CHUNK), 1)
        groups = []
        for g in range(N_GROUPS_A):
            gs = slice(g * HEAD_DIM, (g + 1) * HEAD_DIM)
            w_g = jnp.where(row >= col, ws_ref[g], 0.0).astype(BF16)
            v_g = jnp.concatenate([va[c * CHUNK:(c + 1) * CHUNK, gs] for c in range(n_chunks)], axis=1)
            m_g = _dot(w_g, v_g)
            mixed = jnp.concatenate([m_g[:, c * HEAD_DIM:(c + 1) * HEAD_DIM] + bs_ref[g]
                                     for c in range(n_chunks)], axis=0)
            groups.append(u[:, gs] * mixed)
        gm_ref[...] = _rms_norm(jnp.concatenate(groups, axis=1), nag_ref[...]).astype(BF16)

    def put_q(acc):
        q_hi, q_lo = _split_bf16(acc * (LOG2_E * HEAD_DIM ** -0.5))
        qhi_ref[...] = q_hi
        qlo_ref[...] = q_lo

    def put_k(acc):
        k_ref[...] = acc
        kbf_ref[...] = acc.astype(BF16)
        km_ref[...] = jnp.mean(acc.reshape(tm // MOBA_BLOCK, MOBA_BLOCK, W_B), axis=1)

    def put_v(acc):
        v_ref[...] = acc
        vbf_ref[...] = acc.astype(BF16)

    acc_u = proj(_COL_U)
    acc_va = proj(_COL_VA)
    u = jax.nn.gelu(acc_u)
    acc_q = proj(_COL_Q)
    gmlp(u, acc_va)
    acc_k = proj(_COL_K)
    put_q(acc_q)
    acc_v = proj(_COL_V)
    put_k(acc_k)
    put_v(acc_v)


def _proj_prompt(x, mod_p, w_in, ln_g, ln_b, w_s, bs_full, norm_a_g, w_up, w_down, w_out, *, tm, seq):
    rows = x.shape[0]
    tiles_per_seq = seq // tm
    n_steps = rows // tm
    up_rows, dn_rows, out_rows = w_up.shape[0] // n_steps, D_FF_PAD // n_steps, w_out.shape[0] // n_steps
    assert up_rows * n_steps == w_up.shape[0] and dn_rows * n_steps == D_FF_PAD
    assert out_rows * n_steps == w_out.shape[0] and (n_steps - 1) * dn_rows < w_down.shape[0]
    assert up_rows % BF16_ROWS == 0 and dn_rows % BF16_ROWS == 0 and out_rows % BF16_ROWS == 0
    row_blk = lambda i: (i, 0)
    vec = lambda i: (0, 0)
    whole3 = lambda i: (0, 0, 0)
    out_shape = (
        jax.ShapeDtypeStruct((rows, W_B), BF16),
        jax.ShapeDtypeStruct((rows, W_B), BF16),
        jax.ShapeDtypeStruct((rows, W_B), F32),
        jax.ShapeDtypeStruct((rows, W_B), F32),
        jax.ShapeDtypeStruct((rows, W_B), BF16),
        jax.ShapeDtypeStruct((rows, W_B), BF16),
        jax.ShapeDtypeStruct((rows // tm, tm // MOBA_BLOCK, W_B), F32),
        jax.ShapeDtypeStruct((rows, W_A), BF16),
        jax.ShapeDtypeStruct((w_up.shape[0], D_FF_PAD), BF16),
        jax.ShapeDtypeStruct((w_up.shape[0], D_FF_PAD), BF16),
        jax.ShapeDtypeStruct((D_FF_PAD, w_down.shape[1]), BF16),
        jax.ShapeDtypeStruct(w_out.shape, BF16),
    )
    out_specs = (
        pl.BlockSpec((tm, W_B), row_blk), pl.BlockSpec((tm, W_B), row_blk),
        pl.BlockSpec((tm, W_B), row_blk), pl.BlockSpec((tm, W_B), row_blk),
        pl.BlockSpec((tm, W_B), row_blk), pl.BlockSpec((tm, W_B), row_blk),
        pl.BlockSpec((None, tm // MOBA_BLOCK, W_B), lambda i: (i, 0, 0)),
        pl.BlockSpec((tm, W_A), row_blk),
        pl.BlockSpec((up_rows, D_FF_PAD), row_blk), pl.BlockSpec((up_rows, D_FF_PAD), row_blk),
        pl.BlockSpec((dn_rows, w_down.shape[1]), row_blk),
        pl.BlockSpec((out_rows, w_out.shape[1]), row_blk),
    )
    return pl.pallas_call(
        _proj_prompt_kernel,
        grid=(rows // tm,),
        in_specs=[pl.BlockSpec((tm, D_MODEL), row_blk),
                  pl.BlockSpec((None, 1, D_MODEL), lambda i: (i // tiles_per_seq, 0, 1)),
                  pl.BlockSpec((None, 1, D_MODEL), lambda i: (i // tiles_per_seq, 0, 0)),
                  pl.BlockSpec(w_in.shape, vec, pipeline_mode=pl.Buffered(1)),
                  pl.BlockSpec((1, W_A), vec), pl.BlockSpec((1, W_A), vec),
                  pl.BlockSpec((N_GROUPS_A, CHUNK, CHUNK), whole3),
                  pl.BlockSpec((N_GROUPS_A, CHUNK, HEAD_DIM), whole3),
                  pl.BlockSpec((1, W_A), vec),
                  pl.BlockSpec((up_rows, w_up.shape[1]), row_blk),
                  pl.BlockSpec((dn_rows, w_down.shape[1]), row_blk),
                  pl.BlockSpec((out_rows, w_out.shape[1]), row_blk)],
        out_specs=out_specs,
        out_shape=out_shape,
        compiler_params=_params(1),
        name="proj_prompt",
    )(x, mod_p, mod_p, w_in, ln_g, ln_b, w_s, bs_full, norm_a_g, w_up, w_down, w_out)


def _proj_sample_kernel(n_tok, x_ref, sc_ref, sh_ref, w_ref, lng_ref, lnb_ref,
                        q_ref, k_ref, v_ref, u_ref, va_ref, h_scr):
    j = pl.program_id(0)
    n_seq = sc_ref.shape[0]

    @pl.when(j == 0)
    def _():
        for t in range(n_tok):
            x_t = x_ref[:, t * D_MODEL:(t + 1) * D_MODEL]
            h_scr[t * n_seq:(t + 1) * n_seq, :] = (x_t * (1.0 + sc_ref[...]) + sh_ref[...]).astype(BF16)

    acc = _dot(h_scr[...], w_ref[...])

    def put(o_ref, val):
        for t in range(n_tok):
            o_ref[:, t * W_B:(t + 1) * W_B] = val[t * n_seq:(t + 1) * n_seq, :]

    @pl.when(j == _COL_Q)
    def _():
        put(q_ref, acc * (HEAD_DIM ** -0.5))

    @pl.when(j == _COL_K)
    def _():
        put(k_ref, acc)

    @pl.when(j == _COL_V)
    def _():
        put(v_ref, acc)

    @pl.when(j == _COL_U)
    def _():
        put(u_ref, jax.nn.gelu(acc))

    @pl.when(j == _COL_VA)
    def _():
        put(va_ref, _layer_norm(jax.nn.gelu(acc), lng_ref[...], lnb_ref[...]))


def _proj_sample(x2d, mod_s, w_in, ln_g, ln_b, *, n_seq, n_tok):
    n_col = w_in.shape[1] // W_B
    whole = lambda j: (0, 0)
    out = jax.ShapeDtypeStruct((n_seq, n_tok * W_B), F32)
    return pl.pallas_call(
        functools.partial(_proj_sample_kernel, n_tok),
        grid=(n_col,),
        in_specs=[pl.BlockSpec((n_seq, n_tok * D_MODEL), whole),
                  pl.BlockSpec((n_seq, D_MODEL), lambda j: (0, 1)),
                  pl.BlockSpec((n_seq, D_MODEL), lambda j: (0, 0)),
                  pl.BlockSpec((D_MODEL, W_B), lambda j: (0, j)),
                  pl.BlockSpec((1, W_A), whole), pl.BlockSpec((1, W_A), whole)],
        out_specs=tuple(pl.BlockSpec((n_seq, n_tok * W_B), whole) for _ in range(5)),
        out_shape=(out,) * 5,
        scratch_shapes=[pltpu.VMEM((n_tok * n_seq, D_MODEL), BF16)],
        compiler_params=_params(1),
        name="proj_sample",
    )(x2d, mod_s, mod_s, w_in, ln_g, ln_b)


def _attn_prompt_kernel(qhi_ref, qlo_ref, k_ref, v_ref, km_ref, rev_ref, o_ref, bias_scr, ones_scr, blk_scr):
    n_blocks = k_ref.shape[0] // MOBA_BLOCK

    @pl.when(pl.program_id(1) == 0)
    def _():
        lane = lax.broadcasted_iota(jnp.int32, ones_scr.shape, 1)
        ones_scr[...] = (lane == 0).astype(BF16)
        blk_scr[...] = (lane == lax.broadcasted_iota(jnp.int32, blk_scr.shape, 0) // MOBA_BLOCK).astype(BF16)
        row = lax.broadcasted_iota(jnp.int32, (MOBA_BLOCK, MOBA_BLOCK), 0)
        col = lax.broadcasted_iota(jnp.int32, (MOBA_BLOCK, MOBA_BLOCK), 1)
        for d in range(n_blocks):
            r = jnp.broadcast_to(rev_ref[d:d + 1, :], (MOBA_BLOCK, 2 * MOBA_BLOCK))
            tile = pltpu.roll(r, 0, 1, stride=1, stride_axis=0)[:, MOBA_BLOCK:]
            bias_scr[d] = jnp.where(row >= col, tile, NEG_INF) if d == 0 else tile

    km_hi, km_lo = _split_bf16(km_ref[...])

    def rows_of(qb):
        return slice(qb * MOBA_BLOCK, (qb + 1) * MOBA_BLOCK)

    def qk(qb):
        q = qhi_ref[rows_of(qb), :]
        if qb not in sels:
            return [_dot_nt(q, k_ref[rows_of(jb), :]) for jb in range(qb + 1)]
        q = jnp.concatenate([q, sels[qb]], axis=1)
        return [_dot_nt(q, jnp.concatenate([k_ref[rows_of(jb), :], blk_scr[rows_of(jb), :]], axis=1))
                for jb in range(qb + 1)]

    def select(qb):
        q_hi, q_lo = qhi_ref[rows_of(qb), :], qlo_ref[rows_of(qb), :]
        s_t = _dot_nt(km_hi, q_hi) + _dot_nt(km_lo, q_hi) + _dot_nt(km_hi, q_lo)
        blk = lax.broadcasted_iota(jnp.int32, s_t.shape, 0)
        neg = jnp.where((_topk_select(s_t, qb, axis=0) > 0.0) | (blk >= qb), 0.0, NEG_INF)
        pad = jnp.zeros((HEAD_DIM - n_blocks, MOBA_BLOCK), F32)
        return jnp.concatenate([neg, pad], axis=0).T.astype(BF16)

    sels = {qb: select(qb) for qb in range(MOBA_TOPK + 1, n_blocks)}

    def softmax(qb, logits):
        pieces = []
        for jb in range(qb + 1):
            lj = logits[jb] + bias_scr[qb - jb]
            pieces.append(lj)
        m = jnp.max(functools.reduce(jnp.maximum, pieces), axis=-1, keepdims=True)
        ps = [jnp.exp2(lj - m) for lj in pieces]
        return (jnp.concatenate([pj.astype(BF16) for pj in ps], axis=1),)

    def pv(qb, p):
        n_keys = (qb + 1) * MOBA_BLOCK
        v_ext = jnp.concatenate([v_ref[0:n_keys, :], ones_scr[0:n_keys, :]], axis=1)
        out = _dot(p, v_ext)
        o_ref[rows_of(qb), :] = out[:, 0:HEAD_DIM] / out[:, HEAD_DIM:HEAD_DIM + 1]

    ahead = [qk(t) for t in range(min(QK_AHEAD, n_blocks))]
    pending = []
    for qb in range(n_blocks):
        if qb + QK_AHEAD < n_blocks:
            ahead.append(qk(qb + QK_AHEAD))
        pending.append((qb,) + softmax(qb, ahead.pop(0)))
        if len(pending) > PV_LAG:
            pv(*pending.pop(0))
    for item in pending:
        pv(*item)


def _attn_prompt(q_hi, q_lo, k_bf, v_bf, k_mean, rev_tab, *, n_seq, seq):
    n_blocks = seq // MOBA_BLOCK
    qkv = pl.BlockSpec((seq, HEAD_DIM), lambda h, n: (n, h))
    return pl.pallas_call(
        _attn_prompt_kernel,
        grid=(N_HEADS_B, n_seq),
        in_specs=[qkv, qkv, qkv, qkv,
                  pl.BlockSpec((None, n_blocks, HEAD_DIM), lambda h, n: (n, 0, h)),
                  pl.BlockSpec((None, n_blocks, 2 * MOBA_BLOCK), lambda h, n: (h, 0, 0))],
        out_specs=pl.BlockSpec((seq, HEAD_DIM), lambda h, n: (n, h)),
        out_shape=jax.ShapeDtypeStruct((n_seq * seq, W_B), F32),
        scratch_shapes=[pltpu.VMEM((n_blocks, MOBA_BLOCK, MOBA_BLOCK), F32),
                        pltpu.VMEM((seq, HEAD_DIM), BF16), pltpu.VMEM((seq, HEAD_DIM), BF16)],
        compiler_params=_params(2),
        name="attn_prompt",
    )(q_hi, q_lo, k_bf, v_bf, k_mean, rev_tab)


def _attn_sample_kernel(n_pages, pt_ref, q_ref, kn_ref, vn_ref, bias_ref, biasn_ref, *refs):
    k_pages = refs[:n_pages]
    v_pages = refs[n_pages:2 * n_pages]
    o_ref = refs[2 * n_pages]
    n_tok = q_ref.shape[0]
    rows = n_tok * N_HEADS_B
    cols = PAGE_SIZE * N_HEADS_B
    pages_per_block = MOBA_BLOCK // PAGE_SIZE
    n_blocks = n_pages // pages_per_block

    q = q_ref[...].reshape(rows, HEAD_DIM)
    q_bf = q.astype(BF16)

    logit_pages = []
    k_sums = []
    for p in range(n_pages):
        kp = k_pages[p][...]
        k_sums.append(jnp.sum(kp, axis=0))
        logit_pages.append(_dot_nt(q_bf, kp.reshape(cols, HEAD_DIM).astype(BF16)))

    blk_lane = lax.broadcasted_iota(jnp.int32, (rows, n_blocks), 1)
    scores = jnp.zeros((rows, n_blocks), F32)
    for b in range(n_blocks):
        km = sum(k_sums[b * pages_per_block:(b + 1) * pages_per_block]) * (1.0 / MOBA_BLOCK)
        s_b = jnp.sum(q * jnp.concatenate([km] * n_tok, axis=0), axis=1, keepdims=True)
        scores = jnp.where(blk_lane == b, s_b, scores)
    sel = _topk_select(scores, n_blocks, axis=1)

    masked = []
    for p in range(n_pages):
        b = p // pages_per_block
        lp = logit_pages[p] + bias_ref[:, p * cols:(p + 1) * cols]
        masked.append(jnp.where(sel[:, b:b + 1] > 0.0, lp, NEG_INF))
    ln = _dot_nt(q_bf, kn_ref[...].reshape(rows, HEAD_DIM).astype(BF16)) + biasn_ref[...]

    m = jnp.maximum(jnp.max(functools.reduce(jnp.maximum, masked), axis=-1, keepdims=True),
                    jnp.max(ln, axis=-1, keepdims=True))
    p_new = jnp.exp(ln - m)
    acc = _dot(p_new.astype(BF16), vn_ref[...].reshape(rows, HEAD_DIM).astype(BF16))
    p_sum = None
    for p in range(n_pages):
        pp = jnp.exp(masked[p] - m)
        p_sum = pp if p_sum is None else p_sum + pp
        acc = acc + _dot(pp.astype(BF16), v_pages[p][...].reshape(cols, HEAD_DIM).astype(BF16))
    z = jnp.sum(p_sum, axis=-1, keepdims=True) + jnp.sum(p_new, axis=-1, keepdims=True)
    o_ref[...] = (acc / z).reshape(n_tok, N_HEADS_B, HEAD_DIM)


def _attn_sample(page_table, q, k_new, v_new, bias_past, bias_new, cache_k, cache_v):
    n_seq, n_pages = page_table.shape
    n_tok = q.shape[1]
    tok = pl.BlockSpec((None, n_tok, N_HEADS_B, HEAD_DIM), lambda n, pt: (n, 0, 0, 0))

    def page_spec(p):
        return pl.BlockSpec((None, None, PAGE_SIZE, N_HEADS_B, HEAD_DIM),
                            lambda n, pt: (0, pt[n * n_pages + p], 0, 0, 0))

    grid_spec = pltpu.PrefetchScalarGridSpec(
        num_scalar_prefetch=1,
        grid=(n_seq,),
        in_specs=[tok, tok, tok,
                  pl.BlockSpec(bias_past.shape, lambda n, pt: (0, 0)),
                  pl.BlockSpec(bias_new.shape, lambda n, pt: (0, 0))]
                 + [page_spec(p) for p in range(n_pages)] * 2,
        out_specs=tok,
    )
    return pl.pallas_call(
        functools.partial(_attn_sample_kernel, n_pages),
        grid_spec=grid_spec,
        out_shape=jax.ShapeDtypeStruct((n_seq, n_tok, N_HEADS_B, HEAD_DIM), F32),
        compiler_params=_params(1),
        name="attn_sample",
    )(page_table.reshape(-1), q, k_new, v_new, bias_past, bias_new,
      *([cache_k] * n_pages), *([cache_v] * n_pages))


def _mix_project(gm_n, attn, wo_ref, nbg_ref):
    a_n = _rms_norm(attn, nbg_ref[...]).astype(BF16)
    return _dot(gm_n, wo_ref[0:W_A, :]) + _dot(a_n, wo_ref[W_A:W_A + W_B, :])


def _mix_finish(x, o, g1, sc2, sh2, l1g_ref, l1b_ref):
    x1 = _layer_norm(ALPHA * x + g1 * o, l1g_ref[...], l1b_ref[...])
    return x1, (x1 * (1.0 + sc2) + sh2).astype(BF16)


def _mix_prompt_kernel(x_ref, gm_ref, at_ref, g1_ref, sc2_ref, sh2_ref, wo_ref, nbg_ref, l1g_ref, l1b_ref,
                       x1_ref, h2_ref):
    tm = x_ref.shape[0]
    parts = [slice(r, r + MIX_ROWS) for r in range(0, tm, MIX_ROWS)]
    outs = [_mix_project(gm_ref[rs, :], at_ref[rs, :], wo_ref, nbg_ref) for rs in parts]
    for rs, o in zip(parts, outs):
        x1_ref[rs, :], h2_ref[rs, :] = _mix_finish(x_ref[rs, :], o, g1_ref[...], sc2_ref[...], sh2_ref[...],
                                                   l1g_ref, l1b_ref)


def _mix_prompt(x, gm_n, attn, mod_p, w_out, norm_b_g, ln1_g, ln1_b, *, tm, seq):
    rows = x.shape[0]
    tiles_per_seq = seq // tm
    row_blk = lambda i: (i, 0)
    vec = lambda i: (0, 0)
    mod = lambda c: pl.BlockSpec((None, 1, D_MODEL), lambda i: (i // tiles_per_seq, 0, c))
    return pl.pallas_call(
        _mix_prompt_kernel,
        grid=(rows // tm,),
        in_specs=[pl.BlockSpec((tm, D_MODEL), row_blk), pl.BlockSpec((tm, W_A), row_blk),
                  pl.BlockSpec((tm, W_B), row_blk), mod(2), mod(4), mod(3),
                  pl.BlockSpec((D_MODEL, D_MODEL), vec), pl.BlockSpec((1, W_B), vec),
                  pl.BlockSpec((1, D_MODEL), vec), pl.BlockSpec((1, D_MODEL), vec)],
        out_specs=(pl.BlockSpec((tm, D_MODEL), row_blk), pl.BlockSpec((tm, D_MODEL), row_blk)),
        out_shape=(jax.ShapeDtypeStruct((rows, D_MODEL), F32), jax.ShapeDtypeStruct((rows, D_MODEL), BF16)),
        compiler_params=_params(1),
        name="mix_prompt",
    )(x, gm_n, attn, mod_p, mod_p, mod_p, w_out, norm_b_g, ln1_g, ln1_b)


def _mix_sample_kernel(x_ref, u_ref, va_ref, ws_ref, bs_ref, nag_ref, at_ref, g1_ref, sc2_ref, sh2_ref,
                       wo_ref, nbg_ref, l1g_ref, l1b_ref, x1_ref, h2_ref):
    n_tok = ws_ref.shape[0]
    mixed = bs_ref[...]
    for s in range(n_tok):
        mixed = mixed + ws_ref[s:s + 1, :] * va_ref[:, s * W_A:(s + 1) * W_A]
    gm_n = _rms_norm(u_ref[...] * mixed, nag_ref[...]).astype(BF16)
    o = _mix_project(gm_n, at_ref[...], wo_ref, nbg_ref)
    x1_ref[...], h2_ref[...] = _mix_finish(x_ref[...], o, g1_ref[...], sc2_ref[...], sh2_ref[...],
                                           l1g_ref, l1b_ref)


def _mix_sample(x2d, u, va, ws_tok, bs_tok, norm_a_g, attn2d, mod_s, w_out, norm_b_g, ln1_g, ln1_b,
                *, n_seq, n_tok):
    tok_blk = lambda t: (0, t)
    vec = lambda t: (0, 0)
    mod = lambda c: pl.BlockSpec((n_seq, D_MODEL), lambda t: (0, c))
    out_blk = pl.BlockSpec((n_seq, D_MODEL), lambda t: (t, 0))
    return pl.pallas_call(
        _mix_sample_kernel,
        grid=(n_tok,),
        in_specs=[pl.BlockSpec((n_seq, D_MODEL), tok_blk), pl.BlockSpec((n_seq, W_A), tok_blk),
                  pl.BlockSpec((n_seq, n_tok * W_A), vec),
                  pl.BlockSpec((None, n_tok, W_A), lambda t: (t, 0, 0)),
                  pl.BlockSpec((None, 1, W_A), lambda t: (t, 0, 0)),
                  pl.BlockSpec((1, W_A), vec),
                  pl.BlockSpec((n_seq, W_B), tok_blk), mod(2), mod(4), mod(3),
                  pl.BlockSpec((D_MODEL, D_MODEL), vec), pl.BlockSpec((1, W_B), vec),
                  pl.BlockSpec((1, D_MODEL), vec), pl.BlockSpec((1, D_MODEL), vec)],
        out_specs=(out_blk, out_blk),
        out_shape=(jax.ShapeDtypeStruct((n_tok * n_seq, D_MODEL), F32),
                   jax.ShapeDtypeStruct((n_tok * n_seq, D_MODEL), BF16)),
        compiler_params=_params(1),
        name="mix_sample",
    )(x2d, u, va, ws_tok, bs_tok, norm_a_g, attn2d, mod_s, mod_s, mod_s, w_out, norm_b_g, ln1_g, ln1_b)


def _ffn_act(conv_a, conv_b):
    return (jax.nn.silu(conv_a) * conv_b).astype(BF16)


def _ffn_prompt_kernel(tiles_per_seq, h_ref, halo_ref, x1_ref, g2_ref, wa_ref, wb_ref, wd_ref, wc_ref, bc_ref,
                       l2g_ref, l2b_ref, y_ref, st_ref, acc_scr, h_scr):
    i = pl.program_id(0)
    j = pl.program_id(1)
    tm = h_ref.shape[0]
    n_halo = halo_ref.shape[0]

    @pl.when(j == 0)
    def _():
        seq_start = (i % tiles_per_seq) == 0
        h_scr[0:n_halo, :] = jnp.where(seq_start, jnp.zeros_like(halo_ref[...]), halo_ref[...])
        h_scr[n_halo:n_halo + tm, :] = h_ref[...]
        acc_scr[...] = jnp.zeros_like(acc_scr)

    n_parts = tm // FFN_ROWS
    assert n_parts * FFN_ROWS == tm

    def up_proj(r, prev):
        if prev is None:
            h = h_scr[0:n_halo + FFN_ROWS, :]
            return _dot(h, wa_ref[...]), _dot(h, wb_ref[...])
        h = h_scr[n_halo + r * FFN_ROWS:n_halo + (r + 1) * FFN_ROWS, :]
        return tuple(jnp.concatenate([p[FFN_ROWS:, :], _dot(h, w_ref[...])], axis=0)
                     for p, w_ref in zip(prev, (wa_ref, wb_ref)))

    def conv(half, up, last):
        if last:
            st_ref[half] = up[n_halo + FFN_ROWS - SUBLANES:n_halo + FFN_ROWS, :]
        out = bc_ref[half:half + 1, :]
        for c in range(CONV_W):
            lag = CONV_W - 1 - c
            out = out + up[n_halo - lag:n_halo - lag + FFN_ROWS, :] * wc_ref[half, c:c + 1, :]
        return out

    def down(r, act):
        acc_scr[r * FFN_ROWS:(r + 1) * FFN_ROWS, :] += _dot(act, wd_ref[...])

    ups = up_proj(0, None)
    for r in range(n_parts):
        nxt = up_proj(r + 1, ups) if r + 1 < n_parts else None
        act = _ffn_act(conv(0, ups[0], r == n_parts - 1), conv(1, ups[1], r == n_parts - 1))
        down(r, act)
        ups = nxt

    @pl.when(j == pl.num_programs(1) - 1)
    def _():
        y_ref[...] = _layer_norm(ALPHA * x1_ref[...] + g2_ref[...] * acc_scr[...], l2g_ref[...], l2b_ref[...])


def _ffn_prompt(h2, x1, mod_p, w_up_a, w_up_b, w_down, w_conv, b_conv, ln2_g, ln2_b, *, tm, seq):
    rows = h2.shape[0]
    tiles_per_seq = seq // tm
    row_blk = lambda i, j: (i, 0)
    vec = lambda i, j: (0, 0)
    halo_blk = lambda i, j: (jnp.maximum(i * (tm // BF16_ROWS) - 1, 0), 0)
    return pl.pallas_call(
        functools.partial(_ffn_prompt_kernel, tiles_per_seq),
        grid=(rows // tm, N_FF_TILES),
        in_specs=[pl.BlockSpec((tm, D_MODEL), row_blk), pl.BlockSpec((BF16_ROWS, D_MODEL), halo_blk),
                  pl.BlockSpec((tm, D_MODEL), row_blk),
                  pl.BlockSpec((None, 1, D_MODEL), lambda i, j: (i // tiles_per_seq, 0, 5)),
                  pl.BlockSpec((D_MODEL, FF_TILE), lambda i, j: (0, j)),
                  pl.BlockSpec((D_MODEL, FF_TILE), lambda i, j: (0, j)),
                  pl.BlockSpec((FF_TILE, D_MODEL), lambda i, j: (j, 0)),
                  pl.BlockSpec((2, CONV_W, FF_TILE), lambda i, j: (0, 0, j)),
                  pl.BlockSpec((2, FF_TILE), lambda i, j: (0, j)),
                  pl.BlockSpec((1, D_MODEL), vec), pl.BlockSpec((1, D_MODEL), vec)],
        out_specs=(pl.BlockSpec((tm, D_MODEL), row_blk),
                   pl.BlockSpec((None, 2, SUBLANES, FF_TILE), lambda i, j: (i, 0, 0, j))),
        out_shape=(jax.ShapeDtypeStruct((rows, D_MODEL), F32),
                   jax.ShapeDtypeStruct((rows // tm, 2, SUBLANES, D_FF_PAD), F32)),
        scratch_shapes=[pltpu.VMEM((tm, D_MODEL), F32), pltpu.VMEM((BF16_ROWS + tm, D_MODEL), BF16)],
        compiler_params=_params(2),
        name="ffn_prompt",
    )(h2, h2, x1, mod_p, w_up_a, w_up_b, w_down, w_conv, b_conv, ln2_g, ln2_b)


def _ffn_sample_kernel(n_tok, h_ref, x1_ref, g2_ref, wa_ref, wb_ref, wd_ref, wc_ref, bc_ref, past_ref,
                       l2g_ref, l2b_ref, y_ref, st_ref, acc_scr):
    j = pl.program_id(0)
    n_seq = g2_ref.shape[0]
    h = h_ref[...]

    def conv(half, w_ref):
        up = _dot(h, w_ref[...])
        full = [past_ref[r, half] for r in range(CONV_W - 1)]
        full += [up[t * n_seq:(t + 1) * n_seq, :] for t in range(n_tok)]
        for r in range(CONV_W - 1):
            st_ref[r, half] = full[n_tok + r]
        w = wc_ref[half]
        b = bc_ref[half:half + 1, :]
        return jnp.concatenate(
            [b + sum(full[t + c] * w[c:c + 1, :] for c in range(CONV_W)) for t in range(n_tok)], axis=0)

    @pl.when(j == 0)
    def _():
        acc_scr[...] = jnp.zeros_like(acc_scr)

    acc_scr[...] += _dot(_ffn_act(conv(0, wa_ref), conv(1, wb_ref)), wd_ref[...])

    @pl.when(j == pl.num_programs(0) - 1)
    def _():
        g2 = g2_ref[...]
        for t in range(n_tok):
            rs = slice(t * n_seq, (t + 1) * n_seq)
            y_ref[:, t * D_MODEL:(t + 1) * D_MODEL] = _layer_norm(
                ALPHA * x1_ref[rs, :] + g2 * acc_scr[rs, :], l2g_ref[...], l2b_ref[...])


def _ffn_sample(h2, x1, mod_s, w_up_a, w_up_b, w_down, w_conv, b_conv, past, ln2_g, ln2_b, *, n_seq, n_tok):
    rows = n_seq * n_tok
    full = lambda j: (0, 0)
    state_blk = pl.BlockSpec((CONV_W - 1, 2, n_seq, FF_TILE), lambda j: (0, 0, 0, j))
    return pl.pallas_call(
        functools.partial(_ffn_sample_kernel, n_tok),
        grid=(N_FF_TILES,),
        in_specs=[pl.BlockSpec((rows, D_MODEL), full), pl.BlockSpec((rows, D_MODEL), full),
                  pl.BlockSpec((n_seq, D_MODEL), lambda j: (0, 5)),
                  pl.BlockSpec((D_MODEL, FF_TILE), lambda j: (0, j)),
                  pl.BlockSpec((D_MODEL, FF_TILE), lambda j: (0, j)),
                  pl.BlockSpec((FF_TILE, D_MODEL), lambda j: (j, 0)),
                  pl.BlockSpec((2, CONV_W, FF_TILE), lambda j: (0, 0, j)),
                  pl.BlockSpec((2, FF_TILE), lambda j: (0, j)),
                  state_blk,
                  pl.BlockSpec((1, D_MODEL), full), pl.BlockSpec((1, D_MODEL), full)],
        out_specs=(pl.BlockSpec((n_seq, n_tok * D_MODEL), full), state_blk),
        out_shape=(jax.ShapeDtypeStruct((n_seq, n_tok * D_MODEL), F32),
                   jax.ShapeDtypeStruct((CONV_W - 1, 2, n_seq, D_FF_PAD), F32)),
        scratch_shapes=[pltpu.VMEM((rows, D_MODEL), F32)],
        compiler_params=_params(1),
        name="ffn_sample",
    )(h2, x1, mod_s, w_up_a, w_up_b, w_down, w_conv, b_conv, past, ln2_g, ln2_b)


def _t5_bucket_table(n_rel):
    max_exact = NUM_BUCKETS // 2
    ratio = MAX_DISTANCE // max_exact
    log_ratio = ratio.bit_length() - 1
    assert 1 << log_ratio == ratio
    steps = NUM_BUCKETS - max_exact
    out = np.zeros((n_rel,), np.int32)
    for n in range(n_rel):
        if n < max_exact:
            out[n] = n
            continue
        k = 0
        while k + 1 < steps and n ** steps >= (max_exact ** steps) << (log_ratio * (k + 1)):
            k += 1
        out[n] = min(max_exact + k, NUM_BUCKETS - 1)
    return out


def _bias_tables(rel_bias, seq, past, n_tok):
    n_rel = max(seq, past + n_tok)
    onehot = np.zeros((NUM_BUCKETS, n_rel), np.float32)
    onehot[_t5_bucket_table(n_rel), np.arange(n_rel)] = 1.0
    tab = jnp.dot(rel_bias.astype(F32).T, onehot, precision=lax.Precision.HIGHEST)
    n_blocks = seq // MOBA_BLOCK
    tab_pad = jnp.pad(tab, ((0, 0), (MOBA_BLOCK, 1)))
    rev = jnp.stack([tab_pad[:, d * MOBA_BLOCK + 1:(d + 2) * MOBA_BLOCK + 1][:, ::-1]
                     for d in range(n_blocks)], axis=1)
    same_head = np.eye(N_HEADS_B, dtype=bool)
    by_key = jnp.stack([tab[:, t + 1:t + 1 + past][:, ::-1] for t in range(n_tok)], axis=0)
    n_pages = past // PAGE_SIZE
    spread = np.repeat(np.eye(PAGE_SIZE, dtype=np.float32), N_HEADS_B, axis=1)
    by_col = jnp.einsum('rpk,kc->rpc', by_key.reshape(n_tok * N_HEADS_B, n_pages, PAGE_SIZE), spread,
                        precision=lax.Precision.HIGHEST).reshape(n_tok * N_HEADS_B, past * N_HEADS_B)
    row_head = np.tile(np.arange(N_HEADS_B), n_tok)[:, None]
    col_head = np.tile(np.arange(N_HEADS_B), past)[None, :]
    bias_past = jnp.where(row_head == col_head, by_col, NEG_INF)
    new_rows = []
    for t in range(n_tok):
        by_tok = jnp.stack([tab[:, max(t - s, 0)] for s in range(n_tok)], axis=1)
        ok = same_head[:, None, :] & (np.arange(n_tok) <= t)[None, :, None]
        new_rows.append(jnp.where(ok, by_tok[:, :, None], NEG_INF).reshape(N_HEADS_B, n_tok * N_HEADS_B))
    bias_new = jnp.stack(new_rows, axis=0).reshape(n_tok * N_HEADS_B, n_tok * N_HEADS_B)
    return rev * LOG2_E, bias_past, bias_new


def _pad_ff(a, axis):
    pad = [(0, 0)] * a.ndim
    pad[axis] = (0, D_FF_PAD - D_FF)
    return jnp.pad(a, pad)


def kernel(x_prompt, x_sample, cache_k, cache_v, state_ffn_conv, page_table, c_prompt, c_sample,
           w_ada, b_ada, w_in, ln_sgu_g, ln_sgu_b, w_s, b_s, rel_bias, norm_a_g, norm_b_g, w_out,
           ln1_g, ln1_b, w_up, w_conv, b_conv, w_down, ln2_g, ln2_b):
    n_p, seq, _ = x_prompt.shape
    n_s, n_tok, _ = x_sample.shape
    n_pages = page_table.shape[1]
    past = n_pages * PAGE_SIZE
    assert w_ada.shape[0] == 1 and seq % MOBA_BLOCK == 0 and past % MOBA_BLOCK == 0 and n_tok <= CHUNK
    assert n_s % SUBLANES == 0

    w_in_bf = w_in[0].astype(BF16)
    w_conv_p = _pad_ff(jnp.transpose(w_conv[0].reshape(CONV_W, 2, D_FF), (1, 0, 2)), 2)
    b_conv_p = _pad_ff(b_conv[0].reshape(2, D_FF), 1)
    row = lambda a: a.reshape(1, -1)
    ln_g, ln_b = row(ln_sgu_g[0]), row(ln_sgu_b[0])
    nag, nbg = row(norm_a_g[0]), row(norm_b_g[0])
    l1g, l1b, l2g, l2b = row(ln1_g[0]), row(ln1_b[0]), row(ln2_g[0]), row(ln2_b[0])
    bs_full = jnp.broadcast_to(b_s[0][:, :, None], (N_GROUPS_A, CHUNK, HEAD_DIM))
    tril_tok = np.tril(np.ones((n_tok, n_tok), np.float32))
    ws_tok = jnp.repeat(jnp.transpose(w_s[0][:, :n_tok, :n_tok] * tril_tok, (1, 2, 0)), HEAD_DIM, axis=2)
    bs_tok = jnp.repeat(b_s[0][:, :n_tok].T, HEAD_DIM, axis=1).reshape(n_tok, 1, W_A)
    rev_tab, bias_past, bias_new = _bias_tables(rel_bias, seq, past, n_tok)

    pad_rows = (-(n_s + n_p)) % SUBLANES
    c_all = jnp.concatenate([c_sample, c_prompt, jnp.zeros((pad_rows, D_MODEL), F32)], axis=0)
    mod = _ada(c_all, w_ada[0], b_ada[0])
    mod_s = mod
    mod_p = mod[n_s:n_s + n_p].reshape(n_p, 1, 6 * D_MODEL)

    xp = x_prompt.reshape(n_p * seq, D_MODEL)
    q_hi, q_lo, k_p, v_p, k_bf, v_bf, k_mean, gm_n, w_up_a, w_up_b, w_down_bf, w_out_bf = _proj_prompt(
        xp, mod_p, w_in_bf, ln_g, ln_b, w_s[0], bs_full, nag, w_up[0], w_down[0], w_out[0], tm=PROJ_TILE, seq=seq)
    attn_p = _attn_prompt(q_hi, q_lo, k_bf, v_bf, k_mean.reshape(n_p, seq // MOBA_BLOCK, W_B), rev_tab,
                          n_seq=n_p, seq=seq)
    x1_p, h2_p = _mix_prompt(xp, gm_n, attn_p, mod_p, w_out_bf, nbg, l1g, l1b, tm=ROW_TILE, seq=seq)
    y_p, st_p = _ffn_prompt(h2_p, x1_p, mod_p, w_up_a, w_up_b, w_down_bf, w_conv_p, b_conv_p, l2g, l2b,
                            tm=ROW_TILE, seq=seq)
    tiles_per_seq = st_p.shape[0] // n_p
    conv_p = st_p[tiles_per_seq - 1::tiles_per_seq, :, SUBLANES - (CONV_W - 1):, :D_FF]
    conv_p = jnp.transpose(conv_p, (0, 2, 1, 3)).reshape(1, n_p, CONV_W - 1, 2 * D_FF)

    xs = x_sample.reshape(n_s, n_tok * D_MODEL)
    q_s, k_s, v_s, u_s, va_s = _proj_sample(xs, mod_s, w_in_bf, ln_g, ln_b, n_seq=n_s, n_tok=n_tok)
    heads = lambda a: a.reshape(n_s, n_tok, N_HEADS_B, HEAD_DIM)
    attn_s = _attn_sample(page_table, heads(q_s), heads(k_s), heads(v_s), bias_past, bias_new, cache_k, cache_v)
    x1_s, h2_s = _mix_sample(xs, u_s, va_s, ws_tok, bs_tok, nag, attn_s.reshape(n_s, n_tok * W_B), mod_s,
                             w_out_bf, nbg, l1g, l1b, n_seq=n_s, n_tok=n_tok)
    past_up = state_ffn_conv[0].reshape(n_s, CONV_W - 1, 2, D_FF)
    past_up = _pad_ff(jnp.transpose(past_up, (1, 2, 0, 3)), 3)
    y_s, st_s = _ffn_sample(h2_s, x1_s, mod_s, w_up_a, w_up_b, w_down_bf, w_conv_p, b_conv_p, past_up,
                            l2g, l2b, n_seq=n_s, n_tok=n_tok)
    conv_s = jnp.transpose(st_s[..., :D_FF], (2, 0, 1, 3)).reshape(1, n_s, CONV_W - 1, 2 * D_FF)

    heads_p = (1, n_p, seq, N_HEADS_B, HEAD_DIM)
    heads_s = (1, n_s, n_tok, N_HEADS_B, HEAD_DIM)
    return (y_p.reshape(n_p, seq, D_MODEL), y_s.reshape(n_s, n_tok, D_MODEL),
            k_p.reshape(heads_p), v_p.reshape(heads_p), conv_p,
            k_s.reshape(heads_s), v_s.reshape(heads_s), va_s.reshape(1, n_s, n_tok, W_A), conv_s)
```
